```python
import math
import jax
import jax.numpy as jnp
from jax import lax
import numpy as np

D_MODEL = 1024
BATCH = 1
SEQ = 16384
DEPTH = 1

N_META = 16
BLOCK = 128
WINDOW = 128
ATTN_PAD = BLOCK - N_META
ATTN_WIDTH = D_MODEL // 2
HEAD_DIM = 64
N_Q_HEADS = ATTN_WIDTH // HEAD_DIM
N_KV_HEADS = 2
GQA = N_Q_HEADS // N_KV_HEADS
KV_WIDTH = N_KV_HEADS * HEAD_DIM
SSM_WIDTH = D_MODEL - ATTN_WIDTH
SSM_CH = 16
SSM_GROUPS = SSM_WIDTH // SSM_CH
SSM_STATE = 64
DT_MIN = 0.001
DT_MAX = 0.1
IN_WIDTH = ATTN_WIDTH + 2 * KV_WIDTH + SSM_WIDTH
PEER_HEADS = 8
PEER_N_KEYS = 128
PEER_EXPERTS = PEER_N_KEYS * PEER_N_KEYS
PEER_TOPK = 16
PEER_DK = 256
PEER_HALF = PEER_DK // 2
PEER_CHUNK = 256

NORM_EPS = 1e-6
MASK_VALUE = -1e30

kernel_name = "hymba_swa_s5_peer_block"


def rms_norm(x, g):
    xf = x.astype(jnp.float32)
    y = xf * lax.rsqrt(jnp.mean(xf * xf, axis=-1, keepdims=True) + NORM_EPS)
    return (y * g.astype(jnp.float32)).astype(x.dtype)


def sliding_window_attention(q, k, v, q_norm_g, k_norm_g, sinks):
    B, L, _ = q.shape
    q = rms_norm(q.reshape(B, L, N_Q_HEADS, HEAD_DIM), q_norm_g).astype(jnp.float32)
    k = rms_norm(k.reshape(B, L, N_KV_HEADS, HEAD_DIM), k_norm_g).astype(jnp.float32)
    v = v.reshape(B, L, N_KV_HEADS, HEAD_DIM).astype(jnp.float32)
    k_meta, v_meta = k[:, :N_META], v[:, :N_META]
    pad = ((0, 0), (ATTN_PAD, 0), (0, 0), (0, 0))
    nb = (L + ATTN_PAD) // BLOCK
    qb = jnp.pad(q, pad).reshape(B, nb, BLOCK, N_KV_HEADS, GQA, HEAD_DIM)
    kb = jnp.pad(k, pad).reshape(B, nb, BLOCK, N_KV_HEADS, HEAD_DIM)
    vb = jnp.pad(v, pad).reshape(B, nb, BLOCK, N_KV_HEADS, HEAD_DIM)

    def with_prev(t):
        prev = jnp.concatenate([jnp.zeros_like(t[:, :1]), t[:, :-1]], axis=1)
        return jnp.concatenate([prev, t], axis=2)

    kb2, vb2 = with_prev(kb), with_prev(vb)
    scale = HEAD_DIM ** -0.5
    s_band = jnp.einsum('bnqhgd,bnkhd->bnhgqk', qb, kb2) * scale
    s_meta = jnp.einsum('bnqhgd,bmhd->bnhgqm', qb, k_meta) * scale

    qpos = jnp.arange(nb)[:, None] * BLOCK + jnp.arange(BLOCK)[None, :]
    kpos = jnp.arange(nb)[:, None] * BLOCK - BLOCK + jnp.arange(2 * BLOCK)[None]
    diff = qpos[:, :, None] - kpos[:, None, :]
    band_ok = (diff >= 0) & (diff < WINDOW) & (kpos[:, None, :] >= BLOCK)
    meta_ok = (ATTN_PAD + jnp.arange(N_META))[None, None, :] <= qpos[:, :, None]
    s_band = jnp.where(band_ok[None, :, None, None], s_band, MASK_VALUE)
    s_meta = jnp.where(meta_ok[None, :, None, None], s_meta, MASK_VALUE)
    sink = jnp.broadcast_to(sinks.astype(jnp.float32).reshape(1, 1, N_KV_HEADS, GQA, 1, 1),
                            s_band.shape[:-1] + (1,))
    p = jax.nn.softmax(jnp.concatenate([s_band, s_meta, sink], axis=-1), axis=-1)
    out = (jnp.einsum('bnhgqk,bnkhd->bnqhgd', p[..., :2 * BLOCK], vb2)
           + jnp.einsum('bnhgqm,bmhd->bnqhgd', p[..., 2 * BLOCK:2 * BLOCK + N_META], v_meta))
    out = out.reshape(B, nb * BLOCK, ATTN_WIDTH)[:, ATTN_PAD:]
    return out.astype(q.dtype)


def _complex_linear_combine(e1, e2):
    a1r, a1i, b1r, b1i = e1
    a2r, a2i, b2r, b2i = e2
    return (a2r * a1r - a2i * a1i,
            a2r * a1i + a2i * a1r,
            a2r * b1r - a2i * b1i + b2r,
            a2r * b1i + a2i * b1r + b2i)


def s5_ssm(u, a_re, a_im, log_dt, b_re, b_im, c_re, c_im, d, glu_w, glu_b):
    B, L, _ = u.shape
    uf = u.astype(jnp.float32).reshape(B, L, SSM_GROUPS, SSM_CH)
    dt = jnp.exp(log_dt.astype(jnp.float32))[:, None]
    lr, li = a_re.astype(jnp.float32), a_im.astype(jnp.float32)
    mag = jnp.exp(lr * dt)
    abar_r, abar_i = mag * jnp.cos(li * dt), mag * jnp.sin(li * dt)
    den = lr * lr + li * li
    nr, ni = abar_r - 1.0, abar_i
    coef_r = (nr * lr + ni * li) / den
    coef_i = (ni * lr - nr * li) / den
    br, bi = b_re.astype(jnp.float32), b_im.astype(jnp.float32)
    bbar_r = coef_r[..., None] * br - coef_i[..., None] * bi
    bbar_i = coef_r[..., None] * bi + coef_i[..., None] * br
    bu_r = jnp.einsum('blgp,gnp->blgn', uf, bbar_r)
    bu_i = jnp.einsum('blgp,gnp->blgn', uf, bbar_i)
    a_r = jnp.broadcast_to(abar_r[None, None], bu_r.shape)
    a_i = jnp.broadcast_to(abar_i[None, None], bu_i.shape)
    _, _, s_r, s_i = lax.associative_scan(_complex_linear_combine, (a_r, a_i, bu_r, bu_i), axis=1)
    y = (jnp.einsum('blgn,gpn->blgp', s_r, c_re.astype(jnp.float32))
         - jnp.einsum('blgn,gpn->blgp', s_i, c_im.astype(jnp.float32))
         + d.astype(jnp.float32).reshape(SSM_GROUPS, SSM_CH) * uf)
    z = jax.nn.gelu(y.reshape(B, L, SSM_WIDTH), approximate=False)
    z = z * jax.nn.sigmoid(z @ glu_w.astype(jnp.float32) + glu_b.astype(jnp.float32))
    return z.astype(u.dtype)


def peer_ffn(h, w_query, sub_keys, peer_u, peer_v):
    B, L, D = h.shape
    T = B * L
    n_chunks = -(-T // PEER_CHUNK)
    tok = jnp.pad(h.reshape(T, D), ((0, n_chunks * PEER_CHUNK - T), (0, 0)))
    tok = tok.reshape(n_chunks, PEER_CHUNK, D)
    keys1 = sub_keys[:, 0].astype(jnp.float32)
    keys2 = sub_keys[:, 1].astype(jnp.float32)

    def chunk_fn(hc):
        q = (hc @ w_query).astype(jnp.float32).reshape(PEER_CHUNK, PEER_HEADS, PEER_DK)
        s1 = jnp.einsum('chd,hkd->chk', q[..., :PEER_HALF], keys1)
        s2 = jnp.einsum('chd,hkd->chk', q[..., PEER_HALF:], keys2)
        v1, i1 = lax.top_k(s1, PEER_TOPK)
        v2, i2 = lax.top_k(s2, PEER_TOPK)
        cand = (v1[..., :, None] + v2[..., None, :]).reshape(PEER_CHUNK, PEER_HEADS, PEER_TOPK * PEER_TOPK)
        sc, ci = lax.top_k(cand, PEER_TOPK)
        idx = (jnp.take_along_axis(i1, ci // PEER_TOPK, axis=-1) * PEER_N_KEYS
               + jnp.take_along_axis(i2, ci % PEER_TOPK, axis=-1))
        g = jax.nn.softmax(sc, axis=-1)
        u_sel = peer_u[idx]
        act = jax.nn.gelu(jnp.einsum('chkd,cd->chk', u_sel, hc).astype(jnp.float32),
                          approximate=False) * g
        v_sel = peer_v[idx]
        return jnp.einsum('chk,chkd->cd', act.astype(hc.dtype), v_sel)

    out = lax.map(chunk_fn, tok).reshape(n_chunks * PEER_CHUNK, D)[:T]
    return out.reshape(B, L, D).astype(h.dtype)


def setup_inputs(seed: int = 0) -> dict:
    key = jax.random.key(seed)
    ks = jax.random.split(key, 26)
    nrm = jax.random.normal
    f32 = jnp.float32
    n_idx = jnp.arange(SSM_STATE, dtype=f32)
    return {
        "x": nrm(ks[0], (BATCH, SEQ, D_MODEL), f32),
        "meta_tokens": nrm(ks[1], (N_META, D_MODEL), f32),
        "norm1_g": 1.0 + 0.02 * nrm(ks[2], (DEPTH, D_MODEL), f32),
        "w_in": nrm(ks[3], (DEPTH, D_MODEL, IN_WIDTH), f32) * D_MODEL ** -0.5,
        "q_norm_g": 1.0 + 0.02 * nrm(ks[4], (DEPTH, HEAD_DIM), f32),
        "k_norm_g": 1.0 + 0.02 * nrm(ks[5], (DEPTH, HEAD_DIM), f32),
        "attn_sinks": 0.5 * nrm(ks[6], (DEPTH, N_Q_HEADS), f32),
        "ssm_a_re": -0.5 + 0.01 * nrm(ks[7], (DEPTH, SSM_GROUPS, SSM_STATE), f32),
        "ssm_a_im": math.pi * n_idx + 0.01 * nrm(ks[8], (DEPTH, SSM_GROUPS, SSM_STATE), f32),
        "ssm_log_dt": jax.random.uniform(ks[9], (DEPTH, SSM_GROUPS), f32,
                                         minval=math.log(DT_MIN), maxval=math.log(DT_MAX)),
        "ssm_b_re": nrm(ks[10], (DEPTH, SSM_GROUPS, SSM_STATE, SSM_CH), f32) * (2 * SSM_CH) ** -0.5,
        "ssm_b_im": nrm(ks[11], (DEPTH, SSM_GROUPS, SSM_STATE, SSM_CH), f32) * (2 * SSM_CH) ** -0.5,
        "ssm_c_re": nrm(ks[12], (DEPTH, SSM_GROUPS, SSM_CH, SSM_STATE), f32) * SSM_STATE ** -0.5,
        "ssm_c_im": nrm(ks[13], (DEPTH, SSM_GROUPS, SSM_CH, SSM_STATE), f32) * SSM_STATE ** -0.5,
        "ssm_d": nrm(ks[14], (DEPTH, SSM_WIDTH), f32),
        "ssm_glu_w": nrm(ks[15], (DEPTH, SSM_WIDTH, SSM_WIDTH), f32) * SSM_WIDTH ** -0.5,
        "ssm_glu_b": 0.02 * nrm(ks[16], (DEPTH, SSM_WIDTH), f32),
        "attn_out_g": 1.0 + 0.02 * nrm(ks[17], (DEPTH, ATTN_WIDTH), f32),
        "ssm_out_g": 1.0 + 0.02 * nrm(ks[18], (DEPTH, SSM_WIDTH), f32),
        "w_out": nrm(ks[19], (DEPTH, D_MODEL, D_MODEL), f32) * D_MODEL ** -0.5,
        "norm2_g": 1.0 + 0.02 * nrm(ks[20], (DEPTH, D_MODEL), f32),
        "peer_w_query": nrm(ks[21], (DEPTH, D_MODEL, PEER_HEADS * PEER_DK), f32) * D_MODEL ** -0.5,
        "peer_sub_keys": nrm(ks[22], (DEPTH, PEER_HEADS, 2, PEER_N_KEYS, PEER_HALF), f32) * PEER_HALF ** -0.5,
        "peer_u": nrm(ks[23], (DEPTH, PEER_EXPERTS, D_MODEL), f32) * D_MODEL ** -0.5,
        "peer_v": nrm(ks[24], (DEPTH, PEER_EXPERTS, D_MODEL), f32) * PEER_HEADS ** -0.5,
    }


def reference(x, meta_tokens, norm1_g, w_in, q_norm_g, k_norm_g, attn_sinks, ssm_a_re, ssm_a_im,
              ssm_log_dt, ssm_b_re, ssm_b_im, ssm_c_re, ssm_c_im, ssm_d, ssm_glu_w, ssm_glu_b,
              attn_out_g, ssm_out_g, w_out, norm2_g, peer_w_query, peer_sub_keys, peer_u, peer_v):
    B = x.shape[0]
    meta = jnp.broadcast_to(meta_tokens.astype(x.dtype)[None], (B, N_META, D_MODEL))
    h = jnp.concatenate([meta, x], axis=1)
    o_k = ATTN_WIDTH
    o_v = ATTN_WIDTH + KV_WIDTH
    o_u = ATTN_WIDTH + 2 * KV_WIDTH
    for l in range(DEPTH):
        proj = rms_norm(h, norm1_g[l]) @ w_in[l]
        attn = sliding_window_attention(proj[..., :o_k], proj[..., o_k:o_v], proj[..., o_v:o_u],
                                        q_norm_g[l], k_norm_g[l], attn_sinks[l])
        ssm = s5_ssm(proj[..., o_u:], ssm_a_re[l], ssm_a_im[l], ssm_log_dt[l], ssm_b_re[l], ssm_b_im[l],
                     ssm_c_re[l], ssm_c_im[l], ssm_d[l], ssm_glu_w[l], ssm_glu_b[l])
        mixed = jnp.concatenate([rms_norm(attn, attn_out_g[l]), rms_norm(ssm, ssm_out_g[l])], axis=-1)
        h = h + mixed @ w_out[l]
        h = h + peer_ffn(rms_norm(h, norm2_g[l]), peer_w_query[l], peer_sub_keys[l], peer_u[l], peer_v[l])
    return h[:, N_META:]
```

```python
import functools
import math

import jax
import jax.numpy as jnp
from jax import lax
from jax.experimental import pallas as pl
from jax.experimental.pallas import tpu as pltpu

F32 = jnp.float32
BF16 = jnp.bfloat16
I32 = jnp.int32
U32 = jnp.uint32

D_MODEL = 1024
N_META = 16
BLOCK = 128
ATTN_WIDTH = 512
HEAD_DIM = 64
N_Q_HEADS = 8
N_KV_HEADS = 2
GQA = 4
KV_WIDTH = 128
QK_WIDTH = ATTN_WIDTH + KV_WIDTH
SSM_WIDTH = 512
SSM_CH = 16
SSM_GROUPS = 32
SSM_STATE = 64
IN_WIDTH = ATTN_WIDTH + 2 * KV_WIDTH + SSM_WIDTH
PEER_HEADS = 8
PEER_N_KEYS = 128
PEER_TOPK = 16
PEER_DK = 256
PEER_HALF = 128
PEER_SEL = PEER_HEADS * PEER_TOPK
NORM_EPS = 1e-6
MASK_VALUE = -1e30

SUBLANES = 8
LANES = 128
SSM_HALF_W = SSM_WIDTH // 2
SSM_HALF_STATES = SSM_GROUPS // 2 * SSM_STATE
PACK_ROWS = D_MODEL // 2 // LANES
VMEM_LIMIT_TABLE = 52 * 1024 * 1024


def _dot(a, b):
    return jnp.dot(a, b, preferred_element_type=F32)


def _dot_nt(a, b):
    return lax.dot_general(a, b, (((1,), (1,)), ((), ())), preferred_element_type=F32)


def _gelu(x):
    return 0.5 * x * (1.0 + lax.erf(x * (2.0 ** -0.5)))


def _in_proj_kernel(x_ref, g1_ref, w_ref, ones_ref, qkg_ref, q_ref, k_ref, v_ref, u_ref):
    x = x_ref[...]
    ms = jnp.mean(x * x, axis=-1, keepdims=True)
    y = (x * lax.rsqrt(ms + NORM_EPS) * g1_ref[...]).astype(BF16)
    proj = _dot(y, w_ref[...])
    qk = proj[:, :QK_WIDTH]
    ss = _dot((qk * qk).astype(BF16), ones_ref[...]) * (1.0 / HEAD_DIM)
    qkn = qk * lax.rsqrt(ss + NORM_EPS) * qkg_ref[...]
    q_ref[...] = qkn[:, :ATTN_WIDTH].astype(BF16)
    k_ref[...] = qkn[:, ATTN_WIDTH:].astype(BF16)
    v_ref[...] = proj[:, QK_WIDTH:QK_WIDTH + KV_WIDTH].astype(BF16)
    u_ref[...] = proj[:, QK_WIDTH + KV_WIDTH:]


def _in_proj(x, g1, w_in_bf, ones_bd, qkg, tm):
    rows = x.shape[0]
    row_spec = lambda w: pl.BlockSpec((tm, w), lambda i: (i, 0))
    full = lambda a: pl.BlockSpec(a.shape, lambda i: (0,) * a.ndim)
    return pl.pallas_call(
        _in_proj_kernel,
        grid=(rows // tm,),
        in_specs=[row_spec(D_MODEL), full(g1), full(w_in_bf), full(ones_bd), full(qkg)],
        out_specs=[row_spec(ATTN_WIDTH), row_spec(KV_WIDTH), row_spec(KV_WIDTH), row_spec(SSM_WIDTH)],
        out_shape=[jax.ShapeDtypeStruct((rows, ATTN_WIDTH), BF16),
                   jax.ShapeDtypeStruct((rows, KV_WIDTH), BF16),
                   jax.ShapeDtypeStruct((rows, KV_WIDTH), BF16),
                   jax.ShapeDtypeStruct((rows, SSM_WIDTH), F32)],
        compiler_params=pltpu.CompilerParams(dimension_semantics=("arbitrary",)),
        name="in_proj",
    )(x, g1, w_in_bf, ones_bd, qkg)


def _attn_kernel(sink_ref, q_ref, kp_ref, kc_ref, vp_ref, vc_ref, km_ref, vm_ref, g_ref, o_ref, acc_ref):
    i = pl.program_id(0)
    row = lax.broadcasted_iota(I32, (BLOCK, BLOCK), 0)
    col = lax.broadcasted_iota(I32, (BLOCK, BLOCK), 1)
    prev_ok = jnp.logical_and(col > row, i > 0)
    cur_ok = col <= row
    for h in range(N_Q_HEADS):
        kv = h // GQA
        hs = slice(kv * HEAD_DIM, (kv + 1) * HEAD_DIM)
        qh = q_ref[:, h * HEAD_DIM:(h + 1) * HEAD_DIM]
        sp = jnp.where(prev_ok, _dot_nt(qh, kp_ref[:, hs]), MASK_VALUE)
        sc = jnp.where(cur_ok, _dot_nt(qh, kc_ref[:, hs]), MASK_VALUE)
        sm = _dot_nt(qh, km_ref[:, hs])
        sink = sink_ref[h]
        m = jnp.maximum(jnp.maximum(jnp.max(sp, axis=-1, keepdims=True), jnp.max(sc, axis=-1, keepdims=True)),
                        jnp.maximum(jnp.max(sm, axis=-1, keepdims=True), sink))
        pp, pc, pm = jnp.exp(sp - m), jnp.exp(sc - m), jnp.exp(sm - m)
        denom = (jnp.sum(pp, axis=-1, keepdims=True) + jnp.sum(pc, axis=-1, keepdims=True)
                 + jnp.sum(pm, axis=-1, keepdims=True) + jnp.exp(sink - m))
        o = (_dot(pp.astype(BF16), vp_ref[:, hs]) + _dot(pc.astype(BF16), vc_ref[:, hs])
             + _dot(pm.astype(BF16), vm_ref[:, hs]))
        acc_ref[:, h * HEAD_DIM:(h + 1) * HEAD_DIM] = o / denom
    a = acc_ref[...]
    ms = jnp.mean(a * a, axis=-1, keepdims=True)
    o_ref[...] = (a * lax.rsqrt(ms + NORM_EPS) * g_ref[...]).astype(BF16)


def _attention(sinks, q, k, v, k_meta, v_meta, g_out):
    nb = q.shape[0] // BLOCK
    cur = lambda w: pl.BlockSpec((BLOCK, w), lambda i: (i, 0))
    prev = lambda w: pl.BlockSpec((BLOCK, w), lambda i: (jnp.maximum(i - 1, 0), 0))
    full = lambda a: pl.BlockSpec(a.shape, lambda i: (0,) * a.ndim)
    return pl.pallas_call(
        _attn_kernel,
        grid=(nb,),
        in_specs=[pl.BlockSpec(memory_space=pltpu.SMEM),
                  cur(ATTN_WIDTH), prev(KV_WIDTH), cur(KV_WIDTH), prev(KV_WIDTH), cur(KV_WIDTH),
                  full(k_meta), full(v_meta), full(g_out)],
        out_specs=cur(ATTN_WIDTH),
        out_shape=jax.ShapeDtypeStruct((q.shape[0], ATTN_WIDTH), BF16),
        scratch_shapes=[pltpu.VMEM((BLOCK, ATTN_WIDTH), F32)],
        compiler_params=pltpu.CompilerParams(dimension_semantics=("arbitrary",)),
        name="attention",
    )(sinks, q, k, k, v, v, k_meta, v_meta, g_out)


def _ssm_kernel(u_ref, bmat_ref, cmat_ref, coef_ref, carry_in_ref, d_ref, gw_ref, gb_ref, g_ref,
                o_ref, carry_out_ref, st_ref, carry_ref, y_ref, *, tm):
    @pl.when(pl.program_id(0) == 0)
    def _():
        carry_ref[...] = carry_in_ref[...]

    ns = SSM_HALF_STATES
    for h in range(2):
        uh = u_ref[:, h * SSM_HALF_W:(h + 1) * SSM_HALF_W].astype(BF16)
        st_ref[...] = _dot(uh, bmat_ref[h])

        def block(b, carry):
            cr, ci = carry
            r0 = pl.multiple_of(b * SUBLANES, SUBLANES)
            xr = st_ref[pl.ds(r0, SUBLANES), 0:ns]
            xi = st_ref[pl.ds(r0, SUBLANES), ns:2 * ns]
            for j, shift in enumerate((1, 2, 4)):
                ar, ai = coef_ref[h, j, 0], coef_ref[h, j, 1]
                sr, si = pltpu.roll(xr, shift, 0), pltpu.roll(xi, shift, 0)
                xr, xi = xr + ar * sr - ai * si, xi + ar * si + ai * sr
            pr, pi = coef_ref[h, 3, 0], coef_ref[h, 3, 1]
            xr, xi = xr + pr * cr - pi * ci, xi + pr * ci + pi * cr
            st_ref[pl.ds(r0, SUBLANES), 0:ns] = xr
            st_ref[pl.ds(r0, SUBLANES), ns:2 * ns] = xi
            return (jnp.broadcast_to(xr[SUBLANES - 1:SUBLANES, :], (SUBLANES, ns)),
                    jnp.broadcast_to(xi[SUBLANES - 1:SUBLANES, :], (SUBLANES, ns)))

        cr, ci = lax.fori_loop(0, tm // SUBLANES, block, (carry_ref[h, 0], carry_ref[h, 1]))
        carry_ref[h, 0] = cr
        carry_ref[h, 1] = ci
        y_ref[:, h * SSM_HALF_W:(h + 1) * SSM_HALF_W] = _dot(st_ref[...].astype(BF16), cmat_ref[h])

    carry_out_ref[...] = carry_ref[...]
    z = _gelu(y_ref[...] + d_ref[...] * u_ref[...])
    z = z * jax.nn.sigmoid(_dot(z.astype(BF16), gw_ref[...]) + gb_ref[...])
    ms = jnp.mean(z * z, axis=-1, keepdims=True)
    o_ref[...] = (z * lax.rsqrt(ms + NORM_EPS) * g_ref[...]).astype(BF16)


def _ssm(u, bmat, cmat, coef, carry_in, d, gw_bf, gb, g_out, tm):
    rows = u.shape[0]
    row_spec = pl.BlockSpec((tm, SSM_WIDTH), lambda i: (i, 0))
    full = lambda a: pl.BlockSpec(a.shape, lambda i: (0,) * a.ndim)
    return pl.pallas_call(
        functools.partial(_ssm_kernel, tm=tm),
        grid=(rows // tm,),
        in_specs=[row_spec, full(bmat), full(cmat), full(coef), full(carry_in), full(d), full(gw_bf), full(gb),
                  full(g_out)],
        out_specs=[row_spec, full(carry_in)],
        out_shape=[jax.ShapeDtypeStruct((rows, SSM_WIDTH), BF16),
                   jax.ShapeDtypeStruct(carry_in.shape, F32)],
        scratch_shapes=[pltpu.VMEM((tm, 2 * SSM_HALF_STATES), F32),
                        pltpu.VMEM(carry_in.shape, F32),
                        pltpu.VMEM((tm, SSM_WIDTH), F32)],
        compiler_params=pltpu.CompilerParams(dimension_semantics=("arbitrary",)),
        name="ssm",
    )(u, bmat, cmat, coef, carry_in, d, gw_bf, gb, g_out)


def _ssm_params(a_re, a_im, log_dt, b_re, b_im, c_re, c_im):
    dt = jnp.exp(log_dt)[:, None]
    mag = jnp.exp(a_re * dt)
    abar_r, abar_i = mag * jnp.cos(a_im * dt), mag * jnp.sin(a_im * dt)
    den = a_re * a_re + a_im * a_im
    nr, ni = abar_r - 1.0, abar_i
    coef_r = (nr * a_re + ni * a_im) / den
    coef_i = (ni * a_re - nr * a_im) / den
    bbar_r = coef_r[..., None] * b_re - coef_i[..., None] * b_im
    bbar_i = coef_r[..., None] * b_im + coef_i[..., None] * b_re
    gh = SSM_GROUPS // 2
    eye = jnp.eye(gh, dtype=F32)

    def half_b(bb):
        return jnp.einsum('gnp,gk->gpkn', bb, eye).reshape(gh * SSM_CH, gh * SSM_STATE)

    def half_c(cc):
        return jnp.einsum('gpn,gk->gnkp', cc, eye).reshape(gh * SSM_STATE, gh * SSM_CH)

    bmat = jnp.stack([jnp.concatenate([half_b(bbar_r[s]), half_b(bbar_i[s])], axis=1)
                      for s in (slice(0, gh), slice(gh, None))]).astype(BF16)
    cmat = jnp.stack([jnp.concatenate([half_c(c_re[s]), -half_c(c_im[s])], axis=0)
                      for s in (slice(0, gh), slice(gh, None))]).astype(BF16)

    def cpow(k):
        m = mag ** k
        return m * jnp.cos(a_im * dt * k), m * jnp.sin(a_im * dt * k)

    rows = jnp.arange(SUBLANES)
    coefs = []
    for shift in (1, 2, 4):
        pr, pi = cpow(float(shift))
        keep = (rows >= shift).astype(F32)[:, None, None]
        coefs.append(jnp.stack([keep * pr[None], keep * pi[None]]))
    pw = (rows + 1).astype(F32)[:, None, None]
    m = mag[None] ** pw
    coefs.append(jnp.stack([m * jnp.cos(a_im[None] * dt[None] * pw), m * jnp.sin(a_im[None] * dt[None] * pw)]))
    coef = jnp.stack(coefs)
    coef = coef.reshape(4, 2, SUBLANES, 2, SSM_HALF_STATES).transpose(3, 0, 1, 2, 4)
    return bmat, cmat, coef


def _top16(s, payload, n_rows):
    rowid = lax.broadcasted_iota(I32, s.shape, 0)
    vals, pays = [], []
    for _ in range(PEER_TOPK):
        m = jnp.max(s, axis=0, keepdims=True)
        pos = jnp.min(jnp.where(s == m, rowid, n_rows), axis=0, keepdims=True)
        sel = rowid == pos
        vals.append(m)
        pays.append(pos if payload is None else jnp.max(jnp.where(sel, payload, -1), axis=0, keepdims=True))
        s = jnp.where(sel, -jnp.inf, s)
    return jnp.concatenate(vals, axis=0), jnp.concatenate(pays, axis=0)


def _mix_kernel(x_ref, a_ref, s_ref, wo_ref, g2_ref, wq_ref, keys_ref, h1_ref, hn_ref, idx_ref, gate_ref,
                qp_ref, idx_t_ref, gate_t_ref):
    mix = _dot(a_ref[...], wo_ref[0:ATTN_WIDTH, :]) + _dot(s_ref[...], wo_ref[ATTN_WIDTH:, :])
    h1 = x_ref[...] + mix
    h1_ref[...] = h1
    ms = jnp.mean(h1 * h1, axis=-1, keepdims=True)
    hn = h1 * lax.rsqrt(ms + NORM_EPS) * g2_ref[...]
    hn_ref[...] = hn
    qp_ref[...] = _dot(hn.astype(BF16), wq_ref[...])

    def head(h, _):
        c0 = pl.multiple_of(h * PEER_DK, PEER_DK)
        q1 = qp_ref[:, pl.ds(c0, PEER_HALF)].astype(BF16)
        q2 = qp_ref[:, pl.ds(c0 + PEER_HALF, PEER_HALF)].astype(BF16)
        s1 = _dot_nt(keys_ref[h, 0], q1)
        s2 = _dot_nt(keys_ref[h, 1], q2)
        v1, i1 = _top16(s1, None, PEER_N_KEYS)
        v2, i2 = _top16(s2, None, PEER_N_KEYS)
        cand = jnp.concatenate([v1[a:a + 1, :] + v2 for a in range(PEER_TOPK)], axis=0)
        eidx = jnp.concatenate([i1[a:a + 1, :] * PEER_N_KEYS + i2 for a in range(PEER_TOPK)], axis=0)
        sc, ei = _top16(cand, eidx, PEER_TOPK * PEER_TOPK)
        p = jnp.exp(sc - sc[0:1, :])
        r0 = pl.multiple_of(h * PEER_TOPK, PEER_TOPK)
        idx_t_ref[pl.ds(r0, PEER_TOPK), :] = ei * PACK_ROWS
        gate_t_ref[pl.ds(r0, PEER_TOPK), :] = p / jnp.sum(p, axis=0, keepdims=True)
        return 0

    lax.fori_loop(0, PEER_HEADS, head, 0)
    idx_ref[...] = idx_t_ref[...].T
    gate_ref[...] = gate_t_ref[...].T


def _mix(x, attn_n, ssm_n, w_out_bf, g2, wq_bf, keys_bf, tm):
    rows = x.shape[0]
    row_spec = lambda w: pl.BlockSpec((tm, w), lambda i: (i, 0))
    full = lambda a: pl.BlockSpec(a.shape, lambda i: (0,) * a.ndim)
    return pl.pallas_call(
        _mix_kernel,
        grid=(rows // tm,),
        in_specs=[row_spec(D_MODEL), row_spec(ATTN_WIDTH), row_spec(SSM_WIDTH), full(w_out_bf), full(g2),
                  full(wq_bf), full(keys_bf)],
        out_specs=[row_spec(D_MODEL), row_spec(D_MODEL), row_spec(PEER_SEL), row_spec(PEER_SEL)],
        out_shape=[jax.ShapeDtypeStruct((rows, D_MODEL), F32),
                   jax.ShapeDtypeStruct((rows, D_MODEL), F32),
                   jax.ShapeDtypeStruct((rows, PEER_SEL), I32),
                   jax.ShapeDtypeStruct((rows, PEER_SEL), F32)],
        scratch_shapes=[pltpu.VMEM((tm, PEER_HEADS * PEER_DK), F32),
                        pltpu.VMEM((PEER_SEL, tm), I32),
                        pltpu.VMEM((PEER_SEL, tm), F32)],
        compiler_params=pltpu.CompilerParams(dimension_semantics=("arbitrary",),
                                             vmem_limit_bytes=48 * 1024 * 1024),
        name="mix_topk",
    )(x, attn_n, ssm_n, w_out_bf, g2, wq_bf, keys_bf)


def _pack_table(t):
    e = t.shape[0]
    b = lax.bitcast_convert_type(t.astype(BF16), jnp.uint16).astype(U32)
    w = b[:, :D_MODEL // 2] | (b[:, D_MODEL // 2:] << 16)
    return w.reshape(e * PACK_ROWS, LANES)


def _unpack(w):
    lo = lax.bitcast_convert_type(w << 16, F32)
    hi = lax.bitcast_convert_type(w & jnp.uint32(0xFFFF0000), F32)
    return lo, hi


def _load_table(tab_hbm, tab_ref, sem):
    @pl.when(pl.program_id(0) == 0)
    def _():
        cp = pltpu.make_async_copy(tab_hbm, tab_ref, sem)
        cp.start()
        cp.wait()


def _peer_u_kernel(idx_ref, h_ref, gate_ref, tab_hbm, act_ref, tab_ref, part_ref, sem, *, tt):
    _load_table(tab_hbm, tab_ref, sem)

    def token(t, _):
        hv = h_ref[t]
        hlo, hhi = hv[0:PACK_ROWS], hv[PACK_ROWS:]
        for k in range(PEER_SEL):
            r0 = pl.multiple_of(idx_ref[t, k], PACK_ROWS)
            lo, hi = _unpack(tab_ref[pl.ds(r0, PACK_ROWS), :])
            part_ref[k * PACK_ROWS:(k + 1) * PACK_ROWS, :] = lo * hlo + hi * hhi
        part = part_ref[pl.ds(0, PEER_SEL, stride=PACK_ROWS), :]
        for r in range(1, PACK_ROWS):
            part = part + part_ref[pl.ds(r, PEER_SEL, stride=PACK_ROWS), :]
        dots = jnp.sum(part.T, axis=0, keepdims=True)
        act_ref[t] = _gelu(dots) * gate_ref[t]
        return 0

    lax.fori_loop(0, tt, token, 0)


def _peer_u(idx, hn3, gate3, tab, tt):
    rows = idx.shape[0]
    return pl.pallas_call(
        functools.partial(_peer_u_kernel, tt=tt),
        grid=(rows // tt,),
        in_specs=[pl.BlockSpec((tt, PEER_SEL), lambda i: (i, 0), memory_space=pltpu.SMEM),
                  pl.BlockSpec((tt, SUBLANES, LANES), lambda i: (i, 0, 0)),
                  pl.BlockSpec((tt, 1, PEER_SEL), lambda i: (i, 0, 0)),
                  pl.BlockSpec(memory_space=pl.ANY)],
        out_specs=pl.BlockSpec((tt, 1, PEER_SEL), lambda i: (i, 0, 0)),
        out_shape=jax.ShapeDtypeStruct((rows, 1, PEER_SEL), F32),
        scratch_shapes=[pltpu.VMEM(tab.shape, U32),
                        pltpu.VMEM((PEER_SEL * PACK_ROWS, LANES), F32),
                        pltpu.SemaphoreType.DMA],
        compiler_params=pltpu.CompilerParams(dimension_semantics=("arbitrary",),
                                             vmem_limit_bytes=VMEM_LIMIT_TABLE),
        name="peer_u",
    )(idx, hn3, gate3, tab)


def _peer_v_kernel(idx_ref, act_ref, h1_ref, tab_hbm, o_ref, tab_ref, sem, *, tt):
    _load_table(tab_hbm, tab_ref, sem)
    n_acc = 2

    def token(t, _):
        zero = jnp.zeros((PACK_ROWS, LANES), F32)
        alo, ahi = [zero] * n_acc, [zero] * n_acc
        for k in range(PEER_SEL):
            r0 = pl.multiple_of(idx_ref[t, k], PACK_ROWS)
            a = act_ref[t, k]
            lo, hi = _unpack(tab_ref[pl.ds(r0, PACK_ROWS), :])
            alo[k % n_acc] = alo[k % n_acc] + a * lo
            ahi[k % n_acc] = ahi[k % n_acc] + a * hi
        hv = h1_ref[t]
        o_ref[t, 0:PACK_ROWS, :] = hv[0:PACK_ROWS] + sum(alo[1:], alo[0])
        o_ref[t, PACK_ROWS:, :] = hv[PACK_ROWS:] + sum(ahi[1:], ahi[0])
        return 0

    lax.fori_loop(0, tt, token, 0)


def _peer_v(idx, act, h13, tab, tt):
    rows = idx.shape[0]
    smem = lambda: pl.BlockSpec((tt, PEER_SEL), lambda i: (i, 0), memory_space=pltpu.SMEM)
    tile = pl.BlockSpec((tt, SUBLANES, LANES), lambda i: (i, 0, 0))
    return pl.pallas_call(
        functools.partial(_peer_v_kernel, tt=tt),
        grid=(rows // tt,),
        in_specs=[smem(), smem(), tile, pl.BlockSpec(memory_space=pl.ANY)],
        out_specs=tile,
        out_shape=jax.ShapeDtypeStruct((rows, SUBLANES, LANES), F32),
        scratch_shapes=[pltpu.VMEM(tab.shape, U32), pltpu.SemaphoreType.DMA],
        compiler_params=pltpu.CompilerParams(dimension_semantics=("arbitrary",),
                                             vmem_limit_bytes=VMEM_LIMIT_TABLE),
        name="peer_v",
    )(idx, act, h13, tab)


def _block(x2, meta_tokens, norm1_g, w_in, q_norm_g, k_norm_g, attn_sinks, ssm_a_re, ssm_a_im, ssm_log_dt,
           ssm_b_re, ssm_b_im, ssm_c_re, ssm_c_im, ssm_d, ssm_glu_w, ssm_glu_b, attn_out_g, ssm_out_g, w_out,
           norm2_g, peer_w_query, peer_sub_keys, peer_u, peer_v, *, tm_proj, tm_ssm, tm_mix, tt_peer):
    rows = x2.shape[0]
    row2 = lambda a: a.reshape(1, -1).astype(F32)
    w_in_bf = w_in.astype(BF16)
    seg = jnp.arange(QK_WIDTH) // HEAD_DIM
    ones_bd = (seg[:, None] == seg[None, :]).astype(BF16)
    scale = HEAD_DIM ** -0.5
    qkg = jnp.concatenate([jnp.tile(q_norm_g.astype(F32), N_Q_HEADS) * scale,
                           jnp.tile(k_norm_g.astype(F32), N_KV_HEADS)]).reshape(1, QK_WIDTH)
    bmat, cmat, coef = _ssm_params(ssm_a_re.astype(F32), ssm_a_im.astype(F32), ssm_log_dt.astype(F32),
                                   ssm_b_re.astype(F32), ssm_b_im.astype(F32), ssm_c_re.astype(F32),
                                   ssm_c_im.astype(F32))
    glu_w_bf = ssm_glu_w.astype(BF16)
    ssm_args = (row2(ssm_d), glu_w_bf, row2(ssm_glu_b), row2(ssm_out_g))

    h0 = jnp.concatenate([jnp.zeros((BLOCK - N_META, D_MODEL), F32), meta_tokens.astype(F32)], axis=0)
    _, k0, v0, u0 = _in_proj(h0, row2(norm1_g), w_in_bf, ones_bd, qkg, BLOCK)
    carry0 = jnp.zeros((2, 2, SUBLANES, SSM_HALF_STATES), F32)
    _, carry = _ssm(u0, bmat, cmat, coef, carry0, *ssm_args, BLOCK)

    q, k, v, u = _in_proj(x2, row2(norm1_g), w_in_bf, ones_bd, qkg, tm_proj)
    attn_n = _attention(attn_sinks.astype(F32), q, k, v, k0[BLOCK - N_META:], v0[BLOCK - N_META:],
                        row2(attn_out_g))
    ssm_n, _ = _ssm(u, bmat, cmat, coef, carry, *ssm_args, tm_ssm)

    h1, hn, idx, gate = _mix(x2, attn_n, ssm_n, w_out.astype(BF16), row2(norm2_g), peer_w_query.astype(BF16),
                             peer_sub_keys.astype(BF16), tm_mix)
    act = _peer_u(idx, hn.reshape(rows, SUBLANES, LANES), gate.reshape(rows, 1, PEER_SEL), _pack_table(peer_u),
                  tt_peer)
    out = _peer_v(idx, act.reshape(rows, PEER_SEL), h1.reshape(rows, SUBLANES, LANES), _pack_table(peer_v),
                  tt_peer)
    return out.reshape(rows, D_MODEL)


def kernel(x, meta_tokens, norm1_g, w_in, q_norm_g, k_norm_g, attn_sinks, ssm_a_re, ssm_a_im, ssm_log_dt, ssm_b_re, ssm_b_im, ssm_c_re, ssm_c_im, ssm_d, ssm_glu_w, ssm_glu_b, attn_out_g, ssm_out_g, w_out, norm2_g, peer_w_query, peer_sub_keys, peer_u, peer_v):
    b, seq, d = x.shape
    outs = []
    for bi in range(b):
        outs.append(_block(x[bi].astype(F32), meta_tokens, norm1_g[0], w_in[0], q_norm_g[0], k_norm_g[0],
                           attn_sinks[0], ssm_a_re[0], ssm_a_im[0], ssm_log_dt[0], ssm_b_re[0], ssm_b_im[0],
                           ssm_c_re[0], ssm_c_im[0], ssm_d[0], ssm_glu_w[0], ssm_glu_b[0], attn_out_g[0],
                           ssm_out_g[0], w_out[0], norm2_g[0], peer_w_query[0], peer_sub_keys[0], peer_u[0],
                           peer_v[0], tm_proj=512, tm_ssm=256, tm_mix=256, tt_peer=32))
    return jnp.stack(outs).astype(x.dtype)
```

```python
import functools
import math

import jax
import jax.numpy as jnp
from jax import lax
from jax.experimental import pallas as pl
from jax.experimental.pallas import tpu as pltpu

F32 = jnp.float32
BF16 = jnp.bfloat16
I32 = jnp.int32

D_MODEL = 1024
N_META = 16
BLOCK = 128
ATTN_WIDTH = 512
HEAD_DIM = 64
N_Q_HEADS = 8
N_KV_HEADS = 2
GQA = 4
KV_WIDTH = 128
QK_WIDTH = ATTN_WIDTH + KV_WIDTH
SSM_WIDTH = 512
SSM_CH = 16
SSM_GROUPS = 32
SSM_STATE = 64
IN_WIDTH = ATTN_WIDTH + 2 * KV_WIDTH + SSM_WIDTH
PEER_HEADS = 8
PEER_N_KEYS = 128
PEER_TOPK = 16
PEER_DK = 256
PEER_HALF = 128
PEER_SEL = PEER_HEADS * PEER_TOPK
NORM_EPS = 1e-6
MASK_VALUE = -1e30

SUBLANES = 8
LANES = 128
SSM_HALF_W = SSM_WIDTH // 2
SSM_HALF_STATES = SSM_GROUPS // 2 * SSM_STATE
HALF_TILE = SUBLANES // 2
PEER_V_GROUP = 2
VMEM_LIMIT_TABLE = 52 * 1024 * 1024


def _dot(a, b):
    return jnp.dot(a, b, preferred_element_type=F32)


def _dot_nt(a, b):
    return lax.dot_general(a, b, (((1,), (1,)), ((), ())), preferred_element_type=F32)


def _gelu(x):
    return 0.5 * x * (1.0 + lax.erf(x * (2.0 ** -0.5)))


def _in_proj_kernel(x_ref, g1_ref, w_ref, ones_ref, qkg_ref, q_ref, k_ref, v_ref, u_ref):
    x = x_ref[...]
    ms = jnp.mean(x * x, axis=-1, keepdims=True)
    y = (x * lax.rsqrt(ms + NORM_EPS) * g1_ref[...]).astype(BF16)
    proj = _dot(y, w_ref[...])
    qk = proj[:, :QK_WIDTH]
    ss = _dot((qk * qk).astype(BF16), ones_ref[...]) * (1.0 / HEAD_DIM)
    qkn = qk * lax.rsqrt(ss + NORM_EPS) * qkg_ref[...]
    q_ref[...] = qkn[:, :ATTN_WIDTH].astype(BF16)
    k_ref[...] = qkn[:, ATTN_WIDTH:].astype(BF16)
    v_ref[...] = proj[:, QK_WIDTH:QK_WIDTH + KV_WIDTH].astype(BF16)
    u_ref[...] = proj[:, QK_WIDTH + KV_WIDTH:]


def _in_proj(x, g1, w_in_bf, ones_bd, qkg, tm):
    rows = x.shape[0]
    row_spec = lambda w: pl.BlockSpec((tm, w), lambda i: (i, 0))
    full = lambda a: pl.BlockSpec(a.shape, lambda i: (0,) * a.ndim)
    return pl.pallas_call(
        _in_proj_kernel,
        grid=(rows // tm,),
        in_specs=[row_spec(D_MODEL), full(g1), full(w_in_bf), full(ones_bd), full(qkg)],
        out_specs=[row_spec(ATTN_WIDTH), row_spec(KV_WIDTH), row_spec(KV_WIDTH), row_spec(SSM_WIDTH)],
        out_shape=[jax.ShapeDtypeStruct((rows, ATTN_WIDTH), BF16),
                   jax.ShapeDtypeStruct((rows, KV_WIDTH), BF16),
                   jax.ShapeDtypeStruct((rows, KV_WIDTH), BF16),
                   jax.ShapeDtypeStruct((rows, SSM_WIDTH), F32)],
        compiler_params=pltpu.CompilerParams(dimension_semantics=("arbitrary",)),
        name="in_proj",
    )(x, g1, w_in_bf, ones_bd, qkg)


def _attn_kernel(sink_ref, q_ref, kp_ref, kc_ref, vp_ref, vc_ref, km_ref, vm_ref, g_ref, o_ref, acc_ref):
    i = pl.program_id(0)
    row = lax.broadcasted_iota(I32, (BLOCK, BLOCK), 0)
    col = lax.broadcasted_iota(I32, (BLOCK, BLOCK), 1)
    prev_ok = jnp.logical_and(col > row, i > 0)
    cur_ok = col <= row
    for h in range(N_Q_HEADS):
        kv = h // GQA
        hs = slice(kv * HEAD_DIM, (kv + 1) * HEAD_DIM)
        qh = q_ref[:, h * HEAD_DIM:(h + 1) * HEAD_DIM]
        sp = jnp.where(prev_ok, _dot_nt(qh, kp_ref[:, hs]), MASK_VALUE)
        sc = jnp.where(cur_ok, _dot_nt(qh, kc_ref[:, hs]), MASK_VALUE)
        sm = _dot_nt(qh, km_ref[:, hs])
        sink = sink_ref[h]
        m = jnp.maximum(jnp.maximum(jnp.max(sp, axis=-1, keepdims=True), jnp.max(sc, axis=-1, keepdims=True)),
                        jnp.maximum(jnp.max(sm, axis=-1, keepdims=True), sink))
        pp, pc, pm = jnp.exp(sp - m), jnp.exp(sc - m), jnp.exp(sm - m)
        denom = (jnp.sum(pp, axis=-1, keepdims=True) + jnp.sum(pc, axis=-1, keepdims=True)
                 + jnp.sum(pm, axis=-1, keepdims=True) + jnp.exp(sink - m))
        o = (_dot(pp.astype(BF16), vp_ref[:, hs]) + _dot(pc.astype(BF16), vc_ref[:, hs])
             + _dot(pm.astype(BF16), vm_ref[:, hs]))
        acc_ref[:, h * HEAD_DIM:(h + 1) * HEAD_DIM] = o / denom
    a = acc_ref[...]
    ms = jnp.mean(a * a, axis=-1, keepdims=True)
    o_ref[...] = (a * lax.rsqrt(ms + NORM_EPS) * g_ref[...]).astype(BF16)


def _attention(sinks, q, k, v, k_meta, v_meta, g_out):
    nb = q.shape[0] // BLOCK
    cur = lambda w: pl.BlockSpec((BLOCK, w), lambda i: (i, 0))
    prev = lambda w: pl.BlockSpec((BLOCK, w), lambda i: (jnp.maximum(i - 1, 0), 0))
    full = lambda a: pl.BlockSpec(a.shape, lambda i: (0,) * a.ndim)
    return pl.pallas_call(
        _attn_kernel,
        grid=(nb,),
        in_specs=[pl.BlockSpec(memory_space=pltpu.SMEM),
                  cur(ATTN_WIDTH), prev(KV_WIDTH), cur(KV_WIDTH), prev(KV_WIDTH), cur(KV_WIDTH),
                  full(k_meta), full(v_meta), full(g_out)],
        out_specs=cur(ATTN_WIDTH),
        out_shape=jax.ShapeDtypeStruct((q.shape[0], ATTN_WIDTH), BF16),
        scratch_shapes=[pltpu.VMEM((BLOCK, ATTN_WIDTH), F32)],
        compiler_params=pltpu.CompilerParams(dimension_semantics=("arbitrary",)),
        name="attention",
    )(sinks, q, k, k, v, v, k_meta, v_meta, g_out)


def _ssm_kernel(u_ref, bmat_ref, cmat_ref, coef_ref, carry_in_ref, d_ref, gw_ref, gb_ref, g_ref,
                o_ref, carry_out_ref, st_ref, carry_ref, y_ref, *, tm):
    @pl.when(pl.program_id(0) == 0)
    def _():
        carry_ref[...] = carry_in_ref[...]

    ns = SSM_HALF_STATES
    for h in range(2):
        uh = u_ref[:, h * SSM_HALF_W:(h + 1) * SSM_HALF_W].astype(BF16)
        st_ref[...] = _dot(uh, bmat_ref[h])

        def block(b, carry):
            cr, ci = carry
            r0 = pl.multiple_of(b * SUBLANES, SUBLANES)
            xr = st_ref[pl.ds(r0, SUBLANES), 0:ns]
            xi = st_ref[pl.ds(r0, SUBLANES), ns:2 * ns]
            for j, shift in enumerate((1, 2, 4)):
                ar, ai = coef_ref[h, j, 0], coef_ref[h, j, 1]
                sr, si = pltpu.roll(xr, shift, 0), pltpu.roll(xi, shift, 0)
                xr, xi = xr + ar * sr - ai * si, xi + ar * si + ai * sr
            pr, pi = coef_ref[h, 3, 0], coef_ref[h, 3, 1]
            xr, xi = xr + pr * cr - pi * ci, xi + pr * ci + pi * cr
            st_ref[pl.ds(r0, SUBLANES), 0:ns] = xr
            st_ref[pl.ds(r0, SUBLANES), ns:2 * ns] = xi
            return (jnp.broadcast_to(xr[SUBLANES - 1:SUBLANES, :], (SUBLANES, ns)),
                    jnp.broadcast_to(xi[SUBLANES - 1:SUBLANES, :], (SUBLANES, ns)))

        cr, ci = lax.fori_loop(0, tm // SUBLANES, block, (carry_ref[h, 0], carry_ref[h, 1]))
        carry_ref[h, 0] = cr
        carry_ref[h, 1] = ci
        y_ref[:, h * SSM_HALF_W:(h + 1) * SSM_HALF_W] = _dot(st_ref[...].astype(BF16), cmat_ref[h])

    carry_out_ref[...] = carry_ref[...]
    z = _gelu(y_ref[...] + d_ref[...] * u_ref[...])
    z = z * jax.nn.sigmoid(_dot(z.astype(BF16), gw_ref[...]) + gb_ref[...])
    ms = jnp.mean(z * z, axis=-1, keepdims=True)
    o_ref[...] = (z * lax.rsqrt(ms + NORM_EPS) * g_ref[...]).astype(BF16)


def _ssm(u, bmat, cmat, coef, carry_in, d, gw_bf, gb, g_out, tm):
    rows = u.shape[0]
    row_spec = pl.BlockSpec((tm, SSM_WIDTH), lambda i: (i, 0))
    full = lambda a: pl.BlockSpec(a.shape, lambda i: (0,) * a.ndim)
    return pl.pallas_call(
        functools.partial(_ssm_kernel, tm=tm),
        grid=(rows // tm,),
        in_specs=[row_spec, full(bmat), full(cmat), full(coef), full(carry_in), full(d), full(gw_bf), full(gb),
                  full(g_out)],
        out_specs=[row_spec, full(carry_in)],
        out_shape=[jax.ShapeDtypeStruct((rows, SSM_WIDTH), BF16),
                   jax.ShapeDtypeStruct(carry_in.shape, F32)],
        scratch_shapes=[pltpu.VMEM((tm, 2 * SSM_HALF_STATES), F32),
                        pltpu.VMEM(carry_in.shape, F32),
                        pltpu.VMEM((tm, SSM_WIDTH), F32)],
        compiler_params=pltpu.CompilerParams(dimension_semantics=("arbitrary",)),
        name="ssm",
    )(u, bmat, cmat, coef, carry_in, d, gw_bf, gb, g_out)


def _ssm_params(a_re, a_im, log_dt, b_re, b_im, c_re, c_im):
    dt = jnp.exp(log_dt)[:, None]
    mag = jnp.exp(a_re * dt)
    abar_r, abar_i = mag * jnp.cos(a_im * dt), mag * jnp.sin(a_im * dt)
    den = a_re * a_re + a_im * a_im
    nr, ni = abar_r - 1.0, abar_i
    coef_r = (nr * a_re + ni * a_im) / den
    coef_i = (ni * a_re - nr * a_im) / den
    bbar_r = coef_r[..., None] * b_re - coef_i[..., None] * b_im
    bbar_i = coef_r[..., None] * b_im + coef_i[..., None] * b_re
    gh = SSM_GROUPS // 2
    eye = jnp.eye(gh, dtype=F32)

    def half_b(bb):
        return jnp.einsum('gnp,gk->gpkn', bb, eye).reshape(gh * SSM_CH, gh * SSM_STATE)

    def half_c(cc):
        return jnp.einsum('gpn,gk->gnkp', cc, eye).reshape(gh * SSM_STATE, gh * SSM_CH)

    bmat = jnp.stack([jnp.concatenate([half_b(bbar_r[s]), half_b(bbar_i[s])], axis=1)
                      for s in (slice(0, gh), slice(gh, None))]).astype(BF16)
    cmat = jnp.stack([jnp.concatenate([half_c(c_re[s]), -half_c(c_im[s])], axis=0)
                      for s in (slice(0, gh), slice(gh, None))]).astype(BF16)

    def cpow(k):
        m = mag ** k
        return m * jnp.cos(a_im * dt * k), m * jnp.sin(a_im * dt * k)

    rows = jnp.arange(SUBLANES)
    coefs = []
    for shift in (1, 2, 4):
        pr, pi = cpow(float(shift))
        keep = (rows >= shift).astype(F32)[:, None, None]
        coefs.append(jnp.stack([keep * pr[None], keep * pi[None]]))
    pw = (rows + 1).astype(F32)[:, None, None]
    m = mag[None] ** pw
    coefs.append(jnp.stack([m * jnp.cos(a_im[None] * dt[None] * pw), m * jnp.sin(a_im[None] * dt[None] * pw)]))
    coef = jnp.stack(coefs)
    coef = coef.reshape(4, 2, SUBLANES, 2, SSM_HALF_STATES).transpose(3, 0, 1, 2, 4)
    return bmat, cmat, coef


def _top16(s, payload, n_rows):
    rowid = lax.broadcasted_iota(I32, s.shape, 0).astype(F32)
    vals, pays = [], []
    for _ in range(PEER_TOPK):
        m = jnp.max(s, axis=0, keepdims=True)
        pos = jnp.min(jnp.where(s == m, rowid, float(n_rows)), axis=0, keepdims=True)
        sel = rowid == pos
        vals.append(m)
        pays.append(pos if payload is None else jnp.max(jnp.where(sel, payload, -1.0), axis=0, keepdims=True))
        s = jnp.where(sel, -jnp.inf, s)
    return jnp.concatenate(vals, axis=0), jnp.concatenate(pays, axis=0)


_CAND_PAIRS = [(a, b) for a in range(PEER_TOPK) for b in range(PEER_TOPK) if (a + 1) * (b + 1) <= PEER_TOPK]
_CAND_ROWS = -(-len(_CAND_PAIRS) // SUBLANES) * SUBLANES


def _mix_kernel(x_ref, a_ref, s_ref, wo_ref, g2_ref, wq_ref, keys_ref, h1_ref, hn_ref, idx_ref, gate_ref,
                qp_ref, idx_t_ref, gate_t_ref, *, tm):
    mix = _dot(a_ref[...], wo_ref[0:ATTN_WIDTH, :]) + _dot(s_ref[...], wo_ref[ATTN_WIDTH:, :])
    h1 = x_ref[...] + mix
    h1_ref[...] = h1
    ms = jnp.mean(h1 * h1, axis=-1, keepdims=True)
    hn = h1 * lax.rsqrt(ms + NORM_EPS) * g2_ref[...]
    hn_ref[...] = hn
    qp_ref[...] = _dot(hn.astype(BF16), wq_ref[...])

    def head(h, _):
        c0 = pl.multiple_of(h * PEER_DK, PEER_DK)
        for sub in range(tm // LANES):
            tok = slice(sub * LANES, (sub + 1) * LANES)
            q1 = qp_ref[tok, pl.ds(c0, PEER_HALF)].astype(BF16)
            q2 = qp_ref[tok, pl.ds(c0 + PEER_HALF, PEER_HALF)].astype(BF16)
            s1 = _dot_nt(keys_ref[h, 0], q1)
            s2 = _dot_nt(keys_ref[h, 1], q2)
            v1, i1 = _top16(s1, None, PEER_N_KEYS)
            v2, i2 = _top16(s2, None, PEER_N_KEYS)
            pad = _CAND_ROWS - len(_CAND_PAIRS)
            cand = jnp.concatenate([v1[a:a + 1, :] + v2[b:b + 1, :] for a, b in _CAND_PAIRS]
                                   + [jnp.full((pad, LANES), -jnp.inf, F32)], axis=0)
            eidx = jnp.concatenate([i1[a:a + 1, :] * float(PEER_N_KEYS) + i2[b:b + 1, :] for a, b in _CAND_PAIRS]
                                   + [jnp.zeros((pad, LANES), F32)], axis=0)
            sc, ei = _top16(cand, eidx, _CAND_ROWS)
            p = jnp.exp(sc - sc[0:1, :])
            r0 = pl.multiple_of(h * PEER_TOPK, PEER_TOPK)
            idx_t_ref[pl.ds(r0, PEER_TOPK), tok] = ei.astype(I32)
            gate_t_ref[pl.ds(r0, PEER_TOPK), tok] = p / jnp.sum(p, axis=0, keepdims=True)
        return 0

    lax.fori_loop(0, PEER_HEADS, head, 0)
    idx_ref[...] = idx_t_ref[...].T
    gate_ref[...] = gate_t_ref[...].T


def _mix(x, attn_n, ssm_n, w_out_bf, g2, wq_bf, keys_bf, tm):
    rows = x.shape[0]
    row_spec = lambda w: pl.BlockSpec((tm, w), lambda i: (i, 0))
    full = lambda a: pl.BlockSpec(a.shape, lambda i: (0,) * a.ndim)
    return pl.pallas_call(
        functools.partial(_mix_kernel, tm=tm),
        grid=(rows // tm,),
        in_specs=[row_spec(D_MODEL), row_spec(ATTN_WIDTH), row_spec(SSM_WIDTH), full(w_out_bf), full(g2),
                  full(wq_bf), full(keys_bf)],
        out_specs=[row_spec(D_MODEL), row_spec(D_MODEL), row_spec(PEER_SEL), row_spec(PEER_SEL)],
        out_shape=[jax.ShapeDtypeStruct((rows, D_MODEL), F32),
                   jax.ShapeDtypeStruct((rows, D_MODEL), F32),
                   jax.ShapeDtypeStruct((rows, PEER_SEL), I32),
                   jax.ShapeDtypeStruct((rows, PEER_SEL), F32)],
        scratch_shapes=[pltpu.VMEM((tm, PEER_HEADS * PEER_DK), F32),
                        pltpu.VMEM((PEER_SEL, tm), I32),
                        pltpu.VMEM((PEER_SEL, tm), F32)],
        compiler_params=pltpu.CompilerParams(dimension_semantics=("arbitrary",),
                                             vmem_limit_bytes=48 * 1024 * 1024),
        name="mix_topk",
    )(x, attn_n, ssm_n, w_out_bf, g2, wq_bf, keys_bf)


def _expert_tiles(t):
    return t.astype(BF16).reshape(t.shape[0], SUBLANES, LANES)


def _load_table(tab_hbm, tab_ref, sem):
    @pl.when(pl.program_id(0) == 0)
    def _():
        cp = pltpu.make_async_copy(tab_hbm, tab_ref, sem)
        cp.start()
        cp.wait()


def _peer_u_kernel(idx_ref, h_ref, gate_ref, tab_hbm, act_ref, tab_ref, part_a_ref, part_b_ref, sem, *, tt):
    _load_table(tab_hbm, tab_ref, sem)

    def products(t0, part_refs):
        toks = [(idx_ref.at[t0 + j], h_ref[t0 + j], part_refs[j]) for j in range(len(part_refs))]
        for k in range(PEER_SEL):
            for row, hv, part_ref in toks:
                p = tab_ref[row[k]].astype(F32) * hv
                part_ref[k * HALF_TILE:(k + 1) * HALF_TILE, :] = p[0:HALF_TILE] + p[HALF_TILE:]

    def row_sums(part_ref):
        part = part_ref[pl.ds(0, PEER_SEL, stride=HALF_TILE), :]
        for r in range(1, HALF_TILE):
            part = part + part_ref[pl.ds(r, PEER_SEL, stride=HALF_TILE), :]
        return part

    def finish(t, part):
        dots = jnp.sum(part.T, axis=0, keepdims=True)
        act_ref[t] = _gelu(dots) * gate_ref[t]

    products(0, (part_a_ref, part_b_ref))

    def pair(i, _):
        t = 2 * i
        sums_a, sums_b = row_sums(part_a_ref), row_sums(part_b_ref)
        finish(t - 2, sums_a)
        finish(t - 1, sums_b)
        products(t, (part_a_ref, part_b_ref))
        return 0

    lax.fori_loop(1, tt // 2, pair, 0)
    finish(tt - 2, row_sums(part_a_ref))
    finish(tt - 1, row_sums(part_b_ref))


def _peer_u(idx, hn3, gate3, tab, tt):
    rows = idx.shape[0]
    return pl.pallas_call(
        functools.partial(_peer_u_kernel, tt=tt),
        grid=(rows // tt,),
        in_specs=[pl.BlockSpec((tt, PEER_SEL), lambda i: (i, 0), memory_space=pltpu.SMEM),
                  pl.BlockSpec((tt, SUBLANES, LANES), lambda i: (i, 0, 0)),
                  pl.BlockSpec((tt, 1, PEER_SEL), lambda i: (i, 0, 0)),
                  pl.BlockSpec(memory_space=pl.ANY)],
        out_specs=pl.BlockSpec((tt, 1, PEER_SEL), lambda i: (i, 0, 0)),
        out_shape=jax.ShapeDtypeStruct((rows, 1, PEER_SEL), F32),
        scratch_shapes=[pltpu.VMEM(tab.shape, BF16),
                        pltpu.VMEM((PEER_SEL * HALF_TILE, LANES), F32),
                        pltpu.VMEM((PEER_SEL * HALF_TILE, LANES), F32),
                        pltpu.SemaphoreType.DMA],
        compiler_params=pltpu.CompilerParams(dimension_semantics=("arbitrary",),
                                             vmem_limit_bytes=VMEM_LIMIT_TABLE),
        name="peer_u",
    )(idx, hn3, gate3, tab)


def _peer_v_kernel(idx_ref, act_ref, h1_ref, tab_hbm, o_ref, tab_ref, *rest, tt):
    rep_refs, sem = rest[:-1], rest[-1]
    _load_table(tab_hbm, tab_ref, sem)
    n_acc = 2
    g = PEER_V_GROUP

    def replicated(t):
        return jnp.broadcast_to(act_ref[t], (PEER_SEL, LANES)).T

    def accumulate(t0):
        rows = [idx_ref.at[t0 + j] for j in range(g)]
        accs = [[jnp.zeros((SUBLANES, LANES), F32)] * n_acc for _ in range(g)]
        for k in range(PEER_SEL):
            for j, rep_ref in enumerate(rep_refs):
                w = jnp.broadcast_to(rep_ref[k:k + 1, :], (SUBLANES, LANES))
                accs[j][k % n_acc] = accs[j][k % n_acc] + w * tab_ref[rows[j][k]].astype(F32)
        for j in range(g):
            o_ref[t0 + j] = h1_ref[t0 + j] + sum(accs[j][1:], accs[j][0])

    for j, rep_ref in enumerate(rep_refs):
        rep_ref[...] = replicated(j)

    def group(i, _):
        t = g * i
        nxt = [replicated(jnp.minimum(t + g + j, tt - 1)) for j in range(g)]
        accumulate(t)
        for rep_ref, tile in zip(rep_refs, nxt):
            rep_ref[...] = tile
        return 0

    lax.fori_loop(0, tt // g, group, 0)


def _peer_v(idx, act3, h13, tab, tt):
    rows = idx.shape[0]
    tile = pl.BlockSpec((tt, SUBLANES, LANES), lambda i: (i, 0, 0))
    return pl.pallas_call(
        functools.partial(_peer_v_kernel, tt=tt),
        grid=(rows // tt,),
        in_specs=[pl.BlockSpec((tt, PEER_SEL), lambda i: (i, 0), memory_space=pltpu.SMEM),
                  pl.BlockSpec((tt, 1, PEER_SEL), lambda i: (i, 0, 0)),
                  tile, pl.BlockSpec(memory_space=pl.ANY)],
        out_specs=tile,
        out_shape=jax.ShapeDtypeStruct((rows, SUBLANES, LANES), F32),
        scratch_shapes=[pltpu.VMEM(tab.shape, BF16)] + [pltpu.VMEM((PEER_SEL, LANES), F32)] * PEER_V_GROUP
                       + [pltpu.SemaphoreType.DMA],
        compiler_params=pltpu.CompilerParams(dimension_semantics=("arbitrary",),
                                             vmem_limit_bytes=VMEM_LIMIT_TABLE),
        name="peer_v",
    )(idx, act3, h13, tab)


def _block(x2, meta_tokens, norm1_g, w_in, q_norm_g, k_norm_g, attn_sinks, ssm_a_re, ssm_a_im, ssm_log_dt,
           ssm_b_re, ssm_b_im, ssm_c_re, ssm_c_im, ssm_d, ssm_glu_w, ssm_glu_b, attn_out_g, ssm_out_g, w_out,
           norm2_g, peer_w_query, peer_sub_keys, peer_u, peer_v, *, tm_proj, tm_ssm, tm_mix, tt_peer):
    rows = x2.shape[0]
    row2 = lambda a: a.reshape(1, -1).astype(F32)
    w_in_bf = w_in.astype(BF16)
    seg = jnp.arange(QK_WIDTH) // HEAD_DIM
    ones_bd = (seg[:, None] == seg[None, :]).astype(BF16)
    scale = HEAD_DIM ** -0.5
    qkg = jnp.concatenate([jnp.tile(q_norm_g.astype(F32), N_Q_HEADS) * scale,
                           jnp.tile(k_norm_g.astype(F32), N_KV_HEADS)]).reshape(1, QK_WIDTH)
    bmat, cmat, coef = _ssm_params(ssm_a_re.astype(F32), ssm_a_im.astype(F32), ssm_log_dt.astype(F32),
                                   ssm_b_re.astype(F32), ssm_b_im.astype(F32), ssm_c_re.astype(F32),
                                   ssm_c_im.astype(F32))
    glu_w_bf = ssm_glu_w.astype(BF16)
    ssm_args = (row2(ssm_d), glu_w_bf, row2(ssm_glu_b), row2(ssm_out_g))

    h0 = jnp.concatenate([jnp.zeros((BLOCK - N_META, D_MODEL), F32), meta_tokens.astype(F32)], axis=0)
    _, k0, v0, u0 = _in_proj(h0, row2(norm1_g), w_in_bf, ones_bd, qkg, BLOCK)
    carry0 = jnp.zeros((2, 2, SUBLANES, SSM_HALF_STATES), F32)
    _, carry = _ssm(u0, bmat, cmat, coef, carry0, *ssm_args, BLOCK)

    q, k, v, u = _in_proj(x2, row2(norm1_g), w_in_bf, ones_bd, qkg, tm_proj)
    attn_n = _attention(attn_sinks.astype(F32), q, k, v, k0[BLOCK - N_META:], v0[BLOCK - N_META:],
                        row2(attn_out_g))
    ssm_n, _ = _ssm(u, bmat, cmat, coef, carry, *ssm_args, tm_ssm)

    h1, hn, idx, gate = _mix(x2, attn_n, ssm_n, w_out.astype(BF16), row2(norm2_g), peer_w_query.astype(BF16),
                             peer_sub_keys.astype(BF16), tm_mix)
    act = _peer_u(idx, hn.reshape(rows, SUBLANES, LANES), gate.reshape(rows, 1, PEER_SEL),
                  _expert_tiles(peer_u),
                  tt_peer)
    out = _peer_v(idx, act, h1.reshape(rows, SUBLANES, LANES), _expert_tiles(peer_v),
                  tt_peer)
    return out.reshape(rows, D_MODEL)


def kernel(x, meta_tokens, norm1_g, w_in, q_norm_g, k_norm_g, attn_sinks, ssm_a_re, ssm_a_im, ssm_log_dt, ssm_b_re, ssm_b_im, ssm_c_re, ssm_c_im, ssm_d, ssm_glu_w, ssm_glu_b, attn_out_g, ssm_out_g, w_out, norm2_g, peer_w_query, peer_sub_keys, peer_u, peer_v):
    b, seq, d = x.shape
    outs = []
    for bi in range(b):
        outs.append(_block(x[bi].astype(F32), meta_tokens, norm1_g[0], w_in[0], q_norm_g[0], k_norm_g[0],
                           attn_sinks[0], ssm_a_re[0], ssm_a_im[0], ssm_log_dt[0], ssm_b_re[0], ssm_b_im[0],
                           ssm_c_re[0], ssm_c_im[0], ssm_d[0], ssm_glu_w[0], ssm_glu_b[0], attn_out_g[0],
                           ssm_out_g[0], w_out[0], norm2_g[0], peer_w_query[0], peer_sub_keys[0], peer_u[0],
                           peer_v[0], tm_proj=512, tm_ssm=256, tm_mix=256, tt_peer=32))
    return jnp.stack(outs).astype(x.dtype)
```

```python
import functools
import math

import jax
import jax.numpy as jnp
from jax import lax
from jax.experimental import pallas as pl
from jax.experimental.pallas import tpu as pltpu

F32 = jnp.float32
BF16 = jnp.bfloat16
I32 = jnp.int32

D_MODEL = 1024
N_META = 16
BLOCK = 128
ATTN_WIDTH = 512
HEAD_DIM = 64
N_Q_HEADS = 8
N_KV_HEADS = 2
GQA = 4
KV_WIDTH = 128
QK_WIDTH = ATTN_WIDTH + KV_WIDTH
SSM_WIDTH = 512
SSM_CH = 16
SSM_GROUPS = 32
SSM_STATE = 64
IN_WIDTH = ATTN_WIDTH + 2 * KV_WIDTH + SSM_WIDTH
PEER_HEADS = 8
PEER_N_KEYS = 128
PEER_TOPK = 16
PEER_DK = 256
PEER_HALF = 128
PEER_SEL = PEER_HEADS * PEER_TOPK
NORM_EPS = 1e-6
MASK_VALUE = -1e30

SUBLANES = 8
LANES = 128
SSM_HALF_W = SSM_WIDTH // 2
SSM_HALF_STATES = SSM_GROUPS // 2 * SSM_STATE
HALF_TILE = SUBLANES // 2
PEER_V_GROUP = 2
VMEM_LIMIT_TABLE = 52 * 1024 * 1024


def _dot(a, b):
    return jnp.dot(a, b, preferred_element_type=F32)


def _dot_nt(a, b):
    return lax.dot_general(a, b, (((1,), (1,)), ((), ())), preferred_element_type=F32)


def _gelu(x):
    return 0.5 * x * (1.0 + lax.erf(x * (2.0 ** -0.5)))


def _in_proj_kernel(x_ref, g1_ref, w_ref, ones_ref, qkg_ref, qt_ref, k_ref, vt_ref, u_ref):
    x = x_ref[...]
    ms = jnp.mean(x * x, axis=-1, keepdims=True)
    y = (x * lax.rsqrt(ms + NORM_EPS) * g1_ref[...]).astype(BF16)
    proj = _dot(y, w_ref[...])
    qk = proj[:, :QK_WIDTH]
    ss = _dot((qk * qk).astype(BF16), ones_ref[...]) * (1.0 / HEAD_DIM)
    qkn = qk * lax.rsqrt(ss + NORM_EPS) * qkg_ref[...]
    qt_ref[...] = qkn[:, :ATTN_WIDTH].T.astype(BF16)
    k_ref[...] = qkn[:, ATTN_WIDTH:].astype(BF16)
    vt_ref[...] = proj[:, QK_WIDTH:QK_WIDTH + KV_WIDTH].T.astype(BF16)
    u_ref[...] = proj[:, QK_WIDTH + KV_WIDTH:]


def _in_proj(x, g1, w_in_bf, ones_bd, qkg, tm):
    rows = x.shape[0]
    row_spec = lambda w: pl.BlockSpec((tm, w), lambda i: (i, 0))
    col_spec = lambda w: pl.BlockSpec((w, tm), lambda i: (0, i))
    full = lambda a: pl.BlockSpec(a.shape, lambda i: (0,) * a.ndim)
    return pl.pallas_call(
        _in_proj_kernel,
        grid=(rows // tm,),
        in_specs=[row_spec(D_MODEL), full(g1), full(w_in_bf), full(ones_bd), full(qkg)],
        out_specs=[col_spec(ATTN_WIDTH), row_spec(KV_WIDTH), col_spec(KV_WIDTH), row_spec(SSM_WIDTH)],
        out_shape=[jax.ShapeDtypeStruct((ATTN_WIDTH, rows), BF16),
                   jax.ShapeDtypeStruct((rows, KV_WIDTH), BF16),
                   jax.ShapeDtypeStruct((KV_WIDTH, rows), BF16),
                   jax.ShapeDtypeStruct((rows, SSM_WIDTH), F32)],
        compiler_params=pltpu.CompilerParams(dimension_semantics=("arbitrary",)),
        name="in_proj",
    )(x, g1, w_in_bf, ones_bd, qkg)


def _attn_kernel(sink_ref, qt_ref, kp_ref, kc_ref, vtp_ref, vtc_ref, km_ref, vtm_ref, g_ref, o_ref, acc_ref):
    i = pl.program_id(0)
    group_w = GQA * BLOCK
    key = lax.broadcasted_iota(I32, (BLOCK, group_w), 0)
    qry = lax.broadcasted_iota(I32, (BLOCK, group_w), 1) & (BLOCK - 1)
    prev_ok = jnp.logical_and(key > qry, i > 0)
    cur_ok = key <= qry
    for kv in range(N_KV_HEADS):
        hs = slice(kv * HEAD_DIM, (kv + 1) * HEAD_DIM)
        heads = range(kv * GQA, (kv + 1) * GQA)
        qt = jnp.concatenate([qt_ref[h * HEAD_DIM:(h + 1) * HEAD_DIM, :] for h in heads], axis=1)
        sink = jnp.concatenate([jnp.full((1, BLOCK), sink_ref[h], F32) for h in heads], axis=1)
        sp = jnp.where(prev_ok, _dot(kp_ref[:, hs], qt), MASK_VALUE)
        sc = jnp.where(cur_ok, _dot(kc_ref[:, hs], qt), MASK_VALUE)
        sm = _dot(km_ref[:, hs], qt)
        m = jnp.maximum(jnp.maximum(jnp.max(sp, axis=0, keepdims=True), jnp.max(sc, axis=0, keepdims=True)),
                        jnp.maximum(jnp.max(sm, axis=0, keepdims=True), sink))
        pp, pc, pm = jnp.exp(sp - m), jnp.exp(sc - m), jnp.exp(sm - m)
        denom = (jnp.sum(pp, axis=0, keepdims=True) + jnp.sum(pc, axis=0, keepdims=True)
                 + jnp.sum(pm, axis=0, keepdims=True) + jnp.exp(sink - m))
        ot = (_dot(vtp_ref[hs, :], pp.astype(BF16)) + _dot(vtc_ref[hs, :], pc.astype(BF16))
              + _dot(vtm_ref[hs, :], pm.astype(BF16))) / denom
        for g, h in enumerate(heads):
            acc_ref[h * HEAD_DIM:(h + 1) * HEAD_DIM, :] = ot[:, g * BLOCK:(g + 1) * BLOCK]
    a = acc_ref[...]
    ms = jnp.mean(a * a, axis=0, keepdims=True)
    o_ref[...] = (a * lax.rsqrt(ms + NORM_EPS) * g_ref[...]).T.astype(BF16)


def _attention(sinks, qt, k, vt, k_meta, vt_meta, g_out):
    nb = k.shape[0] // BLOCK
    prev_i = lambda i: jnp.maximum(i - 1, 0)
    full = lambda a: pl.BlockSpec(a.shape, lambda i: (0,) * a.ndim)
    return pl.pallas_call(
        _attn_kernel,
        grid=(nb,),
        in_specs=[pl.BlockSpec(memory_space=pltpu.SMEM),
                  pl.BlockSpec((ATTN_WIDTH, BLOCK), lambda i: (0, i)),
                  pl.BlockSpec((BLOCK, KV_WIDTH), lambda i: (prev_i(i), 0)),
                  pl.BlockSpec((BLOCK, KV_WIDTH), lambda i: (i, 0)),
                  pl.BlockSpec((KV_WIDTH, BLOCK), lambda i: (0, prev_i(i))),
                  pl.BlockSpec((KV_WIDTH, BLOCK), lambda i: (0, i)),
                  full(k_meta), full(vt_meta), full(g_out)],
        out_specs=pl.BlockSpec((BLOCK, ATTN_WIDTH), lambda i: (i, 0)),
        out_shape=jax.ShapeDtypeStruct((k.shape[0], ATTN_WIDTH), BF16),
        scratch_shapes=[pltpu.VMEM((ATTN_WIDTH, BLOCK), F32)],
        compiler_params=pltpu.CompilerParams(dimension_semantics=("arbitrary",)),
        name="attention",
    )(sinks, qt, k, k, vt, vt, k_meta, vt_meta, g_out)


def _ssm_kernel(u_ref, bmat_ref, cmat_ref, coef_ref, carry_in_ref, d_ref, gw_ref, gb_ref, g_ref,
                o_ref, carry_out_ref, st_ref, carry_ref, y_ref, *, tm):
    @pl.when(pl.program_id(0) == 0)
    def _():
        carry_ref[...] = carry_in_ref[...]

    ns = SSM_HALF_STATES
    for h in range(2):
        uh = u_ref[:, h * SSM_HALF_W:(h + 1) * SSM_HALF_W].astype(BF16)
        st_ref[...] = _dot(uh, bmat_ref[h])

        def block(b, carry):
            cr, ci = carry
            r0 = pl.multiple_of(b * SUBLANES, SUBLANES)
            xr = st_ref[pl.ds(r0, SUBLANES), 0:ns]
            xi = st_ref[pl.ds(r0, SUBLANES), ns:2 * ns]
            for j, shift in enumerate((1, 2, 4)):
                ar, ai = coef_ref[h, j, 0], coef_ref[h, j, 1]
                sr, si = pltpu.roll(xr, shift, 0), pltpu.roll(xi, shift, 0)
                xr, xi = xr + ar * sr - ai * si, xi + ar * si + ai * sr
            pr, pi = coef_ref[h, 3, 0], coef_ref[h, 3, 1]
            xr, xi = xr + pr * cr - pi * ci, xi + pr * ci + pi * cr
            st_ref[pl.ds(r0, SUBLANES), 0:ns] = xr
            st_ref[pl.ds(r0, SUBLANES), ns:2 * ns] = xi
            return (jnp.broadcast_to(xr[SUBLANES - 1:SUBLANES, :], (SUBLANES, ns)),
                    jnp.broadcast_to(xi[SUBLANES - 1:SUBLANES, :], (SUBLANES, ns)))

        cr, ci = lax.fori_loop(0, tm // SUBLANES, block, (carry_ref[h, 0], carry_ref[h, 1]))
        carry_ref[h, 0] = cr
        carry_ref[h, 1] = ci
        y_ref[:, h * SSM_HALF_W:(h + 1) * SSM_HALF_W] = _dot(st_ref[...].astype(BF16), cmat_ref[h])

    carry_out_ref[...] = carry_ref[...]
    z = _gelu(y_ref[...] + d_ref[...] * u_ref[...])
    z = z * jax.nn.sigmoid(_dot(z.astype(BF16), gw_ref[...]) + gb_ref[...])
    ms = jnp.mean(z * z, axis=-1, keepdims=True)
    o_ref[...] = (z * lax.rsqrt(ms + NORM_EPS) * g_ref[...]).astype(BF16)


def _ssm(u, bmat, cmat, coef, carry_in, d, gw_bf, gb, g_out, tm):
    rows = u.shape[0]
    row_spec = pl.BlockSpec((tm, SSM_WIDTH), lambda i: (i, 0))
    full = lambda a: pl.BlockSpec(a.shape, lambda i: (0,) * a.ndim)
    return pl.pallas_call(
        functools.partial(_ssm_kernel, tm=tm),
        grid=(rows // tm,),
        in_specs=[row_spec, full(bmat), full(cmat), full(coef), full(carry_in), full(d), full(gw_bf), full(gb),
                  full(g_out)],
        out_specs=[row_spec, full(carry_in)],
        out_shape=[jax.ShapeDtypeStruct((rows, SSM_WIDTH), BF16),
                   jax.ShapeDtypeStruct(carry_in.shape, F32)],
        scratch_shapes=[pltpu.VMEM((tm, 2 * SSM_HALF_STATES), F32),
                        pltpu.VMEM(carry_in.shape, F32),
                        pltpu.VMEM((tm, SSM_WIDTH), F32)],
        compiler_params=pltpu.CompilerParams(dimension_semantics=("arbitrary",)),
        name="ssm",
    )(u, bmat, cmat, coef, carry_in, d, gw_bf, gb, g_out)


def _ssm_params(a_re, a_im, log_dt, b_re, b_im, c_re, c_im):
    dt = jnp.exp(log_dt)[:, None]
    mag = jnp.exp(a_re * dt)
    abar_r, abar_i = mag * jnp.cos(a_im * dt), mag * jnp.sin(a_im * dt)
    den = a_re * a_re + a_im * a_im
    nr, ni = abar_r - 1.0, abar_i
    coef_r = (nr * a_re + ni * a_im) / den
    coef_i = (ni * a_re - nr * a_im) / den
    bbar_r = coef_r[..., None] * b_re - coef_i[..., None] * b_im
    bbar_i = coef_r[..., None] * b_im + coef_i[..., None] * b_re
    gh = SSM_GROUPS // 2
    eye = jnp.eye(gh, dtype=F32)

    def half_b(bb):
        return jnp.einsum('gnp,gk->gpkn', bb, eye).reshape(gh * SSM_CH, gh * SSM_STATE)

    def half_c(cc):
        return jnp.einsum('gpn,gk->gnkp', cc, eye).reshape(gh * SSM_STATE, gh * SSM_CH)

    bmat = jnp.stack([jnp.concatenate([half_b(bbar_r[s]), half_b(bbar_i[s])], axis=1)
                      for s in (slice(0, gh), slice(gh, None))]).astype(BF16)
    cmat = jnp.stack([jnp.concatenate([half_c(c_re[s]), -half_c(c_im[s])], axis=0)
                      for s in (slice(0, gh), slice(gh, None))]).astype(BF16)

    def cpow(k):
        m = mag ** k
        return m * jnp.cos(a_im * dt * k), m * jnp.sin(a_im * dt * k)

    rows = jnp.arange(SUBLANES)
    coefs = []
    for shift in (1, 2, 4):
        pr, pi = cpow(float(shift))
        keep = (rows >= shift).astype(F32)[:, None, None]
        coefs.append(jnp.stack([keep * pr[None], keep * pi[None]]))
    pw = (rows + 1).astype(F32)[:, None, None]
    m = mag[None] ** pw
    coefs.append(jnp.stack([m * jnp.cos(a_im[None] * dt[None] * pw), m * jnp.sin(a_im[None] * dt[None] * pw)]))
    coef = jnp.stack(coefs)
    coef = coef.reshape(4, 2, SUBLANES, 2, SSM_HALF_STATES).transpose(3, 0, 1, 2, 4)
    return bmat, cmat, coef


def _top16(s, payload, n_rows):
    rowid = lax.broadcasted_iota(I32, s.shape, 0).astype(F32)
    vals, pays = [], []
    for _ in range(PEER_TOPK):
        m = jnp.max(s, axis=0, keepdims=True)
        pos = jnp.min(jnp.where(s == m, rowid, float(n_rows)), axis=0, keepdims=True)
        sel = rowid == pos
        vals.append(m)
        pays.append(pos if payload is None else jnp.max(jnp.where(sel, payload, -1.0), axis=0, keepdims=True))
        s = jnp.where(sel, -jnp.inf, s)
    return jnp.concatenate(vals, axis=0), jnp.concatenate(pays, axis=0)


_CAND_PAIRS = [(a, b) for a in range(PEER_TOPK) for b in range(PEER_TOPK) if (a + 1) * (b + 1) <= PEER_TOPK]
_CAND_ROWS = -(-len(_CAND_PAIRS) // SUBLANES) * SUBLANES


def _mix_kernel(x_ref, a_ref, s_ref, wo_ref, g2_ref, wq_ref, keys_ref, h1_ref, hn_ref, idx_ref, gate_ref,
                qp_ref, idx_t_ref, gate_t_ref, *, tm):
    mix = _dot(a_ref[...], wo_ref[0:ATTN_WIDTH, :]) + _dot(s_ref[...], wo_ref[ATTN_WIDTH:, :])
    h1 = x_ref[...] + mix
    h1_ref[...] = h1
    ms = jnp.mean(h1 * h1, axis=-1, keepdims=True)
    hn = h1 * lax.rsqrt(ms + NORM_EPS) * g2_ref[...]
    hn_ref[...] = hn
    qp_ref[...] = _dot(hn.astype(BF16), wq_ref[...])

    def retrieve(h, tok):
        c0 = pl.multiple_of(h * PEER_DK, PEER_DK)
        q1 = qp_ref[tok, pl.ds(c0, PEER_HALF)].astype(BF16)
        q2 = qp_ref[tok, pl.ds(c0 + PEER_HALF, PEER_HALF)].astype(BF16)
        s1 = _dot_nt(keys_ref[h, 0], q1)
        s2 = _dot_nt(keys_ref[h, 1], q2)
        v1, i1 = _top16(s1, None, PEER_N_KEYS)
        v2, i2 = _top16(s2, None, PEER_N_KEYS)
        pad = _CAND_ROWS - len(_CAND_PAIRS)
        cand = jnp.concatenate([v1[a:a + 1, :] + v2[b:b + 1, :] for a, b in _CAND_PAIRS]
                               + [jnp.full((pad, LANES), -jnp.inf, F32)], axis=0)
        eidx = jnp.concatenate([i1[a:a + 1, :] * float(PEER_N_KEYS) + i2[b:b + 1, :] for a, b in _CAND_PAIRS]
                               + [jnp.zeros((pad, LANES), F32)], axis=0)
        sc, ei = _top16(cand, eidx, _CAND_ROWS)
        p = jnp.exp(sc - sc[0:1, :])
        r0 = pl.multiple_of(h * PEER_TOPK, PEER_TOPK)
        idx_t_ref[pl.ds(r0, PEER_TOPK), tok] = ei.astype(I32)
        gate_t_ref[pl.ds(r0, PEER_TOPK), tok] = p / jnp.sum(p, axis=0, keepdims=True)

    def head_pair(i, _):
        for h in (2 * i, 2 * i + 1):
            for sub in range(tm // LANES):
                retrieve(h, slice(sub * LANES, (sub + 1) * LANES))
        return 0

    lax.fori_loop(0, PEER_HEADS // 2, head_pair, 0)
    idx_ref[...] = idx_t_ref[...].T
    gate_ref[...] = gate_t_ref[...].T


def _mix(x, attn_n, ssm_n, w_out_bf, g2, wq_bf, keys_bf, tm):
    rows = x.shape[0]
    row_spec = lambda w: pl.BlockSpec((tm, w), lambda i: (i, 0))
    full = lambda a: pl.BlockSpec(a.shape, lambda i: (0,) * a.ndim)
    return pl.pallas_call(
        functools.partial(_mix_kernel, tm=tm),
        grid=(rows // tm,),
        in_specs=[row_spec(D_MODEL), row_spec(ATTN_WIDTH), row_spec(SSM_WIDTH), full(w_out_bf), full(g2),
                  full(wq_bf), full(keys_bf)],
        out_specs=[row_spec(D_MODEL), row_spec(D_MODEL), row_spec(PEER_SEL), row_spec(PEER_SEL)],
        out_shape=[jax.ShapeDtypeStruct((rows, D_MODEL), F32),
                   jax.ShapeDtypeStruct((rows, D_MODEL), F32),
                   jax.ShapeDtypeStruct((rows, PEER_SEL), I32),
                   jax.ShapeDtypeStruct((rows, PEER_SEL), F32)],
        scratch_shapes=[pltpu.VMEM((tm, PEER_HEADS * PEER_DK), F32),
                        pltpu.VMEM((PEER_SEL, tm), I32),
                        pltpu.VMEM((PEER_SEL, tm), F32)],
        compiler_params=pltpu.CompilerParams(dimension_semantics=("arbitrary",),
                                             vmem_limit_bytes=48 * 1024 * 1024),
        name="mix_topk",
    )(x, attn_n, ssm_n, w_out_bf, g2, wq_bf, keys_bf)


def _expert_tiles(t):
    return t.astype(BF16).reshape(t.shape[0], SUBLANES, LANES)


def _load_table(tab_hbm, tab_ref, sem):
    @pl.when(pl.program_id(0) == 0)
    def _():
        cp = pltpu.make_async_copy(tab_hbm, tab_ref, sem)
        cp.start()
        cp.wait()


def _peer_u_kernel(idx_ref, h_ref, gate_ref, tab_hbm, act_ref, tab_ref, part_a_ref, part_b_ref, sem, *, tt):
    _load_table(tab_hbm, tab_ref, sem)

    def products(t0, part_refs):
        toks = [(idx_ref.at[t0 + j], h_ref[t0 + j], part_refs[j]) for j in range(len(part_refs))]
        for k in range(PEER_SEL):
            for row, hv, part_ref in toks:
                p = tab_ref[row[k]].astype(F32) * hv
                part_ref[k * HALF_TILE:(k + 1) * HALF_TILE, :] = p[0:HALF_TILE] + p[HALF_TILE:]

    def row_sums(part_ref):
        part = part_ref[pl.ds(0, PEER_SEL, stride=HALF_TILE), :]
        for r in range(1, HALF_TILE):
            part = part + part_ref[pl.ds(r, PEER_SEL, stride=HALF_TILE), :]
        return part

    def finish(t, part):
        dots = jnp.sum(part.T, axis=0, keepdims=True)
        act_ref[t] = _gelu(dots) * gate_ref[t]

    products(0, (part_a_ref, part_b_ref))

    def pair(i, _):
        t = 2 * i
        sums_a, sums_b = row_sums(part_a_ref), row_sums(part_b_ref)
        finish(t - 2, sums_a)
        finish(t - 1, sums_b)
        products(t, (part_a_ref, part_b_ref))
        return 0

    lax.fori_loop(1, tt // 2, pair, 0)
    finish(tt - 2, row_sums(part_a_ref))
    finish(tt - 1, row_sums(part_b_ref))


def _peer_u(idx, hn3, gate3, tab, tt):
    rows = idx.shape[0]
    return pl.pallas_call(
        functools.partial(_peer_u_kernel, tt=tt),
        grid=(rows // tt,),
        in_specs=[pl.BlockSpec((tt, PEER_SEL), lambda i: (i, 0), memory_space=pltpu.SMEM),
                  pl.BlockSpec((tt, SUBLANES, LANES), lambda i: (i, 0, 0)),
                  pl.BlockSpec((tt, 1, PEER_SEL), lambda i: (i, 0, 0)),
                  pl.BlockSpec(memory_space=pl.ANY)],
        out_specs=pl.BlockSpec((tt, 1, PEER_SEL), lambda i: (i, 0, 0)),
        out_shape=jax.ShapeDtypeStruct((rows, 1, PEER_SEL), F32),
        scratch_shapes=[pltpu.VMEM(tab.shape, BF16),
                        pltpu.VMEM((PEER_SEL * HALF_TILE, LANES), F32),
                        pltpu.VMEM((PEER_SEL * HALF_TILE, LANES), F32),
                        pltpu.SemaphoreType.DMA],
        compiler_params=pltpu.CompilerParams(dimension_semantics=("arbitrary",),
                                             vmem_limit_bytes=VMEM_LIMIT_TABLE),
        name="peer_u",
    )(idx, hn3, gate3, tab)


def _peer_v_kernel(idx_ref, act_ref, h1_ref, tab_hbm, o_ref, tab_ref, *rest, tt):
    rep_refs, sem = rest[:-1], rest[-1]
    _load_table(tab_hbm, tab_ref, sem)
    n_acc = 2
    g = PEER_V_GROUP

    def replicated(t):
        return jnp.broadcast_to(act_ref[t], (PEER_SEL, LANES)).T

    def accumulate(t0):
        rows = [idx_ref.at[t0 + j] for j in range(g)]
        accs = [[jnp.zeros((SUBLANES, LANES), F32)] * n_acc for _ in range(g)]
        for k in range(PEER_SEL):
            for j, rep_ref in enumerate(rep_refs):
                w = jnp.broadcast_to(rep_ref[k:k + 1, :], (SUBLANES, LANES))
                accs[j][k % n_acc] = accs[j][k % n_acc] + w * tab_ref[rows[j][k]].astype(F32)
        for j in range(g):
            o_ref[t0 + j] = h1_ref[t0 + j] + sum(accs[j][1:], accs[j][0])

    for j, rep_ref in enumerate(rep_refs):
        rep_ref[...] = replicated(j)

    def group(i, _):
        t = g * i
        nxt = [replicated(jnp.minimum(t + g + j, tt - 1)) for j in range(g)]
        accumulate(t)
        for rep_ref, tile in zip(rep_refs, nxt):
            rep_ref[...] = tile
        return 0

    lax.fori_loop(0, tt // g, group, 0)


def _peer_v(idx, act3, h13, tab, tt):
    rows = idx.shape[0]
    tile = pl.BlockSpec((tt, SUBLANES, LANES), lambda i: (i, 0, 0))
    return pl.pallas_call(
        functools.partial(_peer_v_kernel, tt=tt),
        grid=(rows // tt,),
        in_specs=[pl.BlockSpec((tt, PEER_SEL), lambda i: (i, 0), memory_space=pltpu.SMEM),
                  pl.BlockSpec((tt, 1, PEER_SEL), lambda i: (i, 0, 0)),
                  tile, pl.BlockSpec(memory_space=pl.ANY)],
        out_specs=tile,
        out_shape=jax.ShapeDtypeStruct((rows, SUBLANES, LANES), F32),
        scratch_shapes=[pltpu.VMEM(tab.shape, BF16)] + [pltpu.VMEM((PEER_SEL, LANES), F32)] * PEER_V_GROUP
                       + [pltpu.SemaphoreType.DMA],
        compiler_params=pltpu.CompilerParams(dimension_semantics=("arbitrary",),
                                             vmem_limit_bytes=VMEM_LIMIT_TABLE),
        name="peer_v",
    )(idx, act3, h13, tab)


def _block(x2, meta_tokens, norm1_g, w_in, q_norm_g, k_norm_g, attn_sinks, ssm_a_re, ssm_a_im, ssm_log_dt,
           ssm_b_re, ssm_b_im, ssm_c_re, ssm_c_im, ssm_d, ssm_glu_w, ssm_glu_b, attn_out_g, ssm_out_g, w_out,
           norm2_g, peer_w_query, peer_sub_keys, peer_u, peer_v, *, tm_proj, tm_ssm, tm_mix, tt_peer):
    rows = x2.shape[0]
    row2 = lambda a: a.reshape(1, -1).astype(F32)
    w_in_bf = w_in.astype(BF16)
    seg = jnp.arange(QK_WIDTH) // HEAD_DIM
    ones_bd = (seg[:, None] == seg[None, :]).astype(BF16)
    scale = HEAD_DIM ** -0.5
    qkg = jnp.concatenate([jnp.tile(q_norm_g.astype(F32), N_Q_HEADS) * scale,
                           jnp.tile(k_norm_g.astype(F32), N_KV_HEADS)]).reshape(1, QK_WIDTH)
    bmat, cmat, coef = _ssm_params(ssm_a_re.astype(F32), ssm_a_im.astype(F32), ssm_log_dt.astype(F32),
                                   ssm_b_re.astype(F32), ssm_b_im.astype(F32), ssm_c_re.astype(F32),
                                   ssm_c_im.astype(F32))
    glu_w_bf = ssm_glu_w.astype(BF16)
    ssm_args = (row2(ssm_d), glu_w_bf, row2(ssm_glu_b), row2(ssm_out_g))

    h0 = jnp.concatenate([jnp.zeros((BLOCK - N_META, D_MODEL), F32), meta_tokens.astype(F32)], axis=0)
    _, k0, vt0, u0 = _in_proj(h0, row2(norm1_g), w_in_bf, ones_bd, qkg, BLOCK)
    carry0 = jnp.zeros((2, 2, SUBLANES, SSM_HALF_STATES), F32)
    _, carry = _ssm(u0, bmat, cmat, coef, carry0, *ssm_args, BLOCK)

    qt, k, vt, u = _in_proj(x2, row2(norm1_g), w_in_bf, ones_bd, qkg, tm_proj)
    g_attn = jnp.broadcast_to(attn_out_g.astype(F32)[:, None], (ATTN_WIDTH, BLOCK))
    attn_n = _attention(attn_sinks.astype(F32), qt, k, vt, k0[BLOCK - N_META:], vt0[:, BLOCK - N_META:], g_attn)
    ssm_n, _ = _ssm(u, bmat, cmat, coef, carry, *ssm_args, tm_ssm)

    h1, hn, idx, gate = _mix(x2, attn_n, ssm_n, w_out.astype(BF16), row2(norm2_g), peer_w_query.astype(BF16),
                             peer_sub_keys.astype(BF16), tm_mix)
    act = _peer_u(idx, hn.reshape(rows, SUBLANES, LANES), gate.reshape(rows, 1, PEER_SEL),
                  _expert_tiles(peer_u),
                  tt_peer)
    out = _peer_v(idx, act, h1.reshape(rows, SUBLANES, LANES), _expert_tiles(peer_v),
                  tt_peer)
    return out.reshape(rows, D_MODEL)


def kernel(x, meta_tokens, norm1_g, w_in, q_norm_g, k_norm_g, attn_sinks, ssm_a_re, ssm_a_im, ssm_log_dt, ssm_b_re, ssm_b_im, ssm_c_re, ssm_c_im, ssm_d, ssm_glu_w, ssm_glu_b, attn_out_g, ssm_out_g, w_out, norm2_g, peer_w_query, peer_sub_keys, peer_u, peer_v):
    b, seq, d = x.shape
    outs = []
    for bi in range(b):
        outs.append(_block(x[bi].astype(F32), meta_tokens, norm1_g[0], w_in[0], q_norm_g[0], k_norm_g[0],
                           attn_sinks[0], ssm_a_re[0], ssm_a_im[0], ssm_log_dt[0], ssm_b_re[0], ssm_b_im[0],
                           ssm_c_re[0], ssm_c_im[0], ssm_d[0], ssm_glu_w[0], ssm_glu_b[0], attn_out_g[0],
                           ssm_out_g[0], w_out[0], norm2_g[0], peer_w_query[0], peer_sub_keys[0], peer_u[0],
                           peer_v[0], tm_proj=512, tm_ssm=256, tm_mix=256, tt_peer=32))
    return jnp.stack(outs).astype(x.dtype)
```

```python
import functools
import math

import jax
import jax.numpy as jnp
from jax import lax
from jax.experimental import pallas as pl
from jax.experimental.pallas import tpu as pltpu
from jax.experimental.pallas import tpu_sc as plsc

F32 = jnp.float32
BF16 = jnp.bfloat16
I32 = jnp.int32

D_MODEL = 1024
N_META = 16
BLOCK = 128
ATTN_WIDTH = 512
HEAD_DIM = 64
N_Q_HEADS = 8
N_KV_HEADS = 2
GQA = 4
KV_WIDTH = 128
QK_WIDTH = ATTN_WIDTH + KV_WIDTH
SSM_WIDTH = 512
SSM_CH = 16
SSM_GROUPS = 32
SSM_STATE = 64
IN_WIDTH = ATTN_WIDTH + 2 * KV_WIDTH + SSM_WIDTH
PEER_HEADS = 8
PEER_N_KEYS = 128
PEER_TOPK = 16
PEER_DK = 256
PEER_HALF = 128
PEER_SEL = PEER_HEADS * PEER_TOPK
NORM_EPS = 1e-6
MASK_VALUE = -1e30

SUBLANES = 8
LANES = 128
SSM_HALF_W = SSM_WIDTH // 2
SSM_HALF_STATES = SSM_GROUPS // 2 * SSM_STATE
HALF_TILE = SUBLANES // 2
PEER_V_GROUP = 2
VMEM_LIMIT_TABLE = 52 * 1024 * 1024


def _dot(a, b):
    return jnp.dot(a, b, preferred_element_type=F32)


def _dot_nt(a, b):
    return lax.dot_general(a, b, (((1,), (1,)), ((), ())), preferred_element_type=F32)


def _gelu(x):
    return 0.5 * x * (1.0 + lax.erf(x * (2.0 ** -0.5)))


def _in_proj_kernel(x_ref, g1_ref, w_ref, ones_ref, qkg_ref, qt_ref, k_ref, vt_ref, u_ref):
    x = x_ref[...]
    ms = jnp.mean(x * x, axis=-1, keepdims=True)
    y = (x * lax.rsqrt(ms + NORM_EPS) * g1_ref[...]).astype(BF16)
    proj = _dot(y, w_ref[...])
    qk = proj[:, :QK_WIDTH]
    ss = _dot((qk * qk).astype(BF16), ones_ref[...]) * (1.0 / HEAD_DIM)
    qkn = qk * lax.rsqrt(ss + NORM_EPS) * qkg_ref[...]
    qt_ref[...] = qkn[:, :ATTN_WIDTH].T.astype(BF16)
    k_ref[...] = qkn[:, ATTN_WIDTH:].astype(BF16)
    vt_ref[...] = proj[:, QK_WIDTH:QK_WIDTH + KV_WIDTH].T.astype(BF16)
    u_ref[...] = proj[:, QK_WIDTH + KV_WIDTH:]


def _in_proj(x, g1, w_in_bf, ones_bd, qkg, tm):
    rows = x.shape[0]
    row_spec = lambda w: pl.BlockSpec((tm, w), lambda i: (i, 0))
    col_spec = lambda w: pl.BlockSpec((w, tm), lambda i: (0, i))
    full = lambda a: pl.BlockSpec(a.shape, lambda i: (0,) * a.ndim)
    return pl.pallas_call(
        _in_proj_kernel,
        grid=(rows // tm,),
        in_specs=[row_spec(D_MODEL), full(g1), full(w_in_bf), full(ones_bd), full(qkg)],
        out_specs=[col_spec(ATTN_WIDTH), row_spec(KV_WIDTH), col_spec(KV_WIDTH), row_spec(SSM_WIDTH)],
        out_shape=[jax.ShapeDtypeStruct((ATTN_WIDTH, rows), BF16),
                   jax.ShapeDtypeStruct((rows, KV_WIDTH), BF16),
                   jax.ShapeDtypeStruct((KV_WIDTH, rows), BF16),
                   jax.ShapeDtypeStruct((rows, SSM_WIDTH), F32)],
        compiler_params=pltpu.CompilerParams(dimension_semantics=("arbitrary",)),
        name="in_proj",
    )(x, g1, w_in_bf, ones_bd, qkg)


def _attn_kernel(sink_ref, qt_ref, kp_ref, kc_ref, vtp_ref, vtc_ref, km_ref, vtm_ref, g_ref, o_ref, acc_ref):
    i = pl.program_id(0)
    group_w = GQA * BLOCK
    key = lax.broadcasted_iota(I32, (BLOCK, group_w), 0)
    qry = lax.broadcasted_iota(I32, (BLOCK, group_w), 1) & (BLOCK - 1)
    prev_ok = jnp.logical_and(key > qry, i > 0)
    cur_ok = key <= qry
    for kv in range(N_KV_HEADS):
        hs = slice(kv * HEAD_DIM, (kv + 1) * HEAD_DIM)
        heads = range(kv * GQA, (kv + 1) * GQA)
        qt = jnp.concatenate([qt_ref[h * HEAD_DIM:(h + 1) * HEAD_DIM, :] for h in heads], axis=1)
        sink = jnp.concatenate([jnp.full((1, BLOCK), sink_ref[h], F32) for h in heads], axis=1)
        sp = jnp.where(prev_ok, _dot(kp_ref[:, hs], qt), MASK_VALUE)
        sc = jnp.where(cur_ok, _dot(kc_ref[:, hs], qt), MASK_VALUE)
        sm = _dot(km_ref[:, hs], qt)
        m = jnp.maximum(jnp.maximum(jnp.max(sp, axis=0, keepdims=True), jnp.max(sc, axis=0, keepdims=True)),
                        jnp.maximum(jnp.max(sm, axis=0, keepdims=True), sink))
        pp, pc, pm = jnp.exp(sp - m), jnp.exp(sc - m), jnp.exp(sm - m)
        denom = (jnp.sum(pp, axis=0, keepdims=True) + jnp.sum(pc, axis=0, keepdims=True)
                 + jnp.sum(pm, axis=0, keepdims=True) + jnp.exp(sink - m))
        ot = (_dot(vtp_ref[hs, :], pp.astype(BF16)) + _dot(vtc_ref[hs, :], pc.astype(BF16))
              + _dot(vtm_ref[hs, :], pm.astype(BF16))) / denom
        for g, h in enumerate(heads):
            acc_ref[h * HEAD_DIM:(h + 1) * HEAD_DIM, :] = ot[:, g * BLOCK:(g + 1) * BLOCK]
    a = acc_ref[...]
    ms = jnp.mean(a * a, axis=0, keepdims=True)
    o_ref[...] = (a * lax.rsqrt(ms + NORM_EPS) * g_ref[...]).T.astype(BF16)


def _attention(sinks, qt, k, vt, k_meta, vt_meta, g_out):
    nb = k.shape[0] // BLOCK
    prev_i = lambda i: jnp.maximum(i - 1, 0)
    full = lambda a: pl.BlockSpec(a.shape, lambda i: (0,) * a.ndim)
    return pl.pallas_call(
        _attn_kernel,
        grid=(nb,),
        in_specs=[pl.BlockSpec(memory_space=pltpu.SMEM),
                  pl.BlockSpec((ATTN_WIDTH, BLOCK), lambda i: (0, i)),
                  pl.BlockSpec((BLOCK, KV_WIDTH), lambda i: (prev_i(i), 0)),
                  pl.BlockSpec((BLOCK, KV_WIDTH), lambda i: (i, 0)),
                  pl.BlockSpec((KV_WIDTH, BLOCK), lambda i: (0, prev_i(i))),
                  pl.BlockSpec((KV_WIDTH, BLOCK), lambda i: (0, i)),
                  full(k_meta), full(vt_meta), full(g_out)],
        out_specs=pl.BlockSpec((BLOCK, ATTN_WIDTH), lambda i: (i, 0)),
        out_shape=jax.ShapeDtypeStruct((k.shape[0], ATTN_WIDTH), BF16),
        scratch_shapes=[pltpu.VMEM((ATTN_WIDTH, BLOCK), F32)],
        compiler_params=pltpu.CompilerParams(dimension_semantics=("arbitrary",)),
        name="attention",
    )(sinks, qt, k, k, vt, vt, k_meta, vt_meta, g_out)


def _ssm_kernel(u_ref, bmat_ref, cmat_ref, coef_ref, carry_in_ref, d_ref, gw_ref, gb_ref, g_ref,
                o_ref, carry_out_ref, st_ref, carry_ref, y_ref, *, tm):
    @pl.when(pl.program_id(0) == 0)
    def _():
        carry_ref[...] = carry_in_ref[...]

    ns = SSM_HALF_STATES
    for h in range(2):
        uh = u_ref[:, h * SSM_HALF_W:(h + 1) * SSM_HALF_W].astype(BF16)
        st_ref[...] = _dot(uh, bmat_ref[h])

        def block(b, carry):
            cr, ci = carry
            r0 = pl.multiple_of(b * SUBLANES, SUBLANES)
            xr = st_ref[pl.ds(r0, SUBLANES), 0:ns]
            xi = st_ref[pl.ds(r0, SUBLANES), ns:2 * ns]
            for j, shift in enumerate((1, 2, 4)):
                ar, ai = coef_ref[h, j, 0], coef_ref[h, j, 1]
                sr, si = pltpu.roll(xr, shift, 0), pltpu.roll(xi, shift, 0)
                xr, xi = xr + ar * sr - ai * si, xi + ar * si + ai * sr
            pr, pi = coef_ref[h, 3, 0], coef_ref[h, 3, 1]
            xr, xi = xr + pr * cr - pi * ci, xi + pr * ci + pi * cr
            st_ref[pl.ds(r0, SUBLANES), 0:ns] = xr
            st_ref[pl.ds(r0, SUBLANES), ns:2 * ns] = xi
            return (jnp.broadcast_to(xr[SUBLANES - 1:SUBLANES, :], (SUBLANES, ns)),
                    jnp.broadcast_to(xi[SUBLANES - 1:SUBLANES, :], (SUBLANES, ns)))

        cr, ci = lax.fori_loop(0, tm // SUBLANES, block, (carry_ref[h, 0], carry_ref[h, 1]))
        carry_ref[h, 0] = cr
        carry_ref[h, 1] = ci
        y_ref[:, h * SSM_HALF_W:(h + 1) * SSM_HALF_W] = _dot(st_ref[...].astype(BF16), cmat_ref[h])

    carry_out_ref[...] = carry_ref[...]
    z = _gelu(y_ref[...] + d_ref[...] * u_ref[...])
    z = z * jax.nn.sigmoid(_dot(z.astype(BF16), gw_ref[...]) + gb_ref[...])
    ms = jnp.mean(z * z, axis=-1, keepdims=True)
    o_ref[...] = (z * lax.rsqrt(ms + NORM_EPS) * g_ref[...]).astype(BF16)


def _ssm(u, bmat, cmat, coef, carry_in, d, gw_bf, gb, g_out, tm):
    rows = u.shape[0]
    row_spec = pl.BlockSpec((tm, SSM_WIDTH), lambda i: (i, 0))
    full = lambda a: pl.BlockSpec(a.shape, lambda i: (0,) * a.ndim)
    return pl.pallas_call(
        functools.partial(_ssm_kernel, tm=tm),
        grid=(rows // tm,),
        in_specs=[row_spec, full(bmat), full(cmat), full(coef), full(carry_in), full(d), full(gw_bf), full(gb),
                  full(g_out)],
        out_specs=[row_spec, full(carry_in)],
        out_shape=[jax.ShapeDtypeStruct((rows, SSM_WIDTH), BF16),
                   jax.ShapeDtypeStruct(carry_in.shape, F32)],
        scratch_shapes=[pltpu.VMEM((tm, 2 * SSM_HALF_STATES), F32),
                        pltpu.VMEM(carry_in.shape, F32),
                        pltpu.VMEM((tm, SSM_WIDTH), F32)],
        compiler_params=pltpu.CompilerParams(dimension_semantics=("arbitrary",)),
        name="ssm",
    )(u, bmat, cmat, coef, carry_in, d, gw_bf, gb, g_out)


def _ssm_params(a_re, a_im, log_dt, b_re, b_im, c_re, c_im):
    dt = jnp.exp(log_dt)[:, None]
    mag = jnp.exp(a_re * dt)
    abar_r, abar_i = mag * jnp.cos(a_im * dt), mag * jnp.sin(a_im * dt)
    den = a_re * a_re + a_im * a_im
    nr, ni = abar_r - 1.0, abar_i
    coef_r = (nr * a_re + ni * a_im) / den
    coef_i = (ni * a_re - nr * a_im) / den
    bbar_r = coef_r[..., None] * b_re - coef_i[..., None] * b_im
    bbar_i = coef_r[..., None] * b_im + coef_i[..., None] * b_re
    gh = SSM_GROUPS // 2
    eye = jnp.eye(gh, dtype=F32)

    def half_b(bb):
        return jnp.einsum('gnp,gk->gpkn', bb, eye).reshape(gh * SSM_CH, gh * SSM_STATE)

    def half_c(cc):
        return jnp.einsum('gpn,gk->gnkp', cc, eye).reshape(gh * SSM_STATE, gh * SSM_CH)

    bmat = jnp.stack([jnp.concatenate([half_b(bbar_r[s]), half_b(bbar_i[s])], axis=1)
                      for s in (slice(0, gh), slice(gh, None))]).astype(BF16)
    cmat = jnp.stack([jnp.concatenate([half_c(c_re[s]), -half_c(c_im[s])], axis=0)
                      for s in (slice(0, gh), slice(gh, None))]).astype(BF16)

    def cpow(k):
        m = mag ** k
        return m * jnp.cos(a_im * dt * k), m * jnp.sin(a_im * dt * k)

    rows = jnp.arange(SUBLANES)
    coefs = []
    for shift in (1, 2, 4):
        pr, pi = cpow(float(shift))
        keep = (rows >= shift).astype(F32)[:, None, None]
        coefs.append(jnp.stack([keep * pr[None], keep * pi[None]]))
    pw = (rows + 1).astype(F32)[:, None, None]
    m = mag[None] ** pw
    coefs.append(jnp.stack([m * jnp.cos(a_im[None] * dt[None] * pw), m * jnp.sin(a_im[None] * dt[None] * pw)]))
    coef = jnp.stack(coefs)
    coef = coef.reshape(4, 2, SUBLANES, 2, SSM_HALF_STATES).transpose(3, 0, 1, 2, 4)
    return bmat, cmat, coef


def _top16(s, payload, n_rows):
    rowid = lax.broadcasted_iota(I32, s.shape, 0).astype(F32)
    vals, pays = [], []
    for _ in range(PEER_TOPK):
        m = jnp.max(s, axis=0, keepdims=True)
        pos = jnp.min(jnp.where(s == m, rowid, float(n_rows)), axis=0, keepdims=True)
        sel = rowid == pos
        vals.append(m)
        pays.append(pos if payload is None else jnp.max(jnp.where(sel, payload, -1.0), axis=0, keepdims=True))
        s = jnp.where(sel, -jnp.inf, s)
    return jnp.concatenate(vals, axis=0), jnp.concatenate(pays, axis=0)


_CAND_PAIRS = [(a, b) for a in range(PEER_TOPK) for b in range(PEER_TOPK) if (a + 1) * (b + 1) <= PEER_TOPK]
_CAND_ROWS = -(-len(_CAND_PAIRS) // SUBLANES) * SUBLANES


def _mix_kernel(x_ref, a_ref, s_ref, wo_ref, g2_ref, wq_ref, keys_ref, h1_ref, hn_ref, idx_ref, gate_ref,
                qp_ref, idx_t_ref, gate_t_ref, *, tm):
    mix = _dot(a_ref[...], wo_ref[0:ATTN_WIDTH, :]) + _dot(s_ref[...], wo_ref[ATTN_WIDTH:, :])
    h1 = x_ref[...] + mix
    h1_ref[...] = h1.reshape(tm, SUBLANES, LANES)
    ms = jnp.mean(h1 * h1, axis=-1, keepdims=True)
    hn = h1 * lax.rsqrt(ms + NORM_EPS) * g2_ref[...]
    hn_ref[...] = hn.reshape(tm, SUBLANES, LANES)
    qp_ref[...] = _dot(hn.astype(BF16), wq_ref[...])

    def retrieve(h, tok):
        c0 = pl.multiple_of(h * PEER_DK, PEER_DK)
        q1 = qp_ref[tok, pl.ds(c0, PEER_HALF)].astype(BF16)
        q2 = qp_ref[tok, pl.ds(c0 + PEER_HALF, PEER_HALF)].astype(BF16)
        s1 = _dot_nt(keys_ref[h, 0], q1)
        s2 = _dot_nt(keys_ref[h, 1], q2)
        v1, i1 = _top16(s1, None, PEER_N_KEYS)
        v2, i2 = _top16(s2, None, PEER_N_KEYS)
        pad = _CAND_ROWS - len(_CAND_PAIRS)
        cand = jnp.concatenate([v1[a:a + 1, :] + v2[b:b + 1, :] for a, b in _CAND_PAIRS]
                               + [jnp.full((pad, LANES), -jnp.inf, F32)], axis=0)
        eidx = jnp.concatenate([i1[a:a + 1, :] * float(PEER_N_KEYS) + i2[b:b + 1, :] for a, b in _CAND_PAIRS]
                               + [jnp.zeros((pad, LANES), F32)], axis=0)
        sc, ei = _top16(cand, eidx, _CAND_ROWS)
        p = jnp.exp(sc - sc[0:1, :])
        r0 = pl.multiple_of(h * PEER_TOPK, PEER_TOPK)
        idx_t_ref[pl.ds(r0, PEER_TOPK), tok] = ei.astype(I32)
        gate_t_ref[pl.ds(r0, PEER_TOPK), tok] = p / jnp.sum(p, axis=0, keepdims=True)

    def head_pair(i, _):
        for h in (2 * i, 2 * i + 1):
            for sub in range(tm // LANES):
                retrieve(h, slice(sub * LANES, (sub + 1) * LANES))
        return 0

    lax.fori_loop(0, PEER_HEADS // 2, head_pair, 0)
    idx_ref[...] = idx_t_ref[...].T
    gate_ref[...] = gate_t_ref[...].T


def _mix(x, attn_n, ssm_n, w_out_bf, g2, wq_bf, keys_bf, tm):
    rows = x.shape[0]
    row_spec = lambda w: pl.BlockSpec((tm, w), lambda i: (i, 0))
    tile_spec = pl.BlockSpec((tm, SUBLANES, LANES), lambda i: (i, 0, 0))
    full = lambda a: pl.BlockSpec(a.shape, lambda i: (0,) * a.ndim)
    return pl.pallas_call(
        functools.partial(_mix_kernel, tm=tm),
        grid=(rows // tm,),
        in_specs=[row_spec(D_MODEL), row_spec(ATTN_WIDTH), row_spec(SSM_WIDTH), full(w_out_bf), full(g2),
                  full(wq_bf), full(keys_bf)],
        out_specs=[tile_spec, tile_spec, row_spec(PEER_SEL), row_spec(PEER_SEL)],
        out_shape=[jax.ShapeDtypeStruct((rows, SUBLANES, LANES), F32),
                   jax.ShapeDtypeStruct((rows, SUBLANES, LANES), F32),
                   jax.ShapeDtypeStruct((rows, PEER_SEL), I32),
                   jax.ShapeDtypeStruct((rows, PEER_SEL), F32)],
        scratch_shapes=[pltpu.VMEM((tm, PEER_HEADS * PEER_DK), F32),
                        pltpu.VMEM((PEER_SEL, tm), I32),
                        pltpu.VMEM((PEER_SEL, tm), F32)],
        compiler_params=pltpu.CompilerParams(dimension_semantics=("arbitrary",),
                                             vmem_limit_bytes=48 * 1024 * 1024),
        name="mix_topk",
    )(x, attn_n, ssm_n, w_out_bf, g2, wq_bf, keys_bf)


def _expert_tiles(t):
    return t.astype(BF16).reshape(t.shape[0], SUBLANES, LANES)


def _load_table(tab_hbm, tab_ref, sem):
    @pl.when(pl.program_id(0) == 0)
    def _():
        cp = pltpu.make_async_copy(tab_hbm, tab_ref, sem)
        cp.start()
        cp.wait()


def _peer_u_kernel(idx_ref, h_ref, gate_ref, tab_hbm, act_ref, tab_ref, part_a_ref, part_b_ref, sem, *, tt):
    _load_table(tab_hbm, tab_ref, sem)

    def products(t0, part_refs):
        toks = [(idx_ref.at[t0 + j], h_ref[t0 + j], part_refs[j]) for j in range(len(part_refs))]
        for k in range(PEER_SEL):
            for row, hv, part_ref in toks:
                p = tab_ref[row[k]].astype(F32) * hv
                part_ref[k * HALF_TILE:(k + 1) * HALF_TILE, :] = p[0:HALF_TILE] + p[HALF_TILE:]

    def row_sums(part_ref):
        part = part_ref[pl.ds(0, PEER_SEL, stride=HALF_TILE), :]
        for r in range(1, HALF_TILE):
            part = part + part_ref[pl.ds(r, PEER_SEL, stride=HALF_TILE), :]
        return part

    def finish(t, part):
        dots = jnp.sum(part.T, axis=0, keepdims=True)
        act_ref[t] = _gelu(dots) * gate_ref[t]

    products(0, (part_a_ref, part_b_ref))

    def pair(i, _):
        t = 2 * i
        sums_a, sums_b = row_sums(part_a_ref), row_sums(part_b_ref)
        finish(t - 2, sums_a)
        finish(t - 1, sums_b)
        products(t, (part_a_ref, part_b_ref))
        return 0

    lax.fori_loop(1, tt // 2, pair, 0)
    finish(tt - 2, row_sums(part_a_ref))
    finish(tt - 1, row_sums(part_b_ref))


def _peer_u(idx, hn3, gate3, tab, tt):
    rows = idx.shape[0]
    return pl.pallas_call(
        functools.partial(_peer_u_kernel, tt=tt),
        grid=(rows // tt,),
        in_specs=[pl.BlockSpec((tt, PEER_SEL), lambda i: (i, 0), memory_space=pltpu.SMEM),
                  pl.BlockSpec((tt, SUBLANES, LANES), lambda i: (i, 0, 0)),
                  pl.BlockSpec((tt, 1, PEER_SEL), lambda i: (i, 0, 0)),
                  pl.BlockSpec(memory_space=pl.ANY)],
        out_specs=pl.BlockSpec((tt, 1, PEER_SEL), lambda i: (i, 0, 0)),
        out_shape=jax.ShapeDtypeStruct((rows, 1, PEER_SEL), F32),
        scratch_shapes=[pltpu.VMEM(tab.shape, BF16),
                        pltpu.VMEM((PEER_SEL * HALF_TILE, LANES), F32),
                        pltpu.VMEM((PEER_SEL * HALF_TILE, LANES), F32),
                        pltpu.SemaphoreType.DMA],
        compiler_params=pltpu.CompilerParams(dimension_semantics=("arbitrary",),
                                             vmem_limit_bytes=VMEM_LIMIT_TABLE),
        name="peer_u",
    )(idx, hn3, gate3, tab)


def _peer_v_kernel(idx_ref, act_ref, h1_ref, tab_hbm, o_ref, tab_ref, tiles_ref, *rest, tt):
    rep_refs, sem = rest[:-1], rest[-1]
    _load_table(tab_hbm, tab_ref, sem)
    n_acc = 2
    g = PEER_V_GROUP

    def replicated(t):
        return jnp.broadcast_to(act_ref[t], (PEER_SEL, LANES)).T

    def accumulate(t0):
        rows = [idx_ref.at[t0 + j] for j in range(g)]
        accs = [[jnp.zeros((SUBLANES, LANES), F32)] * n_acc for _ in range(g)]
        for k in range(PEER_SEL):
            for j, rep_ref in enumerate(rep_refs):
                w = jnp.broadcast_to(rep_ref[k:k + 1, :], (SUBLANES, LANES))
                accs[j][k % n_acc] = accs[j][k % n_acc] + w * tab_ref[rows[j][k]].astype(F32)
        for j in range(g):
            tiles_ref[t0 + j] = h1_ref[t0 + j] + sum(accs[j][1:], accs[j][0])

    for j, rep_ref in enumerate(rep_refs):
        rep_ref[...] = replicated(j)

    def group(i, _):
        t = g * i
        nxt = [replicated(jnp.minimum(t + g + j, tt - 1)) for j in range(g)]
        accumulate(t)
        for rep_ref, tile in zip(rep_refs, nxt):
            rep_ref[...] = tile
        return 0

    lax.fori_loop(0, tt // g, group, 0)
    o_ref[...] = tiles_ref[...].reshape(tt, D_MODEL)


def _peer_v(idx, act3, h13, tab, tt, first):
    rows = idx.shape[0]
    b0 = first // tt
    tile = pl.BlockSpec((tt, SUBLANES, LANES), lambda i: (i + b0, 0, 0))
    return pl.pallas_call(
        functools.partial(_peer_v_kernel, tt=tt),
        grid=((rows - first) // tt,),
        in_specs=[pl.BlockSpec((tt, PEER_SEL), lambda i: (i + b0, 0), memory_space=pltpu.SMEM),
                  pl.BlockSpec((tt, 1, PEER_SEL), lambda i: (i + b0, 0, 0)),
                  tile, pl.BlockSpec(memory_space=pl.ANY)],
        out_specs=pl.BlockSpec((tt, D_MODEL), lambda i: (i, 0)),
        out_shape=jax.ShapeDtypeStruct((rows - first, D_MODEL), F32),
        scratch_shapes=[pltpu.VMEM(tab.shape, BF16), pltpu.VMEM((tt, SUBLANES, LANES), F32)]
                       + [pltpu.VMEM((PEER_SEL, LANES), F32)] * PEER_V_GROUP
                       + [pltpu.SemaphoreType.DMA],
        compiler_params=pltpu.CompilerParams(dimension_semantics=("arbitrary",),
                                             vmem_limit_bytes=VMEM_LIMIT_TABLE),
        name="peer_v",
    )(idx, act3, h13, tab)


SC_LANES = 16
SC_WORKERS = 32
SC_ROWS = 32
SC_COLS = 128
SC_ROW_GROUP = 4
SC_SHARE_NUM, SC_SHARE_DEN = 1, 4


def _peer_v_sc_kernel(tab_hbm, idx_hbm, act_hbm, h1_hbm, out_hbm, idx_v, act_v, rows0_v, rows1_v, acc_v, sem0, sem1,
                      *, n_tok):
    wid = lax.axis_index("s") * 2 + lax.axis_index("c")
    bufs = ((rows0_v, sem0), (rows1_v, sem1))
    n_chunks = PEER_SEL // SC_ROWS

    def gather(c):
        rows_v, sem = bufs[c % 2]
        return pltpu.make_async_copy(tab_hbm.at[idx_v.at[pl.ds(c * SC_ROWS, SC_ROWS)]], rows_v, sem)

    @pl.loop(0, n_tok)
    def _(i):
        t = wid * n_tok + i
        pltpu.sync_copy(idx_hbm.at[pl.ds(t * PEER_SEL, PEER_SEL)], idx_v)
        gather(0).start()
        pltpu.sync_copy(act_hbm.at[pl.ds(t * PEER_SEL, PEER_SEL)], act_v)
        pltpu.sync_copy(h1_hbm.at[t], acc_v)
        for c in range(n_chunks):
            if c + 1 < n_chunks:
                gather(c + 1).start()
            gather(c).wait()
            rows_v = bufs[c % 2][0]

            @pl.loop(0, D_MODEL // SC_COLS)
            def _(cb):
                cols = [pl.ds(cb * SC_COLS + j * SC_LANES, SC_LANES) for j in range(SC_COLS // SC_LANES)]

                @pl.loop(0, SC_ROWS // SC_ROW_GROUP)
                def _(rg):
                    accs = [acc_v[col] for col in cols]
                    for rr in range(SC_ROW_GROUP):
                        r = rg * SC_ROW_GROUP + rr
                        w = plsc.load_gather(act_v, [jnp.full((SC_LANES,), c * SC_ROWS, I32) + r])
                        accs = [a + w * rows_v[r, col] for a, col in zip(accs, cols)]
                    for a, col in zip(accs, cols):
                        acc_v[col] = a
        pltpu.sync_copy(acc_v, out_hbm.at[t])


def _peer_v_sc(tab, idx_flat, act_flat, h1_rows):
    n = h1_rows.shape[0]
    mesh = plsc.VectorSubcoreMesh(core_axis_name="c", subcore_axis_name="s")
    return pl.kernel(
        functools.partial(_peer_v_sc_kernel, n_tok=n // SC_WORKERS),
        out_type=jax.ShapeDtypeStruct((n, D_MODEL), F32),
        mesh=mesh,
        scratch_types=[pltpu.VMEM((PEER_SEL,), I32), pltpu.VMEM((PEER_SEL,), F32),
                       pltpu.VMEM((SC_ROWS, D_MODEL), F32), pltpu.VMEM((SC_ROWS, D_MODEL), F32),
                       pltpu.VMEM((D_MODEL,), F32), pltpu.SemaphoreType.DMA, pltpu.SemaphoreType.DMA],
        compiler_params=pltpu.CompilerParams(needs_layout_passes=False),
        name="peer_v_sc",
    )(tab, idx_flat, act_flat, h1_rows)


def _block(x2, meta_tokens, norm1_g, w_in, q_norm_g, k_norm_g, attn_sinks, ssm_a_re, ssm_a_im, ssm_log_dt,
           ssm_b_re, ssm_b_im, ssm_c_re, ssm_c_im, ssm_d, ssm_glu_w, ssm_glu_b, attn_out_g, ssm_out_g, w_out,
           norm2_g, peer_w_query, peer_sub_keys, peer_u, peer_v, *, tm_proj, tm_ssm, tm_mix, tt_peer):
    rows = x2.shape[0]
    row2 = lambda a: a.reshape(1, -1).astype(F32)
    w_in_bf = w_in.astype(BF16)
    seg = jnp.arange(QK_WIDTH) // HEAD_DIM
    ones_bd = (seg[:, None] == seg[None, :]).astype(BF16)
    scale = HEAD_DIM ** -0.5
    qkg = jnp.concatenate([jnp.tile(q_norm_g.astype(F32), N_Q_HEADS) * scale,
                           jnp.tile(k_norm_g.astype(F32), N_KV_HEADS)]).reshape(1, QK_WIDTH)
    bmat, cmat, coef = _ssm_params(ssm_a_re.astype(F32), ssm_a_im.astype(F32), ssm_log_dt.astype(F32),
                                   ssm_b_re.astype(F32), ssm_b_im.astype(F32), ssm_c_re.astype(F32),
                                   ssm_c_im.astype(F32))
    glu_w_bf = ssm_glu_w.astype(BF16)
    ssm_args = (row2(ssm_d), glu_w_bf, row2(ssm_glu_b), row2(ssm_out_g))

    h0 = jnp.concatenate([jnp.zeros((BLOCK - N_META, D_MODEL), F32), meta_tokens.astype(F32)], axis=0)
    _, k0, vt0, u0 = _in_proj(h0, row2(norm1_g), w_in_bf, ones_bd, qkg, BLOCK)
    carry0 = jnp.zeros((2, 2, SUBLANES, SSM_HALF_STATES), F32)
    _, carry = _ssm(u0, bmat, cmat, coef, carry0, *ssm_args, BLOCK)

    qt, k, vt, u = _in_proj(x2, row2(norm1_g), w_in_bf, ones_bd, qkg, tm_proj)
    g_attn = jnp.broadcast_to(attn_out_g.astype(F32)[:, None], (ATTN_WIDTH, BLOCK))
    attn_n = _attention(attn_sinks.astype(F32), qt, k, vt, k0[BLOCK - N_META:], vt0[:, BLOCK - N_META:], g_attn)
    ssm_n, _ = _ssm(u, bmat, cmat, coef, carry, *ssm_args, tm_ssm)

    h1, hn, idx, gate = _mix(x2, attn_n, ssm_n, w_out.astype(BF16), row2(norm2_g), peer_w_query.astype(BF16),
                             peer_sub_keys.astype(BF16), tm_mix)
    act = _peer_u(idx, hn, gate.reshape(rows, 1, PEER_SEL),
                  _expert_tiles(peer_u),
                  tt_peer)
    n_sc = rows * SC_SHARE_NUM // SC_SHARE_DEN // (SC_WORKERS * tt_peer) * (SC_WORKERS * tt_peer)
    sc_out = _peer_v_sc(peer_v.astype(F32), idx[:n_sc].reshape(-1), act[:n_sc].reshape(-1),
                        h1[:n_sc].reshape(n_sc, D_MODEL))
    tc_out = _peer_v(idx, act, h1, _expert_tiles(peer_v), tt_peer, n_sc)
    return jnp.concatenate([sc_out, tc_out], axis=0)


def kernel(x, meta_tokens, norm1_g, w_in, q_norm_g, k_norm_g, attn_sinks, ssm_a_re, ssm_a_im, ssm_log_dt, ssm_b_re, ssm_b_im, ssm_c_re, ssm_c_im, ssm_d, ssm_glu_w, ssm_glu_b, attn_out_g, ssm_out_g, w_out, norm2_g, peer_w_query, peer_sub_keys, peer_u, peer_v):
    b, seq, d = x.shape
    outs = []
    for bi in range(b):
        outs.append(_block(x[bi].astype(F32), meta_tokens, norm1_g[0], w_in[0], q_norm_g[0], k_norm_g[0],
                           attn_sinks[0], ssm_a_re[0], ssm_a_im[0], ssm_log_dt[0], ssm_b_re[0], ssm_b_im[0],
                           ssm_c_re[0], ssm_c_im[0], ssm_d[0], ssm_glu_w[0], ssm_glu_b[0], attn_out_g[0],
                           ssm_out_g[0], w_out[0], norm2_g[0], peer_w_query[0], peer_sub_keys[0], peer_u[0],
                           peer_v[0], tm_proj=512, tm_ssm=256, tm_mix=256, tt_peer=32))
    return jnp.stack(outs).astype(x.dtype)
```

```python
import functools
import math

import jax
import jax.numpy as jnp
from jax import lax
from jax.experimental import pallas as pl
from jax.experimental.pallas import tpu as pltpu
from jax.experimental.pallas import tpu_sc as plsc

F32 = jnp.float32
BF16 = jnp.bfloat16
I32 = jnp.int32

D_MODEL = 1024
N_META = 16
BLOCK = 128
ATTN_WIDTH = 512
HEAD_DIM = 64
N_Q_HEADS = 8
N_KV_HEADS = 2
GQA = 4
KV_WIDTH = 128
QK_WIDTH = ATTN_WIDTH + KV_WIDTH
SSM_WIDTH = 512
SSM_CH = 16
SSM_GROUPS = 32
SSM_STATE = 64
IN_WIDTH = ATTN_WIDTH + 2 * KV_WIDTH + SSM_WIDTH
PEER_HEADS = 8
PEER_N_KEYS = 128
PEER_TOPK = 16
PEER_DK = 256
PEER_HALF = 128
PEER_SEL = PEER_HEADS * PEER_TOPK
NORM_EPS = 1e-6
MASK_VALUE = -1e30

SUBLANES = 8
LANES = 128
SSM_HALF_W = SSM_WIDTH // 2
SSM_HALF_STATES = SSM_GROUPS // 2 * SSM_STATE
HALF_TILE = SUBLANES // 2
PEER_V_GROUP = 2
VMEM_LIMIT_TABLE = 52 * 1024 * 1024


def _dot(a, b):
    return jnp.dot(a, b, preferred_element_type=F32)


def _dot_nt(a, b):
    return lax.dot_general(a, b, (((1,), (1,)), ((), ())), preferred_element_type=F32)


def _gelu(x):
    return 0.5 * x * (1.0 + lax.erf(x * (2.0 ** -0.5)))


def _in_proj_kernel(x_ref, g1_ref, w_ref, ones_ref, qkg_ref, qt_ref, k_ref, vt_ref, u_ref):
    x = x_ref[...]
    ms = jnp.mean(x * x, axis=-1, keepdims=True)
    y = (x * lax.rsqrt(ms + NORM_EPS) * g1_ref[...]).astype(BF16)
    proj = _dot(y, w_ref[...])
    qk = proj[:, :QK_WIDTH]
    ss = _dot((qk * qk).astype(BF16), ones_ref[...]) * (1.0 / HEAD_DIM)
    qkn = qk * lax.rsqrt(ss + NORM_EPS) * qkg_ref[...]
    qt_ref[...] = qkn[:, :ATTN_WIDTH].T.astype(BF16)
    k_ref[...] = qkn[:, ATTN_WIDTH:].astype(BF16)
    vt_ref[...] = proj[:, QK_WIDTH:QK_WIDTH + KV_WIDTH].T.astype(BF16)
    u_ref[...] = proj[:, QK_WIDTH + KV_WIDTH:]


def _in_proj(x, g1, w_in_bf, ones_bd, qkg, tm):
    rows = x.shape[0]
    row_spec = lambda w: pl.BlockSpec((tm, w), lambda i: (i, 0))
    col_spec = lambda w: pl.BlockSpec((w, tm), lambda i: (0, i))
    full = lambda a: pl.BlockSpec(a.shape, lambda i: (0,) * a.ndim)
    return pl.pallas_call(
        _in_proj_kernel,
        grid=(rows // tm,),
        in_specs=[row_spec(D_MODEL), full(g1), full(w_in_bf), full(ones_bd), full(qkg)],
        out_specs=[col_spec(ATTN_WIDTH), row_spec(KV_WIDTH), col_spec(KV_WIDTH), row_spec(SSM_WIDTH)],
        out_shape=[jax.ShapeDtypeStruct((ATTN_WIDTH, rows), BF16),
                   jax.ShapeDtypeStruct((rows, KV_WIDTH), BF16),
                   jax.ShapeDtypeStruct((KV_WIDTH, rows), BF16),
                   jax.ShapeDtypeStruct((rows, SSM_WIDTH), F32)],
        compiler_params=pltpu.CompilerParams(dimension_semantics=("arbitrary",)),
        name="in_proj",
    )(x, g1, w_in_bf, ones_bd, qkg)


def _attn_kernel(sink_ref, qt_ref, kp_ref, kc_ref, vtp_ref, vtc_ref, km_ref, vtm_ref, g_ref, o_ref, acc_ref):
    i = pl.program_id(0)
    group_w = GQA * BLOCK
    key = lax.broadcasted_iota(I32, (BLOCK, group_w), 0)
    qry = lax.broadcasted_iota(I32, (BLOCK, group_w), 1) & (BLOCK - 1)
    prev_ok = jnp.logical_and(key > qry, i > 0)
    cur_ok = key <= qry
    for kv in range(N_KV_HEADS):
        hs = slice(kv * HEAD_DIM, (kv + 1) * HEAD_DIM)
        heads = range(kv * GQA, (kv + 1) * GQA)
        qt = jnp.concatenate([qt_ref[h * HEAD_DIM:(h + 1) * HEAD_DIM, :] for h in heads], axis=1)
        sink = jnp.concatenate([jnp.full((1, BLOCK), sink_ref[h], F32) for h in heads], axis=1)
        sp = jnp.where(prev_ok, _dot(kp_ref[:, hs], qt), MASK_VALUE)
        sc = jnp.where(cur_ok, _dot(kc_ref[:, hs], qt), MASK_VALUE)
        sm = _dot(km_ref[:, hs], qt)
        m = jnp.maximum(jnp.maximum(jnp.max(sp, axis=0, keepdims=True), jnp.max(sc, axis=0, keepdims=True)),
                        jnp.maximum(jnp.max(sm, axis=0, keepdims=True), sink))
        pp, pc, pm = jnp.exp(sp - m), jnp.exp(sc - m), jnp.exp(sm - m)
        denom = (jnp.sum(pp, axis=0, keepdims=True) + jnp.sum(pc, axis=0, keepdims=True)
                 + jnp.sum(pm, axis=0, keepdims=True) + jnp.exp(sink - m))
        ot = (_dot(vtp_ref[hs, :], pp.astype(BF16)) + _dot(vtc_ref[hs, :], pc.astype(BF16))
              + _dot(vtm_ref[hs, :], pm.astype(BF16))) / denom
        for g, h in enumerate(heads):
            acc_ref[h * HEAD_DIM:(h + 1) * HEAD_DIM, :] = ot[:, g * BLOCK:(g + 1) * BLOCK]
    a = acc_ref[...]
    ms = jnp.mean(a * a, axis=0, keepdims=True)
    o_ref[...] = (a * lax.rsqrt(ms + NORM_EPS) * g_ref[...]).T.astype(BF16)


def _attention(sinks, qt, k, vt, k_meta, vt_meta, g_out):
    nb = k.shape[0] // BLOCK
    prev_i = lambda i: jnp.maximum(i - 1, 0)
    full = lambda a: pl.BlockSpec(a.shape, lambda i: (0,) * a.ndim)
    return pl.pallas_call(
        _attn_kernel,
        grid=(nb,),
        in_specs=[pl.BlockSpec(memory_space=pltpu.SMEM),
                  pl.BlockSpec((ATTN_WIDTH, BLOCK), lambda i: (0, i)),
                  pl.BlockSpec((BLOCK, KV_WIDTH), lambda i: (prev_i(i), 0)),
                  pl.BlockSpec((BLOCK, KV_WIDTH), lambda i: (i, 0)),
                  pl.BlockSpec((KV_WIDTH, BLOCK), lambda i: (0, prev_i(i))),
                  pl.BlockSpec((KV_WIDTH, BLOCK), lambda i: (0, i)),
                  full(k_meta), full(vt_meta), full(g_out)],
        out_specs=pl.BlockSpec((BLOCK, ATTN_WIDTH), lambda i: (i, 0)),
        out_shape=jax.ShapeDtypeStruct((k.shape[0], ATTN_WIDTH), BF16),
        scratch_shapes=[pltpu.VMEM((ATTN_WIDTH, BLOCK), F32)],
        compiler_params=pltpu.CompilerParams(dimension_semantics=("arbitrary",)),
        name="attention",
    )(sinks, qt, k, k, vt, vt, k_meta, vt_meta, g_out)


def _ssm_kernel(u_ref, bmat_ref, cmat_ref, coef_ref, carry_in_ref, d_ref, gw_ref, gb_ref, g_ref,
                o_ref, carry_out_ref, st_ref, carry_ref, y_ref, *, tm):
    @pl.when(pl.program_id(0) == 0)
    def _():
        carry_ref[...] = carry_in_ref[...]

    ns = SSM_HALF_STATES
    for h in range(2):
        uh = u_ref[:, h * SSM_HALF_W:(h + 1) * SSM_HALF_W].astype(BF16)
        st_ref[...] = _dot(uh, bmat_ref[h])

        def block(b, carry):
            cr, ci = carry
            r0 = pl.multiple_of(b * SUBLANES, SUBLANES)
            xr = st_ref[pl.ds(r0, SUBLANES), 0:ns]
            xi = st_ref[pl.ds(r0, SUBLANES), ns:2 * ns]
            for j, shift in enumerate((1, 2, 4)):
                ar, ai = coef_ref[h, j, 0], coef_ref[h, j, 1]
                sr, si = pltpu.roll(xr, shift, 0), pltpu.roll(xi, shift, 0)
                xr, xi = xr + ar * sr - ai * si, xi + ar * si + ai * sr
            pr, pi = coef_ref[h, 3, 0], coef_ref[h, 3, 1]
            xr, xi = xr + pr * cr - pi * ci, xi + pr * ci + pi * cr
            st_ref[pl.ds(r0, SUBLANES), 0:ns] = xr
            st_ref[pl.ds(r0, SUBLANES), ns:2 * ns] = xi
            return (jnp.broadcast_to(xr[SUBLANES - 1:SUBLANES, :], (SUBLANES, ns)),
                    jnp.broadcast_to(xi[SUBLANES - 1:SUBLANES, :], (SUBLANES, ns)))

        cr, ci = lax.fori_loop(0, tm // SUBLANES, block, (carry_ref[h, 0], carry_ref[h, 1]))
        carry_ref[h, 0] = cr
        carry_ref[h, 1] = ci
        y_ref[:, h * SSM_HALF_W:(h + 1) * SSM_HALF_W] = _dot(st_ref[...].astype(BF16), cmat_ref[h])

    carry_out_ref[...] = carry_ref[...]
    z = _gelu(y_ref[...] + d_ref[...] * u_ref[...])
    z = z * jax.nn.sigmoid(_dot(z.astype(BF16), gw_ref[...]) + gb_ref[...])
    ms = jnp.mean(z * z, axis=-1, keepdims=True)
    o_ref[...] = (z * lax.rsqrt(ms + NORM_EPS) * g_ref[...]).astype(BF16)


def _ssm(u, bmat, cmat, coef, carry_in, d, gw_bf, gb, g_out, tm):
    rows = u.shape[0]
    row_spec = pl.BlockSpec((tm, SSM_WIDTH), lambda i: (i, 0))
    full = lambda a: pl.BlockSpec(a.shape, lambda i: (0,) * a.ndim)
    return pl.pallas_call(
        functools.partial(_ssm_kernel, tm=tm),
        grid=(rows // tm,),
        in_specs=[row_spec, full(bmat), full(cmat), full(coef), full(carry_in), full(d), full(gw_bf), full(gb),
                  full(g_out)],
        out_specs=[row_spec, full(carry_in)],
        out_shape=[jax.ShapeDtypeStruct((rows, SSM_WIDTH), BF16),
                   jax.ShapeDtypeStruct(carry_in.shape, F32)],
        scratch_shapes=[pltpu.VMEM((tm, 2 * SSM_HALF_STATES), F32),
                        pltpu.VMEM(carry_in.shape, F32),
                        pltpu.VMEM((tm, SSM_WIDTH), F32)],
        compiler_params=pltpu.CompilerParams(dimension_semantics=("arbitrary",)),
        name="ssm",
    )(u, bmat, cmat, coef, carry_in, d, gw_bf, gb, g_out)


def _ssm_params(a_re, a_im, log_dt, b_re, b_im, c_re, c_im):
    dt = jnp.exp(log_dt)[:, None]
    mag = jnp.exp(a_re * dt)
    abar_r, abar_i = mag * jnp.cos(a_im * dt), mag * jnp.sin(a_im * dt)
    den = a_re * a_re + a_im * a_im
    nr, ni = abar_r - 1.0, abar_i
    coef_r = (nr * a_re + ni * a_im) / den
    coef_i = (ni * a_re - nr * a_im) / den
    bbar_r = coef_r[..., None] * b_re - coef_i[..., None] * b_im
    bbar_i = coef_r[..., None] * b_im + coef_i[..., None] * b_re
    gh = SSM_GROUPS // 2
    eye = jnp.eye(gh, dtype=F32)

    def half_b(bb):
        return jnp.einsum('gnp,gk->gpkn', bb, eye).reshape(gh * SSM_CH, gh * SSM_STATE)

    def half_c(cc):
        return jnp.einsum('gpn,gk->gnkp', cc, eye).reshape(gh * SSM_STATE, gh * SSM_CH)

    bmat = jnp.stack([jnp.concatenate([half_b(bbar_r[s]), half_b(bbar_i[s])], axis=1)
                      for s in (slice(0, gh), slice(gh, None))]).astype(BF16)
    cmat = jnp.stack([jnp.concatenate([half_c(c_re[s]), -half_c(c_im[s])], axis=0)
                      for s in (slice(0, gh), slice(gh, None))]).astype(BF16)

    def cpow(k):
        m = mag ** k
        return m * jnp.cos(a_im * dt * k), m * jnp.sin(a_im * dt * k)

    rows = jnp.arange(SUBLANES)
    coefs = []
    for shift in (1, 2, 4):
        pr, pi = cpow(float(shift))
        keep = (rows >= shift).astype(F32)[:, None, None]
        coefs.append(jnp.stack([keep * pr[None], keep * pi[None]]))
    pw = (rows + 1).astype(F32)[:, None, None]
    m = mag[None] ** pw
    coefs.append(jnp.stack([m * jnp.cos(a_im[None] * dt[None] * pw), m * jnp.sin(a_im[None] * dt[None] * pw)]))
    coef = jnp.stack(coefs)
    coef = coef.reshape(4, 2, SUBLANES, 2, SSM_HALF_STATES).transpose(3, 0, 1, 2, 4)
    return bmat, cmat, coef


def _top16(s, payload, n_rows):
    rowid = lax.broadcasted_iota(I32, s.shape, 0).astype(F32)
    vals, pays = [], []
    for _ in range(PEER_TOPK):
        m = jnp.max(s, axis=0, keepdims=True)
        pos = jnp.min(jnp.where(s == m, rowid, float(n_rows)), axis=0, keepdims=True)
        sel = rowid == pos
        vals.append(m)
        pays.append(pos if payload is None else jnp.max(jnp.where(sel, payload, -1.0), axis=0, keepdims=True))
        s = jnp.where(sel, -jnp.inf, s)
    return jnp.concatenate(vals, axis=0), jnp.concatenate(pays, axis=0)


_CAND_PAIRS = [(a, b) for a in range(PEER_TOPK) for b in range(PEER_TOPK) if (a + 1) * (b + 1) <= PEER_TOPK]
_CAND_ROWS = -(-len(_CAND_PAIRS) // SUBLANES) * SUBLANES


def _mix_kernel(x_ref, a_ref, s_ref, wo_ref, g2_ref, wq_ref, keys_ref, h1_ref, hn_ref, idx_ref, gate_ref,
                qp_ref, idx_t_ref, gate_t_ref, *, tm):
    mix = _dot(a_ref[...], wo_ref[0:ATTN_WIDTH, :]) + _dot(s_ref[...], wo_ref[ATTN_WIDTH:, :])
    h1 = x_ref[...] + mix
    h1_ref[...] = h1.reshape(tm, SUBLANES, LANES)
    ms = jnp.mean(h1 * h1, axis=-1, keepdims=True)
    hn = h1 * lax.rsqrt(ms + NORM_EPS) * g2_ref[...]
    hn_ref[...] = hn.reshape(tm, SUBLANES, LANES)
    qp_ref[...] = _dot(hn.astype(BF16), wq_ref[...])

    def retrieve(h, tok):
        c0 = pl.multiple_of(h * PEER_DK, PEER_DK)
        q1 = qp_ref[tok, pl.ds(c0, PEER_HALF)].astype(BF16)
        q2 = qp_ref[tok, pl.ds(c0 + PEER_HALF, PEER_HALF)].astype(BF16)
        s1 = _dot_nt(keys_ref[h, 0], q1)
        s2 = _dot_nt(keys_ref[h, 1], q2)
        v1, i1 = _top16(s1, None, PEER_N_KEYS)
        v2, i2 = _top16(s2, None, PEER_N_KEYS)
        pad = _CAND_ROWS - len(_CAND_PAIRS)
        cand = jnp.concatenate([v1[a:a + 1, :] + v2[b:b + 1, :] for a, b in _CAND_PAIRS]
                               + [jnp.full((pad, LANES), -jnp.inf, F32)], axis=0)
        eidx = jnp.concatenate([i1[a:a + 1, :] * float(PEER_N_KEYS) + i2[b:b + 1, :] for a, b in _CAND_PAIRS]
                               + [jnp.zeros((pad, LANES), F32)], axis=0)
        sc, ei = _top16(cand, eidx, _CAND_ROWS)
        p = jnp.exp(sc - sc[0:1, :])
        r0 = pl.multiple_of(h * PEER_TOPK, PEER_TOPK)
        idx_t_ref[pl.ds(r0, PEER_TOPK), tok] = ei.astype(I32)
        gate_t_ref[pl.ds(r0, PEER_TOPK), tok] = p / jnp.sum(p, axis=0, keepdims=True)

    def head_pair(i, _):
        for h in (2 * i, 2 * i + 1):
            for sub in range(tm // LANES):
                retrieve(h, slice(sub * LANES, (sub + 1) * LANES))
        return 0

    lax.fori_loop(0, PEER_HEADS // 2, head_pair, 0)
    idx_ref[...] = idx_t_ref[...].T
    gate_ref[...] = gate_t_ref[...].T


def _mix(x, attn_n, ssm_n, w_out_bf, g2, wq_bf, keys_bf, tm):
    rows = x.shape[0]
    row_spec = lambda w: pl.BlockSpec((tm, w), lambda i: (i, 0))
    tile_spec = pl.BlockSpec((tm, SUBLANES, LANES), lambda i: (i, 0, 0))
    full = lambda a: pl.BlockSpec(a.shape, lambda i: (0,) * a.ndim)
    return pl.pallas_call(
        functools.partial(_mix_kernel, tm=tm),
        grid=(rows // tm,),
        in_specs=[row_spec(D_MODEL), row_spec(ATTN_WIDTH), row_spec(SSM_WIDTH), full(w_out_bf), full(g2),
                  full(wq_bf), full(keys_bf)],
        out_specs=[tile_spec, tile_spec, row_spec(PEER_SEL), row_spec(PEER_SEL)],
        out_shape=[jax.ShapeDtypeStruct((rows, SUBLANES, LANES), F32),
                   jax.ShapeDtypeStruct((rows, SUBLANES, LANES), F32),
                   jax.ShapeDtypeStruct((rows, PEER_SEL), I32),
                   jax.ShapeDtypeStruct((rows, PEER_SEL), F32)],
        scratch_shapes=[pltpu.VMEM((tm, PEER_HEADS * PEER_DK), F32),
                        pltpu.VMEM((PEER_SEL, tm), I32),
                        pltpu.VMEM((PEER_SEL, tm), F32)],
        compiler_params=pltpu.CompilerParams(dimension_semantics=("arbitrary",),
                                             vmem_limit_bytes=48 * 1024 * 1024),
        name="mix_topk",
    )(x, attn_n, ssm_n, w_out_bf, g2, wq_bf, keys_bf)


def _expert_tiles(t):
    return t.astype(BF16).reshape(t.shape[0], SUBLANES, LANES)


def _load_table(tab_hbm, tab_ref, sem):
    @pl.when(pl.program_id(0) == 0)
    def _():
        cp = pltpu.make_async_copy(tab_hbm, tab_ref, sem)
        cp.start()
        cp.wait()


def _peer_u_kernel(idx_ref, h_ref, gate_ref, tab_hbm, act_ref, tab_ref, part_a_ref, part_b_ref, sem, *, tt):
    _load_table(tab_hbm, tab_ref, sem)

    def products(t0, part_refs):
        toks = [(idx_ref.at[t0 + j], h_ref[t0 + j], part_refs[j]) for j in range(len(part_refs))]
        for k in range(PEER_SEL):
            for row, hv, part_ref in toks:
                p = tab_ref[row[k]].astype(F32) * hv
                part_ref[k * HALF_TILE:(k + 1) * HALF_TILE, :] = p[0:HALF_TILE] + p[HALF_TILE:]

    def row_sums(part_ref):
        part = part_ref[pl.ds(0, PEER_SEL, stride=HALF_TILE), :]
        for r in range(1, HALF_TILE):
            part = part + part_ref[pl.ds(r, PEER_SEL, stride=HALF_TILE), :]
        return part

    def finish(t, part):
        dots = jnp.sum(part.T, axis=0, keepdims=True)
        act_ref[t] = _gelu(dots) * gate_ref[t]

    products(0, (part_a_ref, part_b_ref))

    def pair(i, _):
        t = 2 * i
        sums_a, sums_b = row_sums(part_a_ref), row_sums(part_b_ref)
        finish(t - 2, sums_a)
        finish(t - 1, sums_b)
        products(t, (part_a_ref, part_b_ref))
        return 0

    lax.fori_loop(1, tt // 2, pair, 0)
    finish(tt - 2, row_sums(part_a_ref))
    finish(tt - 1, row_sums(part_b_ref))


def _peer_u(idx, hn3, gate3, tab, tt, first):
    rows = idx.shape[0]
    b0 = first // tt
    return pl.pallas_call(
        functools.partial(_peer_u_kernel, tt=tt),
        grid=((rows - first) // tt,),
        in_specs=[pl.BlockSpec((tt, PEER_SEL), lambda i: (i + b0, 0), memory_space=pltpu.SMEM),
                  pl.BlockSpec((tt, SUBLANES, LANES), lambda i: (i + b0, 0, 0)),
                  pl.BlockSpec((tt, 1, PEER_SEL), lambda i: (i + b0, 0, 0)),
                  pl.BlockSpec(memory_space=pl.ANY)],
        out_specs=pl.BlockSpec((tt, 1, PEER_SEL), lambda i: (i, 0, 0)),
        out_shape=jax.ShapeDtypeStruct((rows - first, 1, PEER_SEL), F32),
        scratch_shapes=[pltpu.VMEM(tab.shape, BF16),
                        pltpu.VMEM((PEER_SEL * HALF_TILE, LANES), F32),
                        pltpu.VMEM((PEER_SEL * HALF_TILE, LANES), F32),
                        pltpu.SemaphoreType.DMA],
        compiler_params=pltpu.CompilerParams(dimension_semantics=("arbitrary",),
                                             vmem_limit_bytes=VMEM_LIMIT_TABLE),
        name="peer_u",
    )(idx, hn3, gate3, tab)


def _peer_v_kernel(idx_ref, act_ref, h1_ref, tab_hbm, o_ref, tab_ref, tiles_ref, *rest, tt):
    rep_refs, sem = rest[:-1], rest[-1]
    _load_table(tab_hbm, tab_ref, sem)
    n_acc = 2
    g = PEER_V_GROUP

    def replicated(t):
        return jnp.broadcast_to(act_ref[t], (PEER_SEL, LANES)).T

    def accumulate(t0):
        rows = [idx_ref.at[t0 + j] for j in range(g)]
        accs = [[jnp.zeros((SUBLANES, LANES), F32)] * n_acc for _ in range(g)]
        for k in range(PEER_SEL):
            for j, rep_ref in enumerate(rep_refs):
                w = jnp.broadcast_to(rep_ref[k:k + 1, :], (SUBLANES, LANES))
                accs[j][k % n_acc] = accs[j][k % n_acc] + w * tab_ref[rows[j][k]].astype(F32)
        for j in range(g):
            tiles_ref[t0 + j] = h1_ref[t0 + j] + sum(accs[j][1:], accs[j][0])

    for j, rep_ref in enumerate(rep_refs):
        rep_ref[...] = replicated(j)

    def group(i, _):
        t = g * i
        nxt = [replicated(jnp.minimum(t + g + j, tt - 1)) for j in range(g)]
        accumulate(t)
        for rep_ref, tile in zip(rep_refs, nxt):
            rep_ref[...] = tile
        return 0

    lax.fori_loop(0, tt // g, group, 0)
    o_ref[...] = tiles_ref[...].reshape(tt, D_MODEL)


def _peer_v(idx, act3, h13, tab, tt, first):
    rows = idx.shape[0]
    b0 = first // tt
    tile = pl.BlockSpec((tt, SUBLANES, LANES), lambda i: (i + b0, 0, 0))
    return pl.pallas_call(
        functools.partial(_peer_v_kernel, tt=tt),
        grid=((rows - first) // tt,),
        in_specs=[pl.BlockSpec((tt, PEER_SEL), lambda i: (i + b0, 0), memory_space=pltpu.SMEM),
                  pl.BlockSpec((tt, 1, PEER_SEL), lambda i: (i + b0, 0, 0)),
                  tile, pl.BlockSpec(memory_space=pl.ANY)],
        out_specs=pl.BlockSpec((tt, D_MODEL), lambda i: (i, 0)),
        out_shape=jax.ShapeDtypeStruct((rows - first, D_MODEL), F32),
        scratch_shapes=[pltpu.VMEM(tab.shape, BF16), pltpu.VMEM((tt, SUBLANES, LANES), F32)]
                       + [pltpu.VMEM((PEER_SEL, LANES), F32)] * PEER_V_GROUP
                       + [pltpu.SemaphoreType.DMA],
        compiler_params=pltpu.CompilerParams(dimension_semantics=("arbitrary",),
                                             vmem_limit_bytes=VMEM_LIMIT_TABLE),
        name="peer_v",
    )(idx, act3, h13, tab)


SC_LANES = 16
SC_WORKERS = 32
SC_ROWS = 32
SC_COLS = 128
SC_ROW_GROUP = 4
SC_SHARE_U = (1, 4)
SC_SHARE_V = (5, 16)


def _peer_v_sc_kernel(tab_hbm, idx_hbm, act_hbm, h1_hbm, out_hbm, idx_v, act_v, rows0_v, rows1_v, acc_v, sem0, sem1,
                      *, n_tok):
    wid = lax.axis_index("s") * 2 + lax.axis_index("c")
    bufs = ((rows0_v, sem0), (rows1_v, sem1))
    n_chunks = PEER_SEL // SC_ROWS

    def gather(c):
        rows_v, sem = bufs[c % 2]
        return pltpu.make_async_copy(tab_hbm.at[idx_v.at[pl.ds(c * SC_ROWS, SC_ROWS)]], rows_v, sem)

    @pl.loop(0, n_tok)
    def _(i):
        t = wid * n_tok + i
        pltpu.sync_copy(idx_hbm.at[pl.ds(t * PEER_SEL, PEER_SEL)], idx_v)
        gather(0).start()
        pltpu.sync_copy(act_hbm.at[pl.ds(t * PEER_SEL, PEER_SEL)], act_v)
        pltpu.sync_copy(h1_hbm.at[t], acc_v)
        for c in range(n_chunks):
            if c + 1 < n_chunks:
                gather(c + 1).start()
            gather(c).wait()
            rows_v = bufs[c % 2][0]

            @pl.loop(0, D_MODEL // SC_COLS)
            def _(cb):
                cols = [pl.ds(cb * SC_COLS + j * SC_LANES, SC_LANES) for j in range(SC_COLS // SC_LANES)]

                @pl.loop(0, SC_ROWS // SC_ROW_GROUP)
                def _(rg):
                    accs = [acc_v[col] for col in cols]
                    for rr in range(SC_ROW_GROUP):
                        r = rg * SC_ROW_GROUP + rr
                        w = plsc.load_gather(act_v, [jnp.full((SC_LANES,), c * SC_ROWS, I32) + r])
                        accs = [a + w * rows_v[r, col] for a, col in zip(accs, cols)]
                    for a, col in zip(accs, cols):
                        acc_v[col] = a
        pltpu.sync_copy(acc_v, out_hbm.at[t])


def _peer_v_sc(tab, idx_flat, act_flat, h1_rows):
    n = h1_rows.shape[0]
    mesh = plsc.VectorSubcoreMesh(core_axis_name="c", subcore_axis_name="s")
    return pl.kernel(
        functools.partial(_peer_v_sc_kernel, n_tok=n // SC_WORKERS),
        out_type=jax.ShapeDtypeStruct((n, D_MODEL), F32),
        mesh=mesh,
        scratch_types=[pltpu.VMEM((PEER_SEL,), I32), pltpu.VMEM((PEER_SEL,), F32),
                       pltpu.VMEM((SC_ROWS, D_MODEL), F32), pltpu.VMEM((SC_ROWS, D_MODEL), F32),
                       pltpu.VMEM((D_MODEL,), F32), pltpu.SemaphoreType.DMA, pltpu.SemaphoreType.DMA],
        compiler_params=pltpu.CompilerParams(needs_layout_passes=False),
        name="peer_v_sc",
    )(tab, idx_flat, act_flat, h1_rows)


def _peer_u_sc_kernel(tab_hbm, idx_hbm, hn_hbm, part_hbm, idx_v, h_v, rows0_v, rows1_v, part_v, sem0, sem1, *, n_tok):
    wid = lax.axis_index("s") * 2 + lax.axis_index("c")
    bufs = ((rows0_v, sem0), (rows1_v, sem1))
    n_chunks = PEER_SEL // SC_ROWS
    part_w = PEER_SEL * SC_LANES

    def gather(c):
        rows_v, sem = bufs[c % 2]
        return pltpu.make_async_copy(tab_hbm.at[idx_v.at[pl.ds(c * SC_ROWS, SC_ROWS)]], rows_v, sem)

    @pl.loop(0, n_tok)
    def _(i):
        t = wid * n_tok + i
        pltpu.sync_copy(idx_hbm.at[pl.ds(t * PEER_SEL, PEER_SEL)], idx_v)
        gather(0).start()
        pltpu.sync_copy(hn_hbm.at[t], h_v)

        @pl.loop(0, PEER_SEL)
        def _(k):
            part_v[pl.ds(k * SC_LANES, SC_LANES)] = jnp.zeros((SC_LANES,), F32)

        for c in range(n_chunks):
            if c + 1 < n_chunks:
                gather(c + 1).start()
            gather(c).wait()
            rows_v = bufs[c % 2][0]

            @pl.loop(0, D_MODEL // SC_COLS)
            def _(cb):
                cols = [pl.ds(cb * SC_COLS + j * SC_LANES, SC_LANES) for j in range(SC_COLS // SC_LANES)]

                @pl.loop(0, SC_ROWS // SC_ROW_GROUP)
                def _(rg):
                    hs = [h_v[col] for col in cols]
                    for rr in range(SC_ROW_GROUP):
                        r = rg * SC_ROW_GROUP + rr
                        slot = pl.ds((c * SC_ROWS + r) * SC_LANES, SC_LANES)
                        prods = [h * rows_v[r, col] for h, col in zip(hs, cols)]
                        while len(prods) > 1:
                            prods = [a + b for a, b in zip(prods[0::2], prods[1::2])]
                        part_v[slot] = part_v[slot] + prods[0]
        pltpu.sync_copy(part_v, part_hbm.at[pl.ds(t * part_w, part_w)])


def _peer_u_sc(tab, idx_flat, hn_rows):
    n = hn_rows.shape[0]
    mesh = plsc.VectorSubcoreMesh(core_axis_name="c", subcore_axis_name="s")
    return pl.kernel(
        functools.partial(_peer_u_sc_kernel, n_tok=n // SC_WORKERS),
        out_type=jax.ShapeDtypeStruct((n * PEER_SEL * SC_LANES,), F32),
        mesh=mesh,
        scratch_types=[pltpu.VMEM((PEER_SEL,), I32), pltpu.VMEM((D_MODEL,), F32),
                       pltpu.VMEM((SC_ROWS, D_MODEL), F32), pltpu.VMEM((SC_ROWS, D_MODEL), F32),
                       pltpu.VMEM((PEER_SEL * SC_LANES,), F32), pltpu.SemaphoreType.DMA, pltpu.SemaphoreType.DMA],
        compiler_params=pltpu.CompilerParams(needs_layout_passes=False),
        name="peer_u_sc",
    )(tab, idx_flat, hn_rows)


def _fold_act_kernel(part_ref, fold_ref, gate_ref, act_ref):
    p = part_ref[...]
    hi = p.astype(BF16)
    lo = (p - hi.astype(F32)).astype(BF16)
    dots = _dot(hi, fold_ref[...]) + _dot(lo, fold_ref[...])
    act_ref[...] = _gelu(dots) * gate_ref[...]


def _fold_act(part2, gate, tm):
    n = part2.shape[0]
    fold = (jnp.arange(PEER_SEL * SC_LANES)[:, None] // SC_LANES == jnp.arange(PEER_SEL)[None, :]).astype(BF16)
    return pl.pallas_call(
        _fold_act_kernel,
        grid=(n // tm,),
        in_specs=[pl.BlockSpec((tm, PEER_SEL * SC_LANES), lambda i: (i, 0)),
                  pl.BlockSpec(fold.shape, lambda i: (0, 0)),
                  pl.BlockSpec((tm, PEER_SEL), lambda i: (i, 0))],
        out_specs=pl.BlockSpec((tm, PEER_SEL), lambda i: (i, 0)),
        out_shape=jax.ShapeDtypeStruct((n, PEER_SEL), F32),
        compiler_params=pltpu.CompilerParams(dimension_semantics=("arbitrary",)),
        name="fold_act",
    )(part2, fold, gate)


def _block(x2, meta_tokens, norm1_g, w_in, q_norm_g, k_norm_g, attn_sinks, ssm_a_re, ssm_a_im, ssm_log_dt,
           ssm_b_re, ssm_b_im, ssm_c_re, ssm_c_im, ssm_d, ssm_glu_w, ssm_glu_b, attn_out_g, ssm_out_g, w_out,
           norm2_g, peer_w_query, peer_sub_keys, peer_u, peer_v, *, tm_proj, tm_ssm, tm_mix, tt_peer):
    rows = x2.shape[0]
    row2 = lambda a: a.reshape(1, -1).astype(F32)
    w_in_bf = w_in.astype(BF16)
    seg = jnp.arange(QK_WIDTH) // HEAD_DIM
    ones_bd = (seg[:, None] == seg[None, :]).astype(BF16)
    scale = HEAD_DIM ** -0.5
    qkg = jnp.concatenate([jnp.tile(q_norm_g.astype(F32), N_Q_HEADS) * scale,
                           jnp.tile(k_norm_g.astype(F32), N_KV_HEADS)]).reshape(1, QK_WIDTH)
    bmat, cmat, coef = _ssm_params(ssm_a_re.astype(F32), ssm_a_im.astype(F32), ssm_log_dt.astype(F32),
                                   ssm_b_re.astype(F32), ssm_b_im.astype(F32), ssm_c_re.astype(F32),
                                   ssm_c_im.astype(F32))
    glu_w_bf = ssm_glu_w.astype(BF16)
    ssm_args = (row2(ssm_d), glu_w_bf, row2(ssm_glu_b), row2(ssm_out_g))

    h0 = jnp.concatenate([jnp.zeros((BLOCK - N_META, D_MODEL), F32), meta_tokens.astype(F32)], axis=0)
    _, k0, vt0, u0 = _in_proj(h0, row2(norm1_g), w_in_bf, ones_bd, qkg, BLOCK)
    carry0 = jnp.zeros((2, 2, SUBLANES, SSM_HALF_STATES), F32)
    _, carry = _ssm(u0, bmat, cmat, coef, carry0, *ssm_args, BLOCK)

    qt, k, vt, u = _in_proj(x2, row2(norm1_g), w_in_bf, ones_bd, qkg, tm_proj)
    g_attn = jnp.broadcast_to(attn_out_g.astype(F32)[:, None], (ATTN_WIDTH, BLOCK))
    attn_n = _attention(attn_sinks.astype(F32), qt, k, vt, k0[BLOCK - N_META:], vt0[:, BLOCK - N_META:], g_attn)
    ssm_n, _ = _ssm(u, bmat, cmat, coef, carry, *ssm_args, tm_ssm)

    h1, hn, idx, gate = _mix(x2, attn_n, ssm_n, w_out.astype(BF16), row2(norm2_g), peer_w_query.astype(BF16),
                             peer_sub_keys.astype(BF16), tm_mix)
    share = lambda num, den: rows * num // den // (SC_WORKERS * tt_peer) * (SC_WORKERS * tt_peer)
    n_u, n_v = share(*SC_SHARE_U), share(*SC_SHARE_V)
    part = _peer_u_sc(peer_u.astype(F32), idx[:n_u].reshape(-1), hn[:n_u].reshape(n_u, D_MODEL))
    act_tc = _peer_u(idx, hn, gate.reshape(rows, 1, PEER_SEL), _expert_tiles(peer_u), tt_peer, n_u)
    act_sc = _fold_act(part.reshape(n_u, PEER_SEL * SC_LANES), gate[:n_u], tm_mix)
    act = jnp.concatenate([act_sc.reshape(n_u, 1, PEER_SEL), act_tc], axis=0)
    sc_out = _peer_v_sc(peer_v.astype(F32), idx[:n_v].reshape(-1), act[:n_v].reshape(-1),
                        h1[:n_v].reshape(n_v, D_MODEL))
    tc_out = _peer_v(idx, act, h1, _expert_tiles(peer_v), tt_peer, n_v)
    return jnp.concatenate([sc_out, tc_out], axis=0)


def kernel(x, meta_tokens, norm1_g, w_in, q_norm_g, k_norm_g, attn_sinks, ssm_a_re, ssm_a_im, ssm_log_dt, ssm_b_re, ssm_b_im, ssm_c_re, ssm_c_im, ssm_d, ssm_glu_w, ssm_glu_b, attn_out_g, ssm_out_g, w_out, norm2_g, peer_w_query, peer_sub_keys, peer_u, peer_v):
    b, seq, d = x.shape
    outs = []
    for bi in range(b):
        outs.append(_block(x[bi].astype(F32), meta_tokens, norm1_g[0], w_in[0], q_norm_g[0], k_norm_g[0],
                           attn_sinks[0], ssm_a_re[0], ssm_a_im[0], ssm_log_dt[0], ssm_b_re[0], ssm_b_im[0],
                           ssm_c_re[0], ssm_c_im[0], ssm_d[0], ssm_glu_w[0], ssm_glu_b[0], attn_out_g[0],
                           ssm_out_g[0], w_out[0], norm2_g[0], peer_w_query[0], peer_sub_keys[0], peer_u[0],
                           peer_v[0], tm_proj=512, tm_ssm=256, tm_mix=256, tt_peer=32))
    return jnp.stack(outs).astype(x.dtype)
```

```python
import functools
import math

import jax
import jax.numpy as jnp
from jax import lax
from jax.experimental import pallas as pl
from jax.experimental.pallas import tpu as pltpu
from jax.experimental.pallas import tpu_sc as plsc

F32 = jnp.float32
BF16 = jnp.bfloat16
I32 = jnp.int32

D_MODEL = 1024
N_META = 16
BLOCK = 128
ATTN_WIDTH = 512
HEAD_DIM = 64
N_Q_HEADS = 8
N_KV_HEADS = 2
GQA = 4
KV_WIDTH = 128
QK_WIDTH = ATTN_WIDTH + KV_WIDTH
SSM_WIDTH = 512
SSM_CH = 16
SSM_GROUPS = 32
SSM_STATE = 64
IN_WIDTH = ATTN_WIDTH + 2 * KV_WIDTH + SSM_WIDTH
PEER_HEADS = 8
PEER_N_KEYS = 128
PEER_TOPK = 16
PEER_DK = 256
PEER_HALF = 128
PEER_SEL = PEER_HEADS * PEER_TOPK
NORM_EPS = 1e-6
MASK_VALUE = -1e30

SUBLANES = 8
LANES = 128
SSM_HALF_W = SSM_WIDTH // 2
SSM_HALF_STATES = SSM_GROUPS // 2 * SSM_STATE
HALF_TILE = SUBLANES // 2
PEER_V_GROUP = 2
VMEM_LIMIT_TABLE = 52 * 1024 * 1024


def _dot(a, b):
    return jnp.dot(a, b, preferred_element_type=F32)


def _dot_nt(a, b):
    return lax.dot_general(a, b, (((1,), (1,)), ((), ())), preferred_element_type=F32)


def _gelu(x):
    return 0.5 * x * (1.0 + lax.erf(x * (2.0 ** -0.5)))


def _in_proj_kernel(x_ref, g1_ref, w_ref, ones_ref, qkg_ref, qt_ref, k_ref, vt_ref, u_ref):
    x = x_ref[...]
    ms = jnp.mean(x * x, axis=-1, keepdims=True)
    y = (x * lax.rsqrt(ms + NORM_EPS) * g1_ref[...]).astype(BF16)
    proj = _dot(y, w_ref[...])
    qk = proj[:, :QK_WIDTH]
    ss = _dot((qk * qk).astype(BF16), ones_ref[...]) * (1.0 / HEAD_DIM)
    qkn = qk * lax.rsqrt(ss + NORM_EPS) * qkg_ref[...]
    qt_ref[...] = qkn[:, :ATTN_WIDTH].T.astype(BF16)
    k_ref[...] = qkn[:, ATTN_WIDTH:].astype(BF16)
    vt_ref[...] = proj[:, QK_WIDTH:QK_WIDTH + KV_WIDTH].T.astype(BF16)
    u_ref[...] = proj[:, QK_WIDTH + KV_WIDTH:]


def _in_proj(x, g1, w_in_bf, ones_bd, qkg, tm):
    rows = x.shape[0]
    row_spec = lambda w: pl.BlockSpec((tm, w), lambda i: (i, 0))
    col_spec = lambda w: pl.BlockSpec((w, tm), lambda i: (0, i))
    full = lambda a: pl.BlockSpec(a.shape, lambda i: (0,) * a.ndim)
    return pl.pallas_call(
        _in_proj_kernel,
        grid=(rows // tm,),
        in_specs=[row_spec(D_MODEL), full(g1), full(w_in_bf), full(ones_bd), full(qkg)],
        out_specs=[col_spec(ATTN_WIDTH), row_spec(KV_WIDTH), col_spec(KV_WIDTH), row_spec(SSM_WIDTH)],
        out_shape=[jax.ShapeDtypeStruct((ATTN_WIDTH, rows), BF16),
                   jax.ShapeDtypeStruct((rows, KV_WIDTH), BF16),
                   jax.ShapeDtypeStruct((KV_WIDTH, rows), BF16),
                   jax.ShapeDtypeStruct((rows, SSM_WIDTH), F32)],
        compiler_params=pltpu.CompilerParams(dimension_semantics=("arbitrary",)),
        name="in_proj",
    )(x, g1, w_in_bf, ones_bd, qkg)


def _attn_kernel(sink_ref, qt_ref, kp_ref, kc_ref, vtp_ref, vtc_ref, km_ref, vtm_ref, g_ref, o_ref, acc_ref):
    i = pl.program_id(0)
    group_w = GQA * BLOCK
    key = lax.broadcasted_iota(I32, (BLOCK, group_w), 0)
    qry = lax.broadcasted_iota(I32, (BLOCK, group_w), 1) & (BLOCK - 1)
    prev_ok = jnp.logical_and(key > qry, i > 0)
    cur_ok = key <= qry
    for kv in range(N_KV_HEADS):
        hs = slice(kv * HEAD_DIM, (kv + 1) * HEAD_DIM)
        heads = range(kv * GQA, (kv + 1) * GQA)
        qt = jnp.concatenate([qt_ref[h * HEAD_DIM:(h + 1) * HEAD_DIM, :] for h in heads], axis=1)
        sink = jnp.concatenate([jnp.full((1, BLOCK), sink_ref[h], F32) for h in heads], axis=1)
        sp = jnp.where(prev_ok, _dot(kp_ref[:, hs], qt), MASK_VALUE)
        sc = jnp.where(cur_ok, _dot(kc_ref[:, hs], qt), MASK_VALUE)
        sm = _dot(km_ref[:, hs], qt)
        m = jnp.maximum(jnp.maximum(jnp.max(sp, axis=0, keepdims=True), jnp.max(sc, axis=0, keepdims=True)),
                        jnp.maximum(jnp.max(sm, axis=0, keepdims=True), sink))
        pp, pc, pm = jnp.exp(sp - m), jnp.exp(sc - m), jnp.exp(sm - m)
        denom = (jnp.sum(pp, axis=0, keepdims=True) + jnp.sum(pc, axis=0, keepdims=True)
                 + jnp.sum(pm, axis=0, keepdims=True) + jnp.exp(sink - m))
        ot = (_dot(vtp_ref[hs, :], pp.astype(BF16)) + _dot(vtc_ref[hs, :], pc.astype(BF16))
              + _dot(vtm_ref[hs, :], pm.astype(BF16))) / denom
        for g, h in enumerate(heads):
            acc_ref[h * HEAD_DIM:(h + 1) * HEAD_DIM, :] = ot[:, g * BLOCK:(g + 1) * BLOCK]
    a = acc_ref[...]
    ms = jnp.mean(a * a, axis=0, keepdims=True)
    o_ref[...] = (a * lax.rsqrt(ms + NORM_EPS) * g_ref[...]).T.astype(BF16)


def _attention(sinks, qt, k, vt, k_meta, vt_meta, g_out):
    nb = k.shape[0] // BLOCK
    prev_i = lambda i: jnp.maximum(i - 1, 0)
    full = lambda a: pl.BlockSpec(a.shape, lambda i: (0,) * a.ndim)
    return pl.pallas_call(
        _attn_kernel,
        grid=(nb,),
        in_specs=[pl.BlockSpec(memory_space=pltpu.SMEM),
                  pl.BlockSpec((ATTN_WIDTH, BLOCK), lambda i: (0, i)),
                  pl.BlockSpec((BLOCK, KV_WIDTH), lambda i: (prev_i(i), 0)),
                  pl.BlockSpec((BLOCK, KV_WIDTH), lambda i: (i, 0)),
                  pl.BlockSpec((KV_WIDTH, BLOCK), lambda i: (0, prev_i(i))),
                  pl.BlockSpec((KV_WIDTH, BLOCK), lambda i: (0, i)),
                  full(k_meta), full(vt_meta), full(g_out)],
        out_specs=pl.BlockSpec((BLOCK, ATTN_WIDTH), lambda i: (i, 0)),
        out_shape=jax.ShapeDtypeStruct((k.shape[0], ATTN_WIDTH), BF16),
        scratch_shapes=[pltpu.VMEM((ATTN_WIDTH, BLOCK), F32)],
        compiler_params=pltpu.CompilerParams(dimension_semantics=("arbitrary",)),
        name="attention",
    )(sinks, qt, k, k, vt, vt, k_meta, vt_meta, g_out)


def _ssm_kernel(u_ref, bmat_ref, cmat_ref, coef_ref, carry_in_ref, d_ref, gw_ref, gb_ref, g_ref,
                o_ref, carry_out_ref, st_ref, carry_ref, y_ref, *, tm):
    @pl.when(pl.program_id(0) == 0)
    def _():
        carry_ref[...] = carry_in_ref[...]

    ns = SSM_HALF_STATES
    for h in range(2):
        uh = u_ref[:, h * SSM_HALF_W:(h + 1) * SSM_HALF_W].astype(BF16)
        st_ref[...] = _dot(uh, bmat_ref[h])

        def block(b, carry):
            cr, ci = carry
            r0 = pl.multiple_of(b * SUBLANES, SUBLANES)
            xr = st_ref[pl.ds(r0, SUBLANES), 0:ns]
            xi = st_ref[pl.ds(r0, SUBLANES), ns:2 * ns]
            for j, shift in enumerate((1, 2, 4)):
                ar, ai = coef_ref[h, j, 0], coef_ref[h, j, 1]
                sr, si = pltpu.roll(xr, shift, 0), pltpu.roll(xi, shift, 0)
                xr, xi = xr + ar * sr - ai * si, xi + ar * si + ai * sr
            pr, pi = coef_ref[h, 3, 0], coef_ref[h, 3, 1]
            xr, xi = xr + pr * cr - pi * ci, xi + pr * ci + pi * cr
            st_ref[pl.ds(r0, SUBLANES), 0:ns] = xr
            st_ref[pl.ds(r0, SUBLANES), ns:2 * ns] = xi
            return (jnp.broadcast_to(xr[SUBLANES - 1:SUBLANES, :], (SUBLANES, ns)),
                    jnp.broadcast_to(xi[SUBLANES - 1:SUBLANES, :], (SUBLANES, ns)))

        cr, ci = lax.fori_loop(0, tm // SUBLANES, block, (carry_ref[h, 0], carry_ref[h, 1]))
        carry_ref[h, 0] = cr
        carry_ref[h, 1] = ci
        y_ref[:, h * SSM_HALF_W:(h + 1) * SSM_HALF_W] = _dot(st_ref[...].astype(BF16), cmat_ref[h])

    carry_out_ref[...] = carry_ref[...]
    z = _gelu(y_ref[...] + d_ref[...] * u_ref[...])
    z = z * jax.nn.sigmoid(_dot(z.astype(BF16), gw_ref[...]) + gb_ref[...])
    ms = jnp.mean(z * z, axis=-1, keepdims=True)
    o_ref[...] = (z * lax.rsqrt(ms + NORM_EPS) * g_ref[...]).astype(BF16)


def _ssm(u, bmat, cmat, coef, carry_in, d, gw_bf, gb, g_out, tm):
    rows = u.shape[0]
    row_spec = pl.BlockSpec((tm, SSM_WIDTH), lambda i: (i, 0))
    full = lambda a: pl.BlockSpec(a.shape, lambda i: (0,) * a.ndim)
    return pl.pallas_call(
        functools.partial(_ssm_kernel, tm=tm),
        grid=(rows // tm,),
        in_specs=[row_spec, full(bmat), full(cmat), full(coef), full(carry_in), full(d), full(gw_bf), full(gb),
                  full(g_out)],
        out_specs=[row_spec, full(carry_in)],
        out_shape=[jax.ShapeDtypeStruct((rows, SSM_WIDTH), BF16),
                   jax.ShapeDtypeStruct(carry_in.shape, F32)],
        scratch_shapes=[pltpu.VMEM((tm, 2 * SSM_HALF_STATES), F32),
                        pltpu.VMEM(carry_in.shape, F32),
                        pltpu.VMEM((tm, SSM_WIDTH), F32)],
        compiler_params=pltpu.CompilerParams(dimension_semantics=("arbitrary",)),
        name="ssm",
    )(u, bmat, cmat, coef, carry_in, d, gw_bf, gb, g_out)


def _ssm_params(a_re, a_im, log_dt, b_re, b_im, c_re, c_im):
    dt = jnp.exp(log_dt)[:, None]
    mag = jnp.exp(a_re * dt)
    abar_r, abar_i = mag * jnp.cos(a_im * dt), mag * jnp.sin(a_im * dt)
    den = a_re * a_re + a_im * a_im
    nr, ni = abar_r - 1.0, abar_i
    coef_r = (nr * a_re + ni * a_im) / den
    coef_i = (ni * a_re - nr * a_im) / den
    bbar_r = coef_r[..., None] * b_re - coef_i[..., None] * b_im
    bbar_i = coef_r[..., None] * b_im + coef_i[..., None] * b_re
    gh = SSM_GROUPS // 2
    eye = jnp.eye(gh, dtype=F32)

    def half_b(bb):
        return jnp.einsum('gnp,gk->gpkn', bb, eye).reshape(gh * SSM_CH, gh * SSM_STATE)

    def half_c(cc):
        return jnp.einsum('gpn,gk->gnkp', cc, eye).reshape(gh * SSM_STATE, gh * SSM_CH)

    bmat = jnp.stack([jnp.concatenate([half_b(bbar_r[s]), half_b(bbar_i[s])], axis=1)
                      for s in (slice(0, gh), slice(gh, None))]).astype(BF16)
    cmat = jnp.stack([jnp.concatenate([half_c(c_re[s]), -half_c(c_im[s])], axis=0)
                      for s in (slice(0, gh), slice(gh, None))]).astype(BF16)

    def cpow(k):
        m = mag ** k
        return m * jnp.cos(a_im * dt * k), m * jnp.sin(a_im * dt * k)

    rows = jnp.arange(SUBLANES)
    coefs = []
    for shift in (1, 2, 4):
        pr, pi = cpow(float(shift))
        keep = (rows >= shift).astype(F32)[:, None, None]
        coefs.append(jnp.stack([keep * pr[None], keep * pi[None]]))
    pw = (rows + 1).astype(F32)[:, None, None]
    m = mag[None] ** pw
    coefs.append(jnp.stack([m * jnp.cos(a_im[None] * dt[None] * pw), m * jnp.sin(a_im[None] * dt[None] * pw)]))
    coef = jnp.stack(coefs)
    coef = coef.reshape(4, 2, SUBLANES, 2, SSM_HALF_STATES).transpose(3, 0, 1, 2, 4)
    return bmat, cmat, coef


def _top16(s, payload, n_rows):
    rowid = lax.broadcasted_iota(I32, s.shape, 0).astype(F32)
    vals, pays = [], []
    for _ in range(PEER_TOPK):
        m = jnp.max(s, axis=0, keepdims=True)
        pos = jnp.min(jnp.where(s == m, rowid, float(n_rows)), axis=0, keepdims=True)
        sel = rowid == pos
        vals.append(m)
        pays.append(pos if payload is None else jnp.max(jnp.where(sel, payload, -1.0), axis=0, keepdims=True))
        s = jnp.where(sel, -jnp.inf, s)
    return jnp.concatenate(vals, axis=0), jnp.concatenate(pays, axis=0)


_CAND_PAIRS = [(a, b) for a in range(PEER_TOPK) for b in range(PEER_TOPK) if (a + 1) * (b + 1) <= PEER_TOPK]
_CAND_ROWS = -(-len(_CAND_PAIRS) // SUBLANES) * SUBLANES


def _mix_kernel(x_ref, a_ref, s_ref, wo_ref, g2_ref, wq_ref, keys_ref, h1_ref, hn_ref, idx_ref, gate_ref,
                qp_ref, idx_t_ref, gate_t_ref, *, tm):
    mix = _dot(a_ref[...], wo_ref[0:ATTN_WIDTH, :]) + _dot(s_ref[...], wo_ref[ATTN_WIDTH:, :])
    h1 = x_ref[...] + mix
    h1_ref[...] = h1.reshape(tm, SUBLANES, LANES)
    ms = jnp.mean(h1 * h1, axis=-1, keepdims=True)
    hn = h1 * lax.rsqrt(ms + NORM_EPS) * g2_ref[...]
    hn_ref[...] = hn.reshape(tm, SUBLANES, LANES)
    qp_ref[...] = _dot(hn.astype(BF16), wq_ref[...])

    def retrieve(h, tok):
        c0 = pl.multiple_of(h * PEER_DK, PEER_DK)
        q1 = qp_ref[tok, pl.ds(c0, PEER_HALF)].astype(BF16)
        q2 = qp_ref[tok, pl.ds(c0 + PEER_HALF, PEER_HALF)].astype(BF16)
        s1 = _dot_nt(keys_ref[h, 0], q1)
        s2 = _dot_nt(keys_ref[h, 1], q2)
        v1, i1 = _top16(s1, None, PEER_N_KEYS)
        v2, i2 = _top16(s2, None, PEER_N_KEYS)
        pad = _CAND_ROWS - len(_CAND_PAIRS)
        cand = jnp.concatenate([v1[a:a + 1, :] + v2[b:b + 1, :] for a, b in _CAND_PAIRS]
                               + [jnp.full((pad, LANES), -jnp.inf, F32)], axis=0)
        eidx = jnp.concatenate([i1[a:a + 1, :] * float(PEER_N_KEYS) + i2[b:b + 1, :] for a, b in _CAND_PAIRS]
                               + [jnp.zeros((pad, LANES), F32)], axis=0)
        sc, ei = _top16(cand, eidx, _CAND_ROWS)
        p = jnp.exp(sc - sc[0:1, :])
        r0 = pl.multiple_of(h * PEER_TOPK, PEER_TOPK)
        idx_t_ref[pl.ds(r0, PEER_TOPK), tok] = ei.astype(I32)
        gate_t_ref[pl.ds(r0, PEER_TOPK), tok] = p / jnp.sum(p, axis=0, keepdims=True)

    def head_pair(i, _):
        for h in (2 * i, 2 * i + 1):
            for sub in range(tm // LANES):
                retrieve(h, slice(sub * LANES, (sub + 1) * LANES))
        return 0

    lax.fori_loop(0, PEER_HEADS // 2, head_pair, 0)
    idx_ref[...] = idx_t_ref[...].T
    gate_ref[...] = gate_t_ref[...].T


def _mix(x, attn_n, ssm_n, w_out_bf, g2, wq_bf, keys_bf, tm):
    rows = x.shape[0]
    row_spec = lambda w: pl.BlockSpec((tm, w), lambda i: (i, 0))
    tile_spec = pl.BlockSpec((tm, SUBLANES, LANES), lambda i: (i, 0, 0))
    full = lambda a: pl.BlockSpec(a.shape, lambda i: (0,) * a.ndim)
    return pl.pallas_call(
        functools.partial(_mix_kernel, tm=tm),
        grid=(rows // tm,),
        in_specs=[row_spec(D_MODEL), row_spec(ATTN_WIDTH), row_spec(SSM_WIDTH), full(w_out_bf), full(g2),
                  full(wq_bf), full(keys_bf)],
        out_specs=[tile_spec, tile_spec, row_spec(PEER_SEL), row_spec(PEER_SEL)],
        out_shape=[jax.ShapeDtypeStruct((rows, SUBLANES, LANES), F32),
                   jax.ShapeDtypeStruct((rows, SUBLANES, LANES), F32),
                   jax.ShapeDtypeStruct((rows, PEER_SEL), I32),
                   jax.ShapeDtypeStruct((rows, PEER_SEL), F32)],
        scratch_shapes=[pltpu.VMEM((tm, PEER_HEADS * PEER_DK), F32),
                        pltpu.VMEM((PEER_SEL, tm), I32),
                        pltpu.VMEM((PEER_SEL, tm), F32)],
        compiler_params=pltpu.CompilerParams(dimension_semantics=("arbitrary",),
                                             vmem_limit_bytes=48 * 1024 * 1024),
        name="mix_topk",
    )(x, attn_n, ssm_n, w_out_bf, g2, wq_bf, keys_bf)


def _expert_tiles(t):
    return t.astype(BF16).reshape(t.shape[0], SUBLANES, LANES)


def _load_table(tab_hbm, tab_ref, sem):
    @pl.when(pl.program_id(0) == 0)
    def _():
        cp = pltpu.make_async_copy(tab_hbm, tab_ref, sem)
        cp.start()
        cp.wait()


def _peer_u_kernel(idx_ref, h_ref, gate_ref, tab_hbm, act_ref, tab_ref, part_a_ref, part_b_ref, sem, *, tt):
    _load_table(tab_hbm, tab_ref, sem)

    def products(t0, part_refs):
        toks = [(idx_ref.at[t0 + j], h_ref[t0 + j], part_refs[j]) for j in range(len(part_refs))]
        for k in range(PEER_SEL):
            for row, hv, part_ref in toks:
                p = tab_ref[row[k]].astype(F32) * hv
                part_ref[k * HALF_TILE:(k + 1) * HALF_TILE, :] = p[0:HALF_TILE] + p[HALF_TILE:]

    def row_sums(part_ref):
        part = part_ref[pl.ds(0, PEER_SEL, stride=HALF_TILE), :]
        for r in range(1, HALF_TILE):
            part = part + part_ref[pl.ds(r, PEER_SEL, stride=HALF_TILE), :]
        return part

    def finish(t, part):
        dots = jnp.sum(part.T, axis=0, keepdims=True)
        act_ref[t] = _gelu(dots) * gate_ref[t]

    products(0, (part_a_ref, part_b_ref))

    def pair(i, _):
        t = 2 * i
        sums_a, sums_b = row_sums(part_a_ref), row_sums(part_b_ref)
        finish(t - 2, sums_a)
        finish(t - 1, sums_b)
        products(t, (part_a_ref, part_b_ref))
        return 0

    lax.fori_loop(1, tt // 2, pair, 0)
    finish(tt - 2, row_sums(part_a_ref))
    finish(tt - 1, row_sums(part_b_ref))


def _peer_u(idx, hn3, gate3, tab, tt, first):
    rows = idx.shape[0]
    b0 = first // tt
    return pl.pallas_call(
        functools.partial(_peer_u_kernel, tt=tt),
        grid=((rows - first) // tt,),
        in_specs=[pl.BlockSpec((tt, PEER_SEL), lambda i: (i + b0, 0), memory_space=pltpu.SMEM),
                  pl.BlockSpec((tt, SUBLANES, LANES), lambda i: (i + b0, 0, 0)),
                  pl.BlockSpec((tt, 1, PEER_SEL), lambda i: (i + b0, 0, 0)),
                  pl.BlockSpec(memory_space=pl.ANY)],
        out_specs=pl.BlockSpec((tt, 1, PEER_SEL), lambda i: (i, 0, 0)),
        out_shape=jax.ShapeDtypeStruct((rows - first, 1, PEER_SEL), F32),
        scratch_shapes=[pltpu.VMEM(tab.shape, BF16),
                        pltpu.VMEM((PEER_SEL * HALF_TILE, LANES), F32),
                        pltpu.VMEM((PEER_SEL * HALF_TILE, LANES), F32),
                        pltpu.SemaphoreType.DMA],
        compiler_params=pltpu.CompilerParams(dimension_semantics=("arbitrary",),
                                             vmem_limit_bytes=VMEM_LIMIT_TABLE),
        name="peer_u",
    )(idx, hn3, gate3, tab)


def _peer_v_kernel(idx_ref, act_ref, h1_ref, tab_hbm, o_ref, tab_ref, tiles_ref, *rest, tt):
    rep_refs, sem = rest[:-1], rest[-1]
    _load_table(tab_hbm, tab_ref, sem)
    n_acc = 2
    g = PEER_V_GROUP

    def replicated(t):
        return jnp.broadcast_to(act_ref[t], (PEER_SEL, LANES)).T

    def accumulate(t0):
        rows = [idx_ref.at[t0 + j] for j in range(g)]
        accs = [[jnp.zeros((SUBLANES, LANES), F32)] * n_acc for _ in range(g)]
        for k in range(PEER_SEL):
            for j, rep_ref in enumerate(rep_refs):
                w = jnp.broadcast_to(rep_ref[k:k + 1, :], (SUBLANES, LANES))
                accs[j][k % n_acc] = accs[j][k % n_acc] + w * tab_ref[rows[j][k]].astype(F32)
        for j in range(g):
            tiles_ref[t0 + j] = h1_ref[t0 + j] + sum(accs[j][1:], accs[j][0])

    for j, rep_ref in enumerate(rep_refs):
        rep_ref[...] = replicated(j)

    def group(i, _):
        t = g * i
        nxt = [replicated(jnp.minimum(t + g + j, tt - 1)) for j in range(g)]
        accumulate(t)
        for rep_ref, tile in zip(rep_refs, nxt):
            rep_ref[...] = tile
        return 0

    lax.fori_loop(0, tt // g, group, 0)
    o_ref[...] = tiles_ref[...].reshape(tt, D_MODEL)


def _peer_v(idx, act3, h13, tab, tt, first):
    rows = idx.shape[0]
    b0 = first // tt
    tile = pl.BlockSpec((tt, SUBLANES, LANES), lambda i: (i + b0, 0, 0))
    return pl.pallas_call(
        functools.partial(_peer_v_kernel, tt=tt),
        grid=((rows - first) // tt,),
        in_specs=[pl.BlockSpec((tt, PEER_SEL), lambda i: (i + b0, 0), memory_space=pltpu.SMEM),
                  pl.BlockSpec((tt, 1, PEER_SEL), lambda i: (i + b0, 0, 0)),
                  tile, pl.BlockSpec(memory_space=pl.ANY)],
        out_specs=pl.BlockSpec((tt, D_MODEL), lambda i: (i, 0)),
        out_shape=jax.ShapeDtypeStruct((rows - first, D_MODEL), F32),
        scratch_shapes=[pltpu.VMEM(tab.shape, BF16), pltpu.VMEM((tt, SUBLANES, LANES), F32)]
                       + [pltpu.VMEM((PEER_SEL, LANES), F32)] * PEER_V_GROUP
                       + [pltpu.SemaphoreType.DMA],
        compiler_params=pltpu.CompilerParams(dimension_semantics=("arbitrary",),
                                             vmem_limit_bytes=VMEM_LIMIT_TABLE),
        name="peer_v",
    )(idx, act3, h13, tab)


SC_LANES = 16
SC_WORKERS = 32
SC_ROWS = 32
SC_COLS = 128
SC_ROW_GROUP = 4
SC_SHARE_U = (3, 16)
SC_SHARE_V = (5, 16)


def _peer_v_sc_kernel(tab_hbm, idx_hbm, act_hbm, h1_hbm, out_hbm, idx_v, act_v, rows0_v, rows1_v, acc_v, sem0, sem1,
                      *, n_tok):
    wid = lax.axis_index("s") * 2 + lax.axis_index("c")
    bufs = ((rows0_v, sem0), (rows1_v, sem1))
    n_chunks = PEER_SEL // SC_ROWS

    def gather(c):
        rows_v, sem = bufs[c % 2]
        return pltpu.make_async_copy(tab_hbm.at[idx_v.at[pl.ds(c * SC_ROWS, SC_ROWS)]], rows_v, sem)

    @pl.loop(0, n_tok)
    def _(i):
        t = wid * n_tok + i
        pltpu.sync_copy(idx_hbm.at[pl.ds(t * PEER_SEL, PEER_SEL)], idx_v)
        gather(0).start()
        pltpu.sync_copy(act_hbm.at[pl.ds(t * PEER_SEL, PEER_SEL)], act_v)
        pltpu.sync_copy(h1_hbm.at[t], acc_v)
        for c in range(n_chunks):
            if c + 1 < n_chunks:
                gather(c + 1).start()
            gather(c).wait()
            rows_v = bufs[c % 2][0]

            @pl.loop(0, D_MODEL // SC_COLS)
            def _(cb):
                cols = [pl.ds(cb * SC_COLS + j * SC_LANES, SC_LANES) for j in range(SC_COLS // SC_LANES)]

                @pl.loop(0, SC_ROWS // SC_ROW_GROUP)
                def _(rg):
                    accs = [acc_v[col] for col in cols]
                    for rr in range(SC_ROW_GROUP):
                        r = rg * SC_ROW_GROUP + rr
                        w = plsc.load_gather(act_v, [jnp.full((SC_LANES,), c * SC_ROWS, I32) + r])
                        accs = [a + w * rows_v[r, col] for a, col in zip(accs, cols)]
                    for a, col in zip(accs, cols):
                        acc_v[col] = a
        pltpu.sync_copy(acc_v, out_hbm.at[t])


def _peer_v_sc(tab, idx_flat, act_flat, h1_rows):
    n = h1_rows.shape[0]
    mesh = plsc.VectorSubcoreMesh(core_axis_name="c", subcore_axis_name="s")
    return pl.kernel(
        functools.partial(_peer_v_sc_kernel, n_tok=n // SC_WORKERS),
        out_type=jax.ShapeDtypeStruct((n, D_MODEL), F32),
        mesh=mesh,
        scratch_types=[pltpu.VMEM((PEER_SEL,), I32), pltpu.VMEM((PEER_SEL,), F32),
                       pltpu.VMEM((SC_ROWS, D_MODEL), F32), pltpu.VMEM((SC_ROWS, D_MODEL), F32),
                       pltpu.VMEM((D_MODEL,), F32), pltpu.SemaphoreType.DMA, pltpu.SemaphoreType.DMA],
        compiler_params=pltpu.CompilerParams(needs_layout_passes=False),
        name="peer_v_sc",
    )(tab, idx_flat, act_flat, h1_rows)


def _peer_u_sc_kernel(tab_hbm, idx_hbm, hn_hbm, part_hbm, idx_v, h_v, rows0_v, rows1_v, part_v, sem0, sem1, *, n_tok):
    wid = lax.axis_index("s") * 2 + lax.axis_index("c")
    bufs = ((rows0_v, sem0), (rows1_v, sem1))
    n_chunks = PEER_SEL // SC_ROWS
    part_w = PEER_SEL * SC_LANES

    def gather(c):
        rows_v, sem = bufs[c % 2]
        return pltpu.make_async_copy(tab_hbm.at[idx_v.at[pl.ds(c * SC_ROWS, SC_ROWS)]], rows_v, sem)

    @pl.loop(0, n_tok)
    def _(i):
        t = wid * n_tok + i
        pltpu.sync_copy(idx_hbm.at[pl.ds(t * PEER_SEL, PEER_SEL)], idx_v)
        gather(0).start()
        pltpu.sync_copy(hn_hbm.at[t], h_v)

        @pl.loop(0, PEER_SEL)
        def _(k):
            part_v[pl.ds(k * SC_LANES, SC_LANES)] = jnp.zeros((SC_LANES,), F32)

        for c in range(n_chunks):
            if c + 1 < n_chunks:
                gather(c + 1).start()
            gather(c).wait()
            rows_v = bufs[c % 2][0]

            @pl.loop(0, D_MODEL // SC_COLS)
            def _(cb):
                cols = [pl.ds(cb * SC_COLS + j * SC_LANES, SC_LANES) for j in range(SC_COLS // SC_LANES)]
                hs = [h_v[col] for col in cols]

                @pl.loop(0, SC_ROWS // SC_ROW_GROUP)
                def _(rg):
                    for rr in range(SC_ROW_GROUP):
                        r = rg * SC_ROW_GROUP + rr
                        slot = pl.ds((c * SC_ROWS + r) * SC_LANES, SC_LANES)
                        prods = [h * rows_v[r, col] for h, col in zip(hs, cols)]
                        while len(prods) > 1:
                            prods = [a + b for a, b in zip(prods[0::2], prods[1::2])]
                        part_v[slot] = part_v[slot] + prods[0]
        pltpu.sync_copy(part_v, part_hbm.at[pl.ds(t * part_w, part_w)])


def _peer_u_sc(tab, idx_flat, hn_rows):
    n = hn_rows.shape[0]
    mesh = plsc.VectorSubcoreMesh(core_axis_name="c", subcore_axis_name="s")
    return pl.kernel(
        functools.partial(_peer_u_sc_kernel, n_tok=n // SC_WORKERS),
        out_type=jax.ShapeDtypeStruct((n * PEER_SEL * SC_LANES,), F32),
        mesh=mesh,
        scratch_types=[pltpu.VMEM((PEER_SEL,), I32), pltpu.VMEM((D_MODEL,), F32),
                       pltpu.VMEM((SC_ROWS, D_MODEL), F32), pltpu.VMEM((SC_ROWS, D_MODEL), F32),
                       pltpu.VMEM((PEER_SEL * SC_LANES,), F32), pltpu.SemaphoreType.DMA, pltpu.SemaphoreType.DMA],
        compiler_params=pltpu.CompilerParams(needs_layout_passes=False),
        name="peer_u_sc",
    )(tab, idx_flat, hn_rows)


def _fold_act_kernel(part_ref, fold_ref, gate_ref, act_ref):
    p = part_ref[...]
    hi = p.astype(BF16)
    lo = (p - hi.astype(F32)).astype(BF16)
    dots = _dot(hi, fold_ref[...]) + _dot(lo, fold_ref[...])
    act_ref[...] = _gelu(dots) * gate_ref[...]


def _fold_act(part2, gate, tm):
    n = part2.shape[0]
    fold = (jnp.arange(PEER_SEL * SC_LANES)[:, None] // SC_LANES == jnp.arange(PEER_SEL)[None, :]).astype(BF16)
    return pl.pallas_call(
        _fold_act_kernel,
        grid=(n // tm,),
        in_specs=[pl.BlockSpec((tm, PEER_SEL * SC_LANES), lambda i: (i, 0)),
                  pl.BlockSpec(fold.shape, lambda i: (0, 0)),
                  pl.BlockSpec((tm, PEER_SEL), lambda i: (i, 0))],
        out_specs=pl.BlockSpec((tm, PEER_SEL), lambda i: (i, 0)),
        out_shape=jax.ShapeDtypeStruct((n, PEER_SEL), F32),
        compiler_params=pltpu.CompilerParams(dimension_semantics=("arbitrary",)),
        name="fold_act",
    )(part2, fold, gate)


def _block(x2, meta_tokens, norm1_g, w_in, q_norm_g, k_norm_g, attn_sinks, ssm_a_re, ssm_a_im, ssm_log_dt,
           ssm_b_re, ssm_b_im, ssm_c_re, ssm_c_im, ssm_d, ssm_glu_w, ssm_glu_b, attn_out_g, ssm_out_g, w_out,
           norm2_g, peer_w_query, peer_sub_keys, peer_u, peer_v, *, tm_proj, tm_ssm, tm_mix, tt_peer):
    rows = x2.shape[0]
    row2 = lambda a: a.reshape(1, -1).astype(F32)
    w_in_bf = w_in.astype(BF16)
    seg = jnp.arange(QK_WIDTH) // HEAD_DIM
    ones_bd = (seg[:, None] == seg[None, :]).astype(BF16)
    scale = HEAD_DIM ** -0.5
    qkg = jnp.concatenate([jnp.tile(q_norm_g.astype(F32), N_Q_HEADS) * scale,
                           jnp.tile(k_norm_g.astype(F32), N_KV_HEADS)]).reshape(1, QK_WIDTH)
    bmat, cmat, coef = _ssm_params(ssm_a_re.astype(F32), ssm_a_im.astype(F32), ssm_log_dt.astype(F32),
                                   ssm_b_re.astype(F32), ssm_b_im.astype(F32), ssm_c_re.astype(F32),
                                   ssm_c_im.astype(F32))
    glu_w_bf = ssm_glu_w.astype(BF16)
    ssm_args = (row2(ssm_d), glu_w_bf, row2(ssm_glu_b), row2(ssm_out_g))

    h0 = jnp.concatenate([jnp.zeros((BLOCK - N_META, D_MODEL), F32), meta_tokens.astype(F32)], axis=0)
    _, k0, vt0, u0 = _in_proj(h0, row2(norm1_g), w_in_bf, ones_bd, qkg, BLOCK)
    carry0 = jnp.zeros((2, 2, SUBLANES, SSM_HALF_STATES), F32)
    _, carry = _ssm(u0, bmat, cmat, coef, carry0, *ssm_args, BLOCK)

    qt, k, vt, u = _in_proj(x2, row2(norm1_g), w_in_bf, ones_bd, qkg, tm_proj)
    g_attn = jnp.broadcast_to(attn_out_g.astype(F32)[:, None], (ATTN_WIDTH, BLOCK))
    attn_n = _attention(attn_sinks.astype(F32), qt, k, vt, k0[BLOCK - N_META:], vt0[:, BLOCK - N_META:], g_attn)
    ssm_n, _ = _ssm(u, bmat, cmat, coef, carry, *ssm_args, tm_ssm)

    h1, hn, idx, gate = _mix(x2, attn_n, ssm_n, w_out.astype(BF16), row2(norm2_g), peer_w_query.astype(BF16),
                             peer_sub_keys.astype(BF16), tm_mix)
    share = lambda num, den: rows * num // den // (SC_WORKERS * tt_peer) * (SC_WORKERS * tt_peer)
    n_u, n_v = share(*SC_SHARE_U), share(*SC_SHARE_V)
    part = _peer_u_sc(peer_u.astype(F32), idx[:n_u].reshape(-1), hn[:n_u].reshape(n_u, D_MODEL))
    act_tc = _peer_u(idx, hn, gate.reshape(rows, 1, PEER_SEL), _expert_tiles(peer_u), tt_peer, n_u)
    act_sc = _fold_act(part.reshape(n_u, PEER_SEL * SC_LANES), gate[:n_u], tm_mix)
    act = jnp.concatenate([act_sc.reshape(n_u, 1, PEER_SEL), act_tc], axis=0)
    sc_out = _peer_v_sc(peer_v.astype(F32), idx[:n_v].reshape(-1), act[:n_v].reshape(-1),
                        h1[:n_v].reshape(n_v, D_MODEL))
    tc_out = _peer_v(idx, act, h1, _expert_tiles(peer_v), tt_peer, n_v)
    return jnp.concatenate([sc_out, tc_out], axis=0)


def kernel(x, meta_tokens, norm1_g, w_in, q_norm_g, k_norm_g, attn_sinks, ssm_a_re, ssm_a_im, ssm_log_dt, ssm_b_re, ssm_b_im, ssm_c_re, ssm_c_im, ssm_d, ssm_glu_w, ssm_glu_b, attn_out_g, ssm_out_g, w_out, norm2_g, peer_w_query, peer_sub_keys, peer_u, peer_v):
    b, seq, d = x.shape
    outs = []
    for bi in range(b):
        outs.append(_block(x[bi].astype(F32), meta_tokens, norm1_g[0], w_in[0], q_norm_g[0], k_norm_g[0],
                           attn_sinks[0], ssm_a_re[0], ssm_a_im[0], ssm_log_dt[0], ssm_b_re[0], ssm_b_im[0],
                           ssm_c_re[0], ssm_c_im[0], ssm_d[0], ssm_glu_w[0], ssm_glu_b[0], attn_out_g[0],
                           ssm_out_g[0], w_out[0], norm2_g[0], peer_w_query[0], peer_sub_keys[0], peer_u[0],
                           peer_v[0], tm_proj=512, tm_ssm=256, tm_mix=256, tt_peer=32))
    return jnp.stack(outs).astype(x.dtype)
```

```python
import functools
import math

import jax
import jax.numpy as jnp
from jax import lax
from jax.experimental import pallas as pl
from jax.experimental.pallas import tpu as pltpu
from jax.experimental.pallas import tpu_sc as plsc

F32 = jnp.float32
BF16 = jnp.bfloat16
I32 = jnp.int32

D_MODEL = 1024
N_META = 16
BLOCK = 128
ATTN_WIDTH = 512
HEAD_DIM = 64
N_Q_HEADS = 8
N_KV_HEADS = 2
GQA = 4
KV_WIDTH = 128
QK_WIDTH = ATTN_WIDTH + KV_WIDTH
SSM_WIDTH = 512
SSM_CH = 16
SSM_GROUPS = 32
SSM_STATE = 64
IN_WIDTH = ATTN_WIDTH + 2 * KV_WIDTH + SSM_WIDTH
PEER_HEADS = 8
PEER_N_KEYS = 128
PEER_TOPK = 16
PEER_DK = 256
PEER_HALF = 128
PEER_SEL = PEER_HEADS * PEER_TOPK
NORM_EPS = 1e-6
MASK_VALUE = -1e30

SUBLANES = 8
LANES = 128
SSM_HALF_W = SSM_WIDTH // 2
SSM_HALF_STATES = SSM_GROUPS // 2 * SSM_STATE
HALF_TILE = SUBLANES // 2
PEER_V_GROUP = 2
VMEM_LIMIT_TABLE = 52 * 1024 * 1024


def _dot(a, b):
    return jnp.dot(a, b, preferred_element_type=F32)


def _dot_nt(a, b):
    return lax.dot_general(a, b, (((1,), (1,)), ((), ())), preferred_element_type=F32)


def _gelu(x):
    return 0.5 * x * (1.0 + lax.erf(x * (2.0 ** -0.5)))


def _in_proj_kernel(x_ref, g1_ref, w_ref, ones_ref, qkg_ref, qt_ref, k_ref, vt_ref, u_ref):
    x = x_ref[...]
    ms = jnp.mean(x * x, axis=-1, keepdims=True)
    y = (x * lax.rsqrt(ms + NORM_EPS) * g1_ref[...]).astype(BF16)
    proj = _dot(y, w_ref[...])
    qk = proj[:, :QK_WIDTH]
    ss = _dot((qk * qk).astype(BF16), ones_ref[...]) * (1.0 / HEAD_DIM)
    qkn = qk * lax.rsqrt(ss + NORM_EPS) * qkg_ref[...]
    qt_ref[...] = qkn[:, :ATTN_WIDTH].T.astype(BF16)
    k_ref[...] = qkn[:, ATTN_WIDTH:].astype(BF16)
    vt_ref[...] = proj[:, QK_WIDTH:QK_WIDTH + KV_WIDTH].T.astype(BF16)
    u_ref[...] = proj[:, QK_WIDTH + KV_WIDTH:]


def _in_proj(x, g1, w_in_bf, ones_bd, qkg, tm):
    rows = x.shape[0]
    row_spec = lambda w: pl.BlockSpec((tm, w), lambda i: (i, 0))
    col_spec = lambda w: pl.BlockSpec((w, tm), lambda i: (0, i))
    full = lambda a: pl.BlockSpec(a.shape, lambda i: (0,) * a.ndim)
    return pl.pallas_call(
        _in_proj_kernel,
        grid=(rows // tm,),
        in_specs=[row_spec(D_MODEL), full(g1), full(w_in_bf), full(ones_bd), full(qkg)],
        out_specs=[col_spec(ATTN_WIDTH), row_spec(KV_WIDTH), col_spec(KV_WIDTH), row_spec(SSM_WIDTH)],
        out_shape=[jax.ShapeDtypeStruct((ATTN_WIDTH, rows), BF16),
                   jax.ShapeDtypeStruct((rows, KV_WIDTH), BF16),
                   jax.ShapeDtypeStruct((KV_WIDTH, rows), BF16),
                   jax.ShapeDtypeStruct((rows, SSM_WIDTH), F32)],
        compiler_params=pltpu.CompilerParams(dimension_semantics=("arbitrary",)),
        name="in_proj",
    )(x, g1, w_in_bf, ones_bd, qkg)


def _attn_kernel(sink_ref, qt_ref, kp_ref, kc_ref, vtp_ref, vtc_ref, km_ref, vtm_ref, g_ref, o_ref, acc_ref):
    i = pl.program_id(0)
    group_w = GQA * BLOCK
    key = lax.broadcasted_iota(I32, (BLOCK, group_w), 0)
    qry = lax.broadcasted_iota(I32, (BLOCK, group_w), 1) & (BLOCK - 1)
    prev_ok = jnp.logical_and(key > qry, i > 0)
    cur_ok = key <= qry
    for kv in range(N_KV_HEADS):
        hs = slice(kv * HEAD_DIM, (kv + 1) * HEAD_DIM)
        heads = range(kv * GQA, (kv + 1) * GQA)
        qt = jnp.concatenate([qt_ref[h * HEAD_DIM:(h + 1) * HEAD_DIM, :] for h in heads], axis=1)
        sink = jnp.concatenate([jnp.full((1, BLOCK), sink_ref[h], F32) for h in heads], axis=1)
        sp = jnp.where(prev_ok, _dot(kp_ref[:, hs], qt), MASK_VALUE)
        sc = jnp.where(cur_ok, _dot(kc_ref[:, hs], qt), MASK_VALUE)
        sm = _dot(km_ref[:, hs], qt)
        m = jnp.maximum(jnp.maximum(jnp.max(sp, axis=0, keepdims=True), jnp.max(sc, axis=0, keepdims=True)),
                        jnp.maximum(jnp.max(sm, axis=0, keepdims=True), sink))
        pp, pc, pm = jnp.exp(sp - m), jnp.exp(sc - m), jnp.exp(sm - m)
        denom = (jnp.sum(pp, axis=0, keepdims=True) + jnp.sum(pc, axis=0, keepdims=True)
                 + jnp.sum(pm, axis=0, keepdims=True) + jnp.exp(sink - m))
        ot = (_dot(vtp_ref[hs, :], pp.astype(BF16)) + _dot(vtc_ref[hs, :], pc.astype(BF16))
              + _dot(vtm_ref[hs, :], pm.astype(BF16))) / denom
        for g, h in enumerate(heads):
            acc_ref[h * HEAD_DIM:(h + 1) * HEAD_DIM, :] = ot[:, g * BLOCK:(g + 1) * BLOCK]
    a = acc_ref[...]
    ms = jnp.mean(a * a, axis=0, keepdims=True)
    o_ref[...] = (a * lax.rsqrt(ms + NORM_EPS) * g_ref[...]).T.astype(BF16)


def _attention(sinks, qt, k, vt, k_meta, vt_meta, g_out):
    nb = k.shape[0] // BLOCK
    prev_i = lambda i: jnp.maximum(i - 1, 0)
    full = lambda a: pl.BlockSpec(a.shape, lambda i: (0,) * a.ndim)
    return pl.pallas_call(
        _attn_kernel,
        grid=(nb,),
        in_specs=[pl.BlockSpec(memory_space=pltpu.SMEM),
                  pl.BlockSpec((ATTN_WIDTH, BLOCK), lambda i: (0, i)),
                  pl.BlockSpec((BLOCK, KV_WIDTH), lambda i: (prev_i(i), 0)),
                  pl.BlockSpec((BLOCK, KV_WIDTH), lambda i: (i, 0)),
                  pl.BlockSpec((KV_WIDTH, BLOCK), lambda i: (0, prev_i(i))),
                  pl.BlockSpec((KV_WIDTH, BLOCK), lambda i: (0, i)),
                  full(k_meta), full(vt_meta), full(g_out)],
        out_specs=pl.BlockSpec((BLOCK, ATTN_WIDTH), lambda i: (i, 0)),
        out_shape=jax.ShapeDtypeStruct((k.shape[0], ATTN_WIDTH), BF16),
        scratch_shapes=[pltpu.VMEM((ATTN_WIDTH, BLOCK), F32)],
        compiler_params=pltpu.CompilerParams(dimension_semantics=("arbitrary",)),
        name="attention",
    )(sinks, qt, k, k, vt, vt, k_meta, vt_meta, g_out)


def _ssm_kernel(u_ref, bmat_ref, cmat_ref, coef_ref, carry_in_ref, d_ref, gw_ref, gb_ref, g_ref,
                o_ref, carry_out_ref, st_ref, carry_ref, y_ref, *, tm):
    @pl.when(pl.program_id(0) == 0)
    def _():
        carry_ref[...] = carry_in_ref[...]

    ns = SSM_HALF_STATES
    for h in range(2):
        uh = u_ref[:, h * SSM_HALF_W:(h + 1) * SSM_HALF_W].astype(BF16)
        st_ref[...] = _dot(uh, bmat_ref[h])

        def block(b, carry):
            cr, ci = carry
            r0 = pl.multiple_of(b * SUBLANES, SUBLANES)
            xr = st_ref[pl.ds(r0, SUBLANES), 0:ns]
            xi = st_ref[pl.ds(r0, SUBLANES), ns:2 * ns]
            for j, shift in enumerate((1, 2, 4)):
                ar, ai = coef_ref[h, j, 0], coef_ref[h, j, 1]
                sr, si = pltpu.roll(xr, shift, 0), pltpu.roll(xi, shift, 0)
                xr, xi = xr + ar * sr - ai * si, xi + ar * si + ai * sr
            pr, pi = coef_ref[h, 3, 0], coef_ref[h, 3, 1]
            xr, xi = xr + pr * cr - pi * ci, xi + pr * ci + pi * cr
            st_ref[pl.ds(r0, SUBLANES), 0:ns] = xr
            st_ref[pl.ds(r0, SUBLANES), ns:2 * ns] = xi
            return (jnp.broadcast_to(xr[SUBLANES - 1:SUBLANES, :], (SUBLANES, ns)),
                    jnp.broadcast_to(xi[SUBLANES - 1:SUBLANES, :], (SUBLANES, ns)))

        cr, ci = lax.fori_loop(0, tm // SUBLANES, block, (carry_ref[h, 0], carry_ref[h, 1]))
        carry_ref[h, 0] = cr
        carry_ref[h, 1] = ci
        y_ref[:, h * SSM_HALF_W:(h + 1) * SSM_HALF_W] = _dot(st_ref[...].astype(BF16), cmat_ref[h])

    carry_out_ref[...] = carry_ref[...]
    z = _gelu(y_ref[...] + d_ref[...] * u_ref[...])
    z = z * jax.nn.sigmoid(_dot(z.astype(BF16), gw_ref[...]) + gb_ref[...])
    ms = jnp.mean(z * z, axis=-1, keepdims=True)
    o_ref[...] = (z * lax.rsqrt(ms + NORM_EPS) * g_ref[...]).astype(BF16)


def _ssm(u, bmat, cmat, coef, carry_in, d, gw_bf, gb, g_out, tm):
    rows = u.shape[0]
    row_spec = pl.BlockSpec((tm, SSM_WIDTH), lambda i: (i, 0))
    full = lambda a: pl.BlockSpec(a.shape, lambda i: (0,) * a.ndim)
    return pl.pallas_call(
        functools.partial(_ssm_kernel, tm=tm),
        grid=(rows // tm,),
        in_specs=[row_spec, full(bmat), full(cmat), full(coef), full(carry_in), full(d), full(gw_bf), full(gb),
                  full(g_out)],
        out_specs=[row_spec, full(carry_in)],
        out_shape=[jax.ShapeDtypeStruct((rows, SSM_WIDTH), BF16),
                   jax.ShapeDtypeStruct(carry_in.shape, F32)],
        scratch_shapes=[pltpu.VMEM((tm, 2 * SSM_HALF_STATES), F32),
                        pltpu.VMEM(carry_in.shape, F32),
                        pltpu.VMEM((tm, SSM_WIDTH), F32)],
        compiler_params=pltpu.CompilerParams(dimension_semantics=("arbitrary",)),
        name="ssm",
    )(u, bmat, cmat, coef, carry_in, d, gw_bf, gb, g_out)


def _ssm_params(a_re, a_im, log_dt, b_re, b_im, c_re, c_im):
    dt = jnp.exp(log_dt)[:, None]
    mag = jnp.exp(a_re * dt)
    abar_r, abar_i = mag * jnp.cos(a_im * dt), mag * jnp.sin(a_im * dt)
    den = a_re * a_re + a_im * a_im
    nr, ni = abar_r - 1.0, abar_i
    coef_r = (nr * a_re + ni * a_im) / den
    coef_i = (ni * a_re - nr * a_im) / den
    bbar_r = coef_r[..., None] * b_re - coef_i[..., None] * b_im
    bbar_i = coef_r[..., None] * b_im + coef_i[..., None] * b_re
    gh = SSM_GROUPS // 2
    eye = jnp.eye(gh, dtype=F32)

    def half_b(bb):
        return jnp.einsum('gnp,gk->gpkn', bb, eye).reshape(gh * SSM_CH, gh * SSM_STATE)

    def half_c(cc):
        return jnp.einsum('gpn,gk->gnkp', cc, eye).reshape(gh * SSM_STATE, gh * SSM_CH)

    bmat = jnp.stack([jnp.concatenate([half_b(bbar_r[s]), half_b(bbar_i[s])], axis=1)
                      for s in (slice(0, gh), slice(gh, None))]).astype(BF16)
    cmat = jnp.stack([jnp.concatenate([half_c(c_re[s]), -half_c(c_im[s])], axis=0)
                      for s in (slice(0, gh), slice(gh, None))]).astype(BF16)

    def cpow(k):
        m = mag ** k
        return m * jnp.cos(a_im * dt * k), m * jnp.sin(a_im * dt * k)

    rows = jnp.arange(SUBLANES)
    coefs = []
    for shift in (1, 2, 4):
        pr, pi = cpow(float(shift))
        keep = (rows >= shift).astype(F32)[:, None, None]
        coefs.append(jnp.stack([keep * pr[None], keep * pi[None]]))
    pw = (rows + 1).astype(F32)[:, None, None]
    m = mag[None] ** pw
    coefs.append(jnp.stack([m * jnp.cos(a_im[None] * dt[None] * pw), m * jnp.sin(a_im[None] * dt[None] * pw)]))
    coef = jnp.stack(coefs)
    coef = coef.reshape(4, 2, SUBLANES, 2, SSM_HALF_STATES).transpose(3, 0, 1, 2, 4)
    return bmat, cmat, coef


def _top16(s, payload, n_rows):
    rowid = lax.broadcasted_iota(I32, s.shape, 0).astype(F32)
    vals, pays = [], []
    for _ in range(PEER_TOPK):
        m = jnp.max(s, axis=0, keepdims=True)
        pos = jnp.min(jnp.where(s == m, rowid, float(n_rows)), axis=0, keepdims=True)
        sel = rowid == pos
        vals.append(m)
        pays.append(pos if payload is None else jnp.max(jnp.where(sel, payload, -1.0), axis=0, keepdims=True))
        s = jnp.where(sel, -jnp.inf, s)
    return jnp.concatenate(vals, axis=0), jnp.concatenate(pays, axis=0)


_CAND_PAIRS = [(a, b) for a in range(PEER_TOPK) for b in range(PEER_TOPK) if (a + 1) * (b + 1) <= PEER_TOPK]
_CAND_ROWS = -(-len(_CAND_PAIRS) // SUBLANES) * SUBLANES


def _mix_kernel(x_ref, a_ref, s_ref, wo_ref, g2_ref, wq_ref, keys_ref, h1_ref, hn_ref, idx_ref, gate_ref,
                qp_ref, idx_t_ref, gate_t_ref, *, tm):
    mix = _dot(a_ref[...], wo_ref[0:ATTN_WIDTH, :]) + _dot(s_ref[...], wo_ref[ATTN_WIDTH:, :])
    h1 = x_ref[...] + mix
    h1_ref[...] = h1.reshape(tm, SUBLANES, LANES)
    ms = jnp.mean(h1 * h1, axis=-1, keepdims=True)
    hn = h1 * lax.rsqrt(ms + NORM_EPS) * g2_ref[...]
    hn_ref[...] = hn.reshape(tm, SUBLANES, LANES)
    qp_ref[...] = _dot(hn.astype(BF16), wq_ref[...])

    def retrieve(h, tok):
        c0 = pl.multiple_of(h * PEER_DK, PEER_DK)
        q1 = qp_ref[tok, pl.ds(c0, PEER_HALF)].astype(BF16)
        q2 = qp_ref[tok, pl.ds(c0 + PEER_HALF, PEER_HALF)].astype(BF16)
        s1 = _dot_nt(keys_ref[h, 0], q1)
        s2 = _dot_nt(keys_ref[h, 1], q2)
        v1, i1 = _top16(s1, None, PEER_N_KEYS)
        v2, i2 = _top16(s2, None, PEER_N_KEYS)
        pad = _CAND_ROWS - len(_CAND_PAIRS)
        cand = jnp.concatenate([v1[a:a + 1, :] + v2[b:b + 1, :] for a, b in _CAND_PAIRS]
                               + [jnp.full((pad, LANES), -jnp.inf, F32)], axis=0)
        eidx = jnp.concatenate([i1[a:a + 1, :] * float(PEER_N_KEYS) + i2[b:b + 1, :] for a, b in _CAND_PAIRS]
                               + [jnp.zeros((pad, LANES), F32)], axis=0)
        sc, ei = _top16(cand, eidx, _CAND_ROWS)
        p = jnp.exp(sc - sc[0:1, :])
        r0 = pl.multiple_of(h * PEER_TOPK, PEER_TOPK)
        idx_t_ref[pl.ds(r0, PEER_TOPK), tok] = ei.astype(I32)
        gate_t_ref[pl.ds(r0, PEER_TOPK), tok] = p / jnp.sum(p, axis=0, keepdims=True)

    def head_pair(i, _):
        for h in (2 * i, 2 * i + 1):
            for sub in range(tm // LANES):
                retrieve(h, slice(sub * LANES, (sub + 1) * LANES))
        return 0

    lax.fori_loop(0, PEER_HEADS // 2, head_pair, 0)
    idx_ref[...] = idx_t_ref[...].T
    gate_ref[...] = gate_t_ref[...].T


def _mix(x, attn_n, ssm_n, w_out_bf, g2, wq_bf, keys_bf, tm):
    rows = x.shape[0]
    row_spec = lambda w: pl.BlockSpec((tm, w), lambda i: (i, 0))
    tile_spec = pl.BlockSpec((tm, SUBLANES, LANES), lambda i: (i, 0, 0))
    full = lambda a: pl.BlockSpec(a.shape, lambda i: (0,) * a.ndim)
    return pl.pallas_call(
        functools.partial(_mix_kernel, tm=tm),
        grid=(rows // tm,),
        in_specs=[row_spec(D_MODEL), row_spec(ATTN_WIDTH), row_spec(SSM_WIDTH), full(w_out_bf), full(g2),
                  full(wq_bf), full(keys_bf)],
        out_specs=[tile_spec, tile_spec, row_spec(PEER_SEL), row_spec(PEER_SEL)],
        out_shape=[jax.ShapeDtypeStruct((rows, SUBLANES, LANES), F32),
                   jax.ShapeDtypeStruct((rows, SUBLANES, LANES), F32),
                   jax.ShapeDtypeStruct((rows, PEER_SEL), I32),
                   jax.ShapeDtypeStruct((rows, PEER_SEL), F32)],
        scratch_shapes=[pltpu.VMEM((tm, PEER_HEADS * PEER_DK), F32),
                        pltpu.VMEM((PEER_SEL, tm), I32),
                        pltpu.VMEM((PEER_SEL, tm), F32)],
        compiler_params=pltpu.CompilerParams(dimension_semantics=("arbitrary",),
                                             vmem_limit_bytes=48 * 1024 * 1024),
        name="mix_topk",
    )(x, attn_n, ssm_n, w_out_bf, g2, wq_bf, keys_bf)


def _expert_tiles(t):
    return t.astype(BF16).reshape(t.shape[0], SUBLANES, LANES)


def _load_table(tab_hbm, tab_ref, sem):
    @pl.when(pl.program_id(0) == 0)
    def _():
        cp = pltpu.make_async_copy(tab_hbm, tab_ref, sem)
        cp.start()
        cp.wait()


def _peer_u_kernel(idx_ref, h_ref, gate_ref, tab_hbm, act_ref, tab_ref, part_a_ref, part_b_ref, sem, *, tt):
    _load_table(tab_hbm, tab_ref, sem)

    def products(t0, part_refs):
        toks = [(idx_ref.at[t0 + j], h_ref[t0 + j], part_refs[j]) for j in range(len(part_refs))]
        for k in range(PEER_SEL):
            for row, hv, part_ref in toks:
                p = tab_ref[row[k]].astype(F32) * hv
                part_ref[k * HALF_TILE:(k + 1) * HALF_TILE, :] = p[0:HALF_TILE] + p[HALF_TILE:]

    def row_sums(part_ref):
        part = part_ref[pl.ds(0, PEER_SEL, stride=HALF_TILE), :]
        for r in range(1, HALF_TILE):
            part = part + part_ref[pl.ds(r, PEER_SEL, stride=HALF_TILE), :]
        return part

    def finish(t, part):
        dots = jnp.sum(part.T, axis=0, keepdims=True)
        act_ref[t] = _gelu(dots) * gate_ref[t]

    products(0, (part_a_ref, part_b_ref))

    def pair(i, _):
        t = 2 * i
        sums_a, sums_b = row_sums(part_a_ref), row_sums(part_b_ref)
        finish(t - 2, sums_a)
        finish(t - 1, sums_b)
        products(t, (part_a_ref, part_b_ref))
        return 0

    lax.fori_loop(1, tt // 2, pair, 0)
    finish(tt - 2, row_sums(part_a_ref))
    finish(tt - 1, row_sums(part_b_ref))


def _peer_u(idx, hn3, gate3, tab, tt, first):
    rows = idx.shape[0]
    b0 = first // tt
    return pl.pallas_call(
        functools.partial(_peer_u_kernel, tt=tt),
        grid=((rows - first) // tt,),
        in_specs=[pl.BlockSpec((tt, PEER_SEL), lambda i: (i + b0, 0), memory_space=pltpu.SMEM),
                  pl.BlockSpec((tt, SUBLANES, LANES), lambda i: (i + b0, 0, 0)),
                  pl.BlockSpec((tt, 1, PEER_SEL), lambda i: (i + b0, 0, 0)),
                  pl.BlockSpec(memory_space=pl.ANY)],
        out_specs=pl.BlockSpec((tt, 1, PEER_SEL), lambda i: (i, 0, 0)),
        out_shape=jax.ShapeDtypeStruct((rows - first, 1, PEER_SEL), F32),
        scratch_shapes=[pltpu.VMEM(tab.shape, BF16),
                        pltpu.VMEM((PEER_SEL * HALF_TILE, LANES), F32),
                        pltpu.VMEM((PEER_SEL * HALF_TILE, LANES), F32),
                        pltpu.SemaphoreType.DMA],
        compiler_params=pltpu.CompilerParams(dimension_semantics=("arbitrary",),
                                             vmem_limit_bytes=VMEM_LIMIT_TABLE),
        name="peer_u",
    )(idx, hn3, gate3, tab)


def _peer_v_kernel(idx_ref, act_ref, h1_ref, tab_hbm, o_ref, tab_ref, tiles_ref, *rest, tt):
    rep_refs, sem = rest[:-1], rest[-1]
    _load_table(tab_hbm, tab_ref, sem)
    n_acc = 2
    g = PEER_V_GROUP

    def replicated(t):
        return jnp.broadcast_to(act_ref[t], (PEER_SEL, LANES)).T

    def accumulate(t0):
        rows = [idx_ref.at[t0 + j] for j in range(g)]
        accs = [[jnp.zeros((SUBLANES, LANES), F32)] * n_acc for _ in range(g)]
        for k in range(PEER_SEL):
            for j, rep_ref in enumerate(rep_refs):
                w = jnp.broadcast_to(rep_ref[k:k + 1, :], (SUBLANES, LANES))
                accs[j][k % n_acc] = accs[j][k % n_acc] + w * tab_ref[rows[j][k]].astype(F32)
        for j in range(g):
            tiles_ref[t0 + j] = h1_ref[t0 + j] + sum(accs[j][1:], accs[j][0])

    for j, rep_ref in enumerate(rep_refs):
        rep_ref[...] = replicated(j)

    def group(i, _):
        t = g * i
        nxt = [replicated(jnp.minimum(t + g + j, tt - 1)) for j in range(g)]
        accumulate(t)
        for rep_ref, tile in zip(rep_refs, nxt):
            rep_ref[...] = tile
        return 0

    lax.fori_loop(0, tt // g, group, 0)
    o_ref[...] = tiles_ref[...].reshape(tt, D_MODEL)


def _peer_v(idx, act3, h13, tab, tt, first):
    rows = idx.shape[0]
    b0 = first // tt
    tile = pl.BlockSpec((tt, SUBLANES, LANES), lambda i: (i + b0, 0, 0))
    return pl.pallas_call(
        functools.partial(_peer_v_kernel, tt=tt),
        grid=((rows - first) // tt,),
        in_specs=[pl.BlockSpec((tt, PEER_SEL), lambda i: (i + b0, 0), memory_space=pltpu.SMEM),
                  pl.BlockSpec((tt, 1, PEER_SEL), lambda i: (i + b0, 0, 0)),
                  tile, pl.BlockSpec(memory_space=pl.ANY)],
        out_specs=pl.BlockSpec((tt, D_MODEL), lambda i: (i, 0)),
        out_shape=jax.ShapeDtypeStruct((rows - first, D_MODEL), F32),
        scratch_shapes=[pltpu.VMEM(tab.shape, BF16), pltpu.VMEM((tt, SUBLANES, LANES), F32)]
                       + [pltpu.VMEM((PEER_SEL, LANES), F32)] * PEER_V_GROUP
                       + [pltpu.SemaphoreType.DMA],
        compiler_params=pltpu.CompilerParams(dimension_semantics=("arbitrary",),
                                             vmem_limit_bytes=VMEM_LIMIT_TABLE),
        name="peer_v",
    )(idx, act3, h13, tab)


SC_LANES = 16
SC_WORKERS = 32
SC_ROWS = 32
SC_COLS = 128
SC_ROW_GROUP = 4
SC_SHARE_U = (7, 32)
SC_SHARE_V = (11, 32)


def _peer_v_sc_kernel(tab_hbm, idx_hbm, act_hbm, h1_hbm, out_hbm, idx_v, act_v, rows0_v, rows1_v, acc_v, sem0, sem1,
                      *, n_tok):
    wid = lax.axis_index("s") * 2 + lax.axis_index("c")
    bufs = ((rows0_v, sem0), (rows1_v, sem1))
    n_chunks = PEER_SEL // SC_ROWS

    def gather(c):
        rows_v, sem = bufs[c % 2]
        return pltpu.make_async_copy(tab_hbm.at[idx_v.at[pl.ds(c * SC_ROWS, SC_ROWS)]], rows_v, sem)

    @pl.loop(0, n_tok)
    def _(i):
        t = wid * n_tok + i
        pltpu.sync_copy(idx_hbm.at[pl.ds(t * PEER_SEL, PEER_SEL)], idx_v)
        gather(0).start()
        pltpu.sync_copy(act_hbm.at[pl.ds(t * PEER_SEL, PEER_SEL)], act_v)
        pltpu.sync_copy(h1_hbm.at[t], acc_v)
        for c in range(n_chunks):
            if c + 1 < n_chunks:
                gather(c + 1).start()
            gather(c).wait()
            rows_v = bufs[c % 2][0]

            @pl.loop(0, D_MODEL // SC_COLS)
            def _(cb):
                cols = [pl.ds(cb * SC_COLS + j * SC_LANES, SC_LANES) for j in range(SC_COLS // SC_LANES)]

                @pl.loop(0, SC_ROWS // SC_ROW_GROUP)
                def _(rg):
                    accs = [acc_v[col] for col in cols]
                    for rr in range(SC_ROW_GROUP):
                        r = rg * SC_ROW_GROUP + rr
                        w = plsc.load_gather(act_v, [jnp.full((SC_LANES,), c * SC_ROWS, I32) + r])
                        accs = [a + w * rows_v[r, col] for a, col in zip(accs, cols)]
                    for a, col in zip(accs, cols):
                        acc_v[col] = a
        pltpu.sync_copy(acc_v, out_hbm.at[t])


def _peer_v_sc(tab, idx_flat, act_flat, h1_rows):
    n = h1_rows.shape[0]
    mesh = plsc.VectorSubcoreMesh(core_axis_name="c", subcore_axis_name="s")
    return pl.kernel(
        functools.partial(_peer_v_sc_kernel, n_tok=n // SC_WORKERS),
        out_type=jax.ShapeDtypeStruct((n, D_MODEL), F32),
        mesh=mesh,
        scratch_types=[pltpu.VMEM((PEER_SEL,), I32), pltpu.VMEM((PEER_SEL,), F32),
                       pltpu.VMEM((SC_ROWS, D_MODEL), F32), pltpu.VMEM((SC_ROWS, D_MODEL), F32),
                       pltpu.VMEM((D_MODEL,), F32), pltpu.SemaphoreType.DMA, pltpu.SemaphoreType.DMA],
        compiler_params=pltpu.CompilerParams(needs_layout_passes=False),
        name="peer_v_sc",
    )(tab, idx_flat, act_flat, h1_rows)


def _peer_u_sc_kernel(tab_hbm, idx_hbm, hn_hbm, part_hbm, idx_v, h_v, rows0_v, rows1_v, part_v, sem0, sem1, *, n_tok):
    wid = lax.axis_index("s") * 2 + lax.axis_index("c")
    bufs = ((rows0_v, sem0), (rows1_v, sem1))
    n_chunks = PEER_SEL // SC_ROWS
    part_w = PEER_SEL * SC_LANES

    def gather(c):
        rows_v, sem = bufs[c % 2]
        return pltpu.make_async_copy(tab_hbm.at[idx_v.at[pl.ds(c * SC_ROWS, SC_ROWS)]], rows_v, sem)

    @pl.loop(0, n_tok)
    def _(i):
        t = wid * n_tok + i
        pltpu.sync_copy(idx_hbm.at[pl.ds(t * PEER_SEL, PEER_SEL)], idx_v)
        gather(0).start()
        pltpu.sync_copy(hn_hbm.at[t], h_v)

        @pl.loop(0, PEER_SEL)
        def _(k):
            part_v[pl.ds(k * SC_LANES, SC_LANES)] = jnp.zeros((SC_LANES,), F32)

        for c in range(n_chunks):
            if c + 1 < n_chunks:
                gather(c + 1).start()
            gather(c).wait()
            rows_v = bufs[c % 2][0]

            @pl.loop(0, D_MODEL // SC_COLS)
            def _(cb):
                cols = [pl.ds(cb * SC_COLS + j * SC_LANES, SC_LANES) for j in range(SC_COLS // SC_LANES)]
                hs = [h_v[col] for col in cols]

                @pl.loop(0, SC_ROWS // SC_ROW_GROUP)
                def _(rg):
                    for rr in range(SC_ROW_GROUP):
                        r = rg * SC_ROW_GROUP + rr
                        slot = pl.ds((c * SC_ROWS + r) * SC_LANES, SC_LANES)
                        prods = [h * rows_v[r, col] for h, col in zip(hs, cols)]
                        while len(prods) > 1:
                            prods = [a + b for a, b in zip(prods[0::2], prods[1::2])]
                        part_v[slot] = part_v[slot] + prods[0]
        pltpu.sync_copy(part_v, part_hbm.at[pl.ds(t * part_w, part_w)])


def _peer_u_sc(tab, idx_flat, hn_rows):
    n = hn_rows.shape[0]
    mesh = plsc.VectorSubcoreMesh(core_axis_name="c", subcore_axis_name="s")
    return pl.kernel(
        functools.partial(_peer_u_sc_kernel, n_tok=n // SC_WORKERS),
        out_type=jax.ShapeDtypeStruct((n * PEER_SEL * SC_LANES,), F32),
        mesh=mesh,
        scratch_types=[pltpu.VMEM((PEER_SEL,), I32), pltpu.VMEM((D_MODEL,), F32),
                       pltpu.VMEM((SC_ROWS, D_MODEL), F32), pltpu.VMEM((SC_ROWS, D_MODEL), F32),
                       pltpu.VMEM((PEER_SEL * SC_LANES,), F32), pltpu.SemaphoreType.DMA, pltpu.SemaphoreType.DMA],
        compiler_params=pltpu.CompilerParams(needs_layout_passes=False),
        name="peer_u_sc",
    )(tab, idx_flat, hn_rows)


def _fold_act_kernel(part_ref, fold_ref, gate_ref, act_ref):
    p = part_ref[...]
    hi = p.astype(BF16)
    lo = (p - hi.astype(F32)).astype(BF16)
    dots = _dot(hi, fold_ref[...]) + _dot(lo, fold_ref[...])
    act_ref[...] = _gelu(dots) * gate_ref[...]


def _fold_act(part2, gate, tm):
    n = part2.shape[0]
    fold = (jnp.arange(PEER_SEL * SC_LANES)[:, None] // SC_LANES == jnp.arange(PEER_SEL)[None, :]).astype(BF16)
    return pl.pallas_call(
        _fold_act_kernel,
        grid=(n // tm,),
        in_specs=[pl.BlockSpec((tm, PEER_SEL * SC_LANES), lambda i: (i, 0)),
                  pl.BlockSpec(fold.shape, lambda i: (0, 0)),
                  pl.BlockSpec((tm, PEER_SEL), lambda i: (i, 0))],
        out_specs=pl.BlockSpec((tm, PEER_SEL), lambda i: (i, 0)),
        out_shape=jax.ShapeDtypeStruct((n, PEER_SEL), F32),
        compiler_params=pltpu.CompilerParams(dimension_semantics=("arbitrary",)),
        name="fold_act",
    )(part2, fold, gate)


def _block(x2, meta_tokens, norm1_g, w_in, q_norm_g, k_norm_g, attn_sinks, ssm_a_re, ssm_a_im, ssm_log_dt,
           ssm_b_re, ssm_b_im, ssm_c_re, ssm_c_im, ssm_d, ssm_glu_w, ssm_glu_b, attn_out_g, ssm_out_g, w_out,
           norm2_g, peer_w_query, peer_sub_keys, peer_u, peer_v, *, tm_proj, tm_ssm, tm_mix, tt_peer):
    rows = x2.shape[0]
    row2 = lambda a: a.reshape(1, -1).astype(F32)
    w_in_bf = w_in.astype(BF16)
    seg = jnp.arange(QK_WIDTH) // HEAD_DIM
    ones_bd = (seg[:, None] == seg[None, :]).astype(BF16)
    scale = HEAD_DIM ** -0.5
    qkg = jnp.concatenate([jnp.tile(q_norm_g.astype(F32), N_Q_HEADS) * scale,
                           jnp.tile(k_norm_g.astype(F32), N_KV_HEADS)]).reshape(1, QK_WIDTH)
    bmat, cmat, coef = _ssm_params(ssm_a_re.astype(F32), ssm_a_im.astype(F32), ssm_log_dt.astype(F32),
                                   ssm_b_re.astype(F32), ssm_b_im.astype(F32), ssm_c_re.astype(F32),
                                   ssm_c_im.astype(F32))
    glu_w_bf = ssm_glu_w.astype(BF16)
    ssm_args = (row2(ssm_d), glu_w_bf, row2(ssm_glu_b), row2(ssm_out_g))

    h0 = jnp.concatenate([jnp.zeros((BLOCK - N_META, D_MODEL), F32), meta_tokens.astype(F32)], axis=0)
    _, k0, vt0, u0 = _in_proj(h0, row2(norm1_g), w_in_bf, ones_bd, qkg, BLOCK)
    carry0 = jnp.zeros((2, 2, SUBLANES, SSM_HALF_STATES), F32)
    _, carry = _ssm(u0, bmat, cmat, coef, carry0, *ssm_args, BLOCK)

    qt, k, vt, u = _in_proj(x2, row2(norm1_g), w_in_bf, ones_bd, qkg, tm_proj)
    g_attn = jnp.broadcast_to(attn_out_g.astype(F32)[:, None], (ATTN_WIDTH, BLOCK))
    attn_n = _attention(attn_sinks.astype(F32), qt, k, vt, k0[BLOCK - N_META:], vt0[:, BLOCK - N_META:], g_attn)
    ssm_n, _ = _ssm(u, bmat, cmat, coef, carry, *ssm_args, tm_ssm)

    h1, hn, idx, gate = _mix(x2, attn_n, ssm_n, w_out.astype(BF16), row2(norm2_g), peer_w_query.astype(BF16),
                             peer_sub_keys.astype(BF16), tm_mix)
    unit = math.lcm(SC_WORKERS, tt_peer, tm_mix)
    share = lambda num, den: rows * num // den // unit * unit
    n_u, n_v = share(*SC_SHARE_U), share(*SC_SHARE_V)
    part = _peer_u_sc(peer_u.astype(F32), idx[:n_u].reshape(-1), hn[:n_u].reshape(n_u, D_MODEL))
    act_tc = _peer_u(idx, hn, gate.reshape(rows, 1, PEER_SEL), _expert_tiles(peer_u), tt_peer, n_u)
    act_sc = _fold_act(part.reshape(n_u, PEER_SEL * SC_LANES), gate[:n_u], tm_mix)
    act = jnp.concatenate([act_sc.reshape(n_u, 1, PEER_SEL), act_tc], axis=0)
    sc_out = _peer_v_sc(peer_v.astype(F32), idx[:n_v].reshape(-1), act[:n_v].reshape(-1),
                        h1[:n_v].reshape(n_v, D_MODEL))
    tc_out = _peer_v(idx, act, h1, _expert_tiles(peer_v), tt_peer, n_v)
    return jnp.concatenate([sc_out, tc_out], axis=0)


def kernel(x, meta_tokens, norm1_g, w_in, q_norm_g, k_norm_g, attn_sinks, ssm_a_re, ssm_a_im, ssm_log_dt, ssm_b_re, ssm_b_im, ssm_c_re, ssm_c_im, ssm_d, ssm_glu_w, ssm_glu_b, attn_out_g, ssm_out_g, w_out, norm2_g, peer_w_query, peer_sub_keys, peer_u, peer_v):
    b, seq, d = x.shape
    outs = []
    for bi in range(b):
        outs.append(_block(x[bi].astype(F32), meta_tokens, norm1_g[0], w_in[0], q_norm_g[0], k_norm_g[0],
                           attn_sinks[0], ssm_a_re[0], ssm_a_im[0], ssm_log_dt[0], ssm_b_re[0], ssm_b_im[0],
                           ssm_c_re[0], ssm_c_im[0], ssm_d[0], ssm_glu_w[0], ssm_glu_b[0], attn_out_g[0],
                           ssm_out_g[0], w_out[0], norm2_g[0], peer_w_query[0], peer_sub_keys[0], peer_u[0],
                           peer_v[0], tm_proj=512, tm_ssm=256, tm_mix=256, tt_peer=32))
    return jnp.stack(outs).astype(x.dtype)
```

```python
import functools
import math

import jax
import jax.numpy as jnp
from jax import lax
from jax.experimental import pallas as pl
from jax.experimental.pallas import tpu as pltpu
from jax.experimental.pallas import tpu_sc as plsc

F32 = jnp.float32
BF16 = jnp.bfloat16
I32 = jnp.int32

D_MODEL = 1024
N_META = 16
BLOCK = 128
ATTN_WIDTH = 512
HEAD_DIM = 64
N_Q_HEADS = 8
N_KV_HEADS = 2
GQA = 4
KV_WIDTH = 128
QK_WIDTH = ATTN_WIDTH + KV_WIDTH
SSM_WIDTH = 512
SSM_CH = 16
SSM_GROUPS = 32
SSM_STATE = 64
IN_WIDTH = ATTN_WIDTH + 2 * KV_WIDTH + SSM_WIDTH
PEER_HEADS = 8
PEER_N_KEYS = 128
PEER_TOPK = 16
PEER_DK = 256
PEER_HALF = 128
PEER_SEL = PEER_HEADS * PEER_TOPK
NORM_EPS = 1e-6
MASK_VALUE = -1e30

SUBLANES = 8
LANES = 128
SSM_HALF_W = SSM_WIDTH // 2
SSM_HALF_STATES = SSM_GROUPS // 2 * SSM_STATE
HALF_TILE = SUBLANES // 2
PEER_V_GROUP = 2
VMEM_LIMIT_TABLE = 52 * 1024 * 1024


def _dot(a, b):
    return jnp.dot(a, b, preferred_element_type=F32)


def _dot_nt(a, b):
    return lax.dot_general(a, b, (((1,), (1,)), ((), ())), preferred_element_type=F32)


def _gelu(x):
    return 0.5 * x * (1.0 + lax.erf(x * (2.0 ** -0.5)))


def _in_proj_kernel(x_ref, g1_ref, w_ref, ones_ref, qkg_ref, qt_ref, k_ref, vt_ref, u_ref):
    x = x_ref[...]
    ms = jnp.mean(x * x, axis=-1, keepdims=True)
    y = (x * lax.rsqrt(ms + NORM_EPS) * g1_ref[...]).astype(BF16)
    proj = _dot(y, w_ref[...])
    qk = proj[:, :QK_WIDTH]
    ss = _dot((qk * qk).astype(BF16), ones_ref[...]) * (1.0 / HEAD_DIM)
    qkn = qk * lax.rsqrt(ss + NORM_EPS) * qkg_ref[...]
    qt_ref[...] = qkn[:, :ATTN_WIDTH].T.astype(BF16)
    k_ref[...] = qkn[:, ATTN_WIDTH:].astype(BF16)
    vt_ref[...] = proj[:, QK_WIDTH:QK_WIDTH + KV_WIDTH].T.astype(BF16)
    u_ref[...] = proj[:, QK_WIDTH + KV_WIDTH:]


def _in_proj(x, g1, w_in_bf, ones_bd, qkg, tm):
    rows = x.shape[0]
    row_spec = lambda w: pl.BlockSpec((tm, w), lambda i: (i, 0))
    col_spec = lambda w: pl.BlockSpec((w, tm), lambda i: (0, i))
    full = lambda a: pl.BlockSpec(a.shape, lambda i: (0,) * a.ndim)
    return pl.pallas_call(
        _in_proj_kernel,
        grid=(rows // tm,),
        in_specs=[row_spec(D_MODEL), full(g1), full(w_in_bf), full(ones_bd), full(qkg)],
        out_specs=[col_spec(ATTN_WIDTH), row_spec(KV_WIDTH), col_spec(KV_WIDTH), row_spec(SSM_WIDTH)],
        out_shape=[jax.ShapeDtypeStruct((ATTN_WIDTH, rows), BF16),
                   jax.ShapeDtypeStruct((rows, KV_WIDTH), BF16),
                   jax.ShapeDtypeStruct((KV_WIDTH, rows), BF16),
                   jax.ShapeDtypeStruct((rows, SSM_WIDTH), F32)],
        compiler_params=pltpu.CompilerParams(dimension_semantics=("arbitrary",)),
        name="in_proj",
    )(x, g1, w_in_bf, ones_bd, qkg)


def _attn_kernel(sink_ref, qt_ref, kp_ref, kc_ref, vtp_ref, vtc_ref, km_ref, vtm_ref, g_ref, o_ref, acc_ref):
    i = pl.program_id(0)
    group_w = GQA * BLOCK
    key = lax.broadcasted_iota(I32, (BLOCK, group_w), 0)
    qry = lax.broadcasted_iota(I32, (BLOCK, group_w), 1) & (BLOCK - 1)
    prev_ok = jnp.logical_and(key > qry, i > 0)
    cur_ok = key <= qry
    for kv in range(N_KV_HEADS):
        hs = slice(kv * HEAD_DIM, (kv + 1) * HEAD_DIM)
        heads = range(kv * GQA, (kv + 1) * GQA)
        qt = jnp.concatenate([qt_ref[h * HEAD_DIM:(h + 1) * HEAD_DIM, :] for h in heads], axis=1)
        sink = jnp.concatenate([jnp.full((1, BLOCK), sink_ref[h], F32) for h in heads], axis=1)
        sp = jnp.where(prev_ok, _dot(kp_ref[:, hs], qt), MASK_VALUE)
        sc = jnp.where(cur_ok, _dot(kc_ref[:, hs], qt), MASK_VALUE)
        sm = _dot(km_ref[:, hs], qt)
        m = jnp.maximum(jnp.maximum(jnp.max(sp, axis=0, keepdims=True), jnp.max(sc, axis=0, keepdims=True)),
                        jnp.maximum(jnp.max(sm, axis=0, keepdims=True), sink))
        pp, pc, pm = jnp.exp(sp - m), jnp.exp(sc - m), jnp.exp(sm - m)
        denom = (jnp.sum(pp, axis=0, keepdims=True) + jnp.sum(pc, axis=0, keepdims=True)
                 + jnp.sum(pm, axis=0, keepdims=True) + jnp.exp(sink - m))
        ot = (_dot(vtp_ref[hs, :], pp.astype(BF16)) + _dot(vtc_ref[hs, :], pc.astype(BF16))
              + _dot(vtm_ref[hs, :], pm.astype(BF16))) / denom
        for g, h in enumerate(heads):
            acc_ref[h * HEAD_DIM:(h + 1) * HEAD_DIM, :] = ot[:, g * BLOCK:(g + 1) * BLOCK]
    a = acc_ref[...]
    ms = jnp.mean(a * a, axis=0, keepdims=True)
    o_ref[...] = (a * lax.rsqrt(ms + NORM_EPS) * g_ref[...]).T.astype(BF16)


def _attention(sinks, qt, k, vt, k_meta, vt_meta, g_out):
    nb = k.shape[0] // BLOCK
    prev_i = lambda i: jnp.maximum(i - 1, 0)
    full = lambda a: pl.BlockSpec(a.shape, lambda i: (0,) * a.ndim)
    return pl.pallas_call(
        _attn_kernel,
        grid=(nb,),
        in_specs=[pl.BlockSpec(memory_space=pltpu.SMEM),
                  pl.BlockSpec((ATTN_WIDTH, BLOCK), lambda i: (0, i)),
                  pl.BlockSpec((BLOCK, KV_WIDTH), lambda i: (prev_i(i), 0)),
                  pl.BlockSpec((BLOCK, KV_WIDTH), lambda i: (i, 0)),
                  pl.BlockSpec((KV_WIDTH, BLOCK), lambda i: (0, prev_i(i))),
                  pl.BlockSpec((KV_WIDTH, BLOCK), lambda i: (0, i)),
                  full(k_meta), full(vt_meta), full(g_out)],
        out_specs=pl.BlockSpec((BLOCK, ATTN_WIDTH), lambda i: (i, 0)),
        out_shape=jax.ShapeDtypeStruct((k.shape[0], ATTN_WIDTH), BF16),
        scratch_shapes=[pltpu.VMEM((ATTN_WIDTH, BLOCK), F32)],
        compiler_params=pltpu.CompilerParams(dimension_semantics=("arbitrary",)),
        name="attention",
    )(sinks, qt, k, k, vt, vt, k_meta, vt_meta, g_out)


def _ssm_kernel(u_ref, bmat_ref, cmat_ref, coef_ref, carry_in_ref, d_ref, gw_ref, gb_ref, g_ref,
                o_ref, carry_out_ref, st_ref, carry_ref, y_ref, *, tm):
    @pl.when(pl.program_id(0) == 0)
    def _():
        carry_ref[...] = carry_in_ref[...]

    ns = SSM_HALF_STATES
    for h in range(2):
        uh = u_ref[:, h * SSM_HALF_W:(h + 1) * SSM_HALF_W].astype(BF16)
        st_ref[...] = _dot(uh, bmat_ref[h])

        def block(b, carry):
            cr, ci = carry
            r0 = pl.multiple_of(b * SUBLANES, SUBLANES)
            xr = st_ref[pl.ds(r0, SUBLANES), 0:ns]
            xi = st_ref[pl.ds(r0, SUBLANES), ns:2 * ns]
            for j, shift in enumerate((1, 2, 4)):
                ar, ai = coef_ref[h, j, 0], coef_ref[h, j, 1]
                sr, si = pltpu.roll(xr, shift, 0), pltpu.roll(xi, shift, 0)
                xr, xi = xr + ar * sr - ai * si, xi + ar * si + ai * sr
            pr, pi = coef_ref[h, 3, 0], coef_ref[h, 3, 1]
            xr, xi = xr + pr * cr - pi * ci, xi + pr * ci + pi * cr
            st_ref[pl.ds(r0, SUBLANES), 0:ns] = xr
            st_ref[pl.ds(r0, SUBLANES), ns:2 * ns] = xi
            return (jnp.broadcast_to(xr[SUBLANES - 1:SUBLANES, :], (SUBLANES, ns)),
                    jnp.broadcast_to(xi[SUBLANES - 1:SUBLANES, :], (SUBLANES, ns)))

        cr, ci = lax.fori_loop(0, tm // SUBLANES, block, (carry_ref[h, 0], carry_ref[h, 1]))
        carry_ref[h, 0] = cr
        carry_ref[h, 1] = ci
        y_ref[:, h * SSM_HALF_W:(h + 1) * SSM_HALF_W] = _dot(st_ref[...].astype(BF16), cmat_ref[h])

    carry_out_ref[...] = carry_ref[...]
    z = _gelu(y_ref[...] + d_ref[...] * u_ref[...])
    z = z * jax.nn.sigmoid(_dot(z.astype(BF16), gw_ref[...]) + gb_ref[...])
    ms = jnp.mean(z * z, axis=-1, keepdims=True)
    o_ref[...] = (z * lax.rsqrt(ms + NORM_EPS) * g_ref[...]).astype(BF16)


def _ssm(u, bmat, cmat, coef, carry_in, d, gw_bf, gb, g_out, tm):
    rows = u.shape[0]
    row_spec = pl.BlockSpec((tm, SSM_WIDTH), lambda i: (i, 0))
    full = lambda a: pl.BlockSpec(a.shape, lambda i: (0,) * a.ndim)
    return pl.pallas_call(
        functools.partial(_ssm_kernel, tm=tm),
        grid=(rows // tm,),
        in_specs=[row_spec, full(bmat), full(cmat), full(coef), full(carry_in), full(d), full(gw_bf), full(gb),
                  full(g_out)],
        out_specs=[row_spec, full(carry_in)],
        out_shape=[jax.ShapeDtypeStruct((rows, SSM_WIDTH), BF16),
                   jax.ShapeDtypeStruct(carry_in.shape, F32)],
        scratch_shapes=[pltpu.VMEM((tm, 2 * SSM_HALF_STATES), F32),
                        pltpu.VMEM(carry_in.shape, F32),
                        pltpu.VMEM((tm, SSM_WIDTH), F32)],
        compiler_params=pltpu.CompilerParams(dimension_semantics=("arbitrary",)),
        name="ssm",
    )(u, bmat, cmat, coef, carry_in, d, gw_bf, gb, g_out)


def _ssm_params(a_re, a_im, log_dt, b_re, b_im, c_re, c_im):
    dt = jnp.exp(log_dt)[:, None]
    mag = jnp.exp(a_re * dt)
    abar_r, abar_i = mag * jnp.cos(a_im * dt), mag * jnp.sin(a_im * dt)
    den = a_re * a_re + a_im * a_im
    nr, ni = abar_r - 1.0, abar_i
    coef_r = (nr * a_re + ni * a_im) / den
    coef_i = (ni * a_re - nr * a_im) / den
    bbar_r = coef_r[..., None] * b_re - coef_i[..., None] * b_im
    bbar_i = coef_r[..., None] * b_im + coef_i[..., None] * b_re
    gh = SSM_GROUPS // 2
    eye = jnp.eye(gh, dtype=F32)

    def half_b(bb):
        return jnp.einsum('gnp,gk->gpkn', bb, eye).reshape(gh * SSM_CH, gh * SSM_STATE)

    def half_c(cc):
        return jnp.einsum('gpn,gk->gnkp', cc, eye).reshape(gh * SSM_STATE, gh * SSM_CH)

    bmat = jnp.stack([jnp.concatenate([half_b(bbar_r[s]), half_b(bbar_i[s])], axis=1)
                      for s in (slice(0, gh), slice(gh, None))]).astype(BF16)
    cmat = jnp.stack([jnp.concatenate([half_c(c_re[s]), -half_c(c_im[s])], axis=0)
                      for s in (slice(0, gh), slice(gh, None))]).astype(BF16)

    def cpow(k):
        m = mag ** k
        return m * jnp.cos(a_im * dt * k), m * jnp.sin(a_im * dt * k)

    rows = jnp.arange(SUBLANES)
    coefs = []
    for shift in (1, 2, 4):
        pr, pi = cpow(float(shift))
        keep = (rows >= shift).astype(F32)[:, None, None]
        coefs.append(jnp.stack([keep * pr[None], keep * pi[None]]))
    pw = (rows + 1).astype(F32)[:, None, None]
    m = mag[None] ** pw
    coefs.append(jnp.stack([m * jnp.cos(a_im[None] * dt[None] * pw), m * jnp.sin(a_im[None] * dt[None] * pw)]))
    coef = jnp.stack(coefs)
    coef = coef.reshape(4, 2, SUBLANES, 2, SSM_HALF_STATES).transpose(3, 0, 1, 2, 4)
    return bmat, cmat, coef


def _top16(s, payload, n_rows):
    rowid = lax.broadcasted_iota(I32, s.shape, 0).astype(F32)
    vals, pays = [], []
    for _ in range(PEER_TOPK):
        m = jnp.max(s, axis=0, keepdims=True)
        pos = jnp.min(jnp.where(s == m, rowid, float(n_rows)), axis=0, keepdims=True)
        sel = rowid == pos
        vals.append(m)
        pays.append(pos if payload is None else jnp.max(jnp.where(sel, payload, -1.0), axis=0, keepdims=True))
        s = jnp.where(sel, -jnp.inf, s)
    return jnp.concatenate(vals, axis=0), jnp.concatenate(pays, axis=0)


_CAND_PAIRS = [(a, b) for a in range(PEER_TOPK) for b in range(PEER_TOPK) if (a + 1) * (b + 1) <= PEER_TOPK]
_CAND_ROWS = -(-len(_CAND_PAIRS) // SUBLANES) * SUBLANES


def _mix_kernel(x_ref, a_ref, s_ref, wo_ref, g2_ref, wq_ref, keys_ref, h1_ref, hn_ref, idx_ref, gate_ref,
                qp_ref, idx_t_ref, gate_t_ref, *, tm):
    mix = _dot(a_ref[...], wo_ref[0:ATTN_WIDTH, :]) + _dot(s_ref[...], wo_ref[ATTN_WIDTH:, :])
    h1 = x_ref[...] + mix
    h1_ref[...] = h1.reshape(tm, SUBLANES, LANES)
    ms = jnp.mean(h1 * h1, axis=-1, keepdims=True)
    hn = h1 * lax.rsqrt(ms + NORM_EPS) * g2_ref[...]
    hn_ref[...] = hn.reshape(tm, SUBLANES, LANES)
    qp_ref[...] = _dot(hn.astype(BF16), wq_ref[...])

    def retrieve(h, tok):
        c0 = pl.multiple_of(h * PEER_DK, PEER_DK)
        q1 = qp_ref[tok, pl.ds(c0, PEER_HALF)].astype(BF16)
        q2 = qp_ref[tok, pl.ds(c0 + PEER_HALF, PEER_HALF)].astype(BF16)
        s1 = _dot_nt(keys_ref[h, 0], q1)
        s2 = _dot_nt(keys_ref[h, 1], q2)
        v1, i1 = _top16(s1, None, PEER_N_KEYS)
        v2, i2 = _top16(s2, None, PEER_N_KEYS)
        pad = _CAND_ROWS - len(_CAND_PAIRS)
        cand = jnp.concatenate([v1[a:a + 1, :] + v2[b:b + 1, :] for a, b in _CAND_PAIRS]
                               + [jnp.full((pad, LANES), -jnp.inf, F32)], axis=0)
        eidx = jnp.concatenate([i1[a:a + 1, :] * float(PEER_N_KEYS) + i2[b:b + 1, :] for a, b in _CAND_PAIRS]
                               + [jnp.zeros((pad, LANES), F32)], axis=0)
        sc, ei = _top16(cand, eidx, _CAND_ROWS)
        p = jnp.exp(sc - sc[0:1, :])
        r0 = pl.multiple_of(h * PEER_TOPK, PEER_TOPK)
        idx_t_ref[pl.ds(r0, PEER_TOPK), tok] = ei.astype(I32)
        gate_t_ref[pl.ds(r0, PEER_TOPK), tok] = p / jnp.sum(p, axis=0, keepdims=True)

    def head_pair(i, _):
        for h in (2 * i, 2 * i + 1):
            for sub in range(tm // LANES):
                retrieve(h, slice(sub * LANES, (sub + 1) * LANES))
        return 0

    lax.fori_loop(0, PEER_HEADS // 2, head_pair, 0)
    idx_ref[...] = idx_t_ref[...].T
    gate_ref[...] = gate_t_ref[...].T


def _mix(x, attn_n, ssm_n, w_out_bf, g2, wq_bf, keys_bf, tm):
    rows = x.shape[0]
    row_spec = lambda w: pl.BlockSpec((tm, w), lambda i: (i, 0))
    tile_spec = pl.BlockSpec((tm, SUBLANES, LANES), lambda i: (i, 0, 0))
    full = lambda a: pl.BlockSpec(a.shape, lambda i: (0,) * a.ndim)
    return pl.pallas_call(
        functools.partial(_mix_kernel, tm=tm),
        grid=(rows // tm,),
        in_specs=[row_spec(D_MODEL), row_spec(ATTN_WIDTH), row_spec(SSM_WIDTH), full(w_out_bf), full(g2),
                  full(wq_bf), full(keys_bf)],
        out_specs=[tile_spec, tile_spec, row_spec(PEER_SEL), row_spec(PEER_SEL)],
        out_shape=[jax.ShapeDtypeStruct((rows, SUBLANES, LANES), F32),
                   jax.ShapeDtypeStruct((rows, SUBLANES, LANES), F32),
                   jax.ShapeDtypeStruct((rows, PEER_SEL), I32),
                   jax.ShapeDtypeStruct((rows, PEER_SEL), F32)],
        scratch_shapes=[pltpu.VMEM((tm, PEER_HEADS * PEER_DK), F32),
                        pltpu.VMEM((PEER_SEL, tm), I32),
                        pltpu.VMEM((PEER_SEL, tm), F32)],
        compiler_params=pltpu.CompilerParams(dimension_semantics=("arbitrary",),
                                             vmem_limit_bytes=48 * 1024 * 1024),
        name="mix_topk",
    )(x, attn_n, ssm_n, w_out_bf, g2, wq_bf, keys_bf)


def _expert_tiles(t):
    return t.astype(BF16).reshape(t.shape[0], SUBLANES, LANES)


def _load_table(tab_hbm, tab_ref, sem):
    @pl.when(pl.program_id(0) == 0)
    def _():
        cp = pltpu.make_async_copy(tab_hbm, tab_ref, sem)
        cp.start()
        cp.wait()


def _peer_u_kernel(idx_ref, h_ref, gate_ref, tab_hbm, act_ref, tab_ref, part_a_ref, part_b_ref, sem, *, tt):
    _load_table(tab_hbm, tab_ref, sem)

    def products(t0, part_refs):
        toks = [(idx_ref.at[t0 + j], h_ref[t0 + j], part_refs[j]) for j in range(len(part_refs))]
        for k in range(PEER_SEL):
            for row, hv, part_ref in toks:
                p = tab_ref[row[k]].astype(F32) * hv
                part_ref[k * HALF_TILE:(k + 1) * HALF_TILE, :] = p[0:HALF_TILE] + p[HALF_TILE:]

    def row_sums(part_ref):
        part = part_ref[pl.ds(0, PEER_SEL, stride=HALF_TILE), :]
        for r in range(1, HALF_TILE):
            part = part + part_ref[pl.ds(r, PEER_SEL, stride=HALF_TILE), :]
        return part

    def finish(t, part):
        dots = jnp.sum(part.T, axis=0, keepdims=True)
        act_ref[t] = _gelu(dots) * gate_ref[t]

    products(0, (part_a_ref, part_b_ref))

    def pair(i, _):
        t = 2 * i
        sums_a, sums_b = row_sums(part_a_ref), row_sums(part_b_ref)
        finish(t - 2, sums_a)
        finish(t - 1, sums_b)
        products(t, (part_a_ref, part_b_ref))
        return 0

    lax.fori_loop(1, tt // 2, pair, 0)
    finish(tt - 2, row_sums(part_a_ref))
    finish(tt - 1, row_sums(part_b_ref))


def _peer_u(idx, hn3, gate3, tab, tt, first):
    rows = idx.shape[0]
    b0 = first // tt
    return pl.pallas_call(
        functools.partial(_peer_u_kernel, tt=tt),
        grid=((rows - first) // tt,),
        in_specs=[pl.BlockSpec((tt, PEER_SEL), lambda i: (i + b0, 0), memory_space=pltpu.SMEM),
                  pl.BlockSpec((tt, SUBLANES, LANES), lambda i: (i + b0, 0, 0)),
                  pl.BlockSpec((tt, 1, PEER_SEL), lambda i: (i + b0, 0, 0)),
                  pl.BlockSpec(memory_space=pl.ANY)],
        out_specs=pl.BlockSpec((tt, 1, PEER_SEL), lambda i: (i, 0, 0)),
        out_shape=jax.ShapeDtypeStruct((rows - first, 1, PEER_SEL), F32),
        scratch_shapes=[pltpu.VMEM(tab.shape, BF16),
                        pltpu.VMEM((PEER_SEL * HALF_TILE, LANES), F32),
                        pltpu.VMEM((PEER_SEL * HALF_TILE, LANES), F32),
                        pltpu.SemaphoreType.DMA],
        compiler_params=pltpu.CompilerParams(dimension_semantics=("arbitrary",),
                                             vmem_limit_bytes=VMEM_LIMIT_TABLE),
        name="peer_u",
    )(idx, hn3, gate3, tab)


def _peer_v_kernel(idx_ref, act_ref, h1_ref, tab_hbm, o_ref, tab_ref, tiles_ref, *rest, tt):
    rep_refs, sem = rest[:-1], rest[-1]
    _load_table(tab_hbm, tab_ref, sem)
    n_acc = 2
    g = PEER_V_GROUP

    def replicated(t):
        return jnp.broadcast_to(act_ref[t], (PEER_SEL, LANES)).T

    def accumulate(t0):
        rows = [idx_ref.at[t0 + j] for j in range(g)]
        accs = [[jnp.zeros((SUBLANES, LANES), F32)] * n_acc for _ in range(g)]
        for k in range(PEER_SEL):
            for j, rep_ref in enumerate(rep_refs):
                w = jnp.broadcast_to(rep_ref[k:k + 1, :], (SUBLANES, LANES))
                accs[j][k % n_acc] = accs[j][k % n_acc] + w * tab_ref[rows[j][k]].astype(F32)
        for j in range(g):
            tiles_ref[t0 + j] = h1_ref[t0 + j] + sum(accs[j][1:], accs[j][0])

    for j, rep_ref in enumerate(rep_refs):
        rep_ref[...] = replicated(j)

    def group(i, _):
        t = g * i
        nxt = [replicated(jnp.minimum(t + g + j, tt - 1)) for j in range(g)]
        accumulate(t)
        for rep_ref, tile in zip(rep_refs, nxt):
            rep_ref[...] = tile
        return 0

    lax.fori_loop(0, tt // g, group, 0)
    o_ref[...] = tiles_ref[...].reshape(tt, D_MODEL)


def _peer_v(idx, act3, h13, tab, tt, first):
    rows = idx.shape[0]
    b0 = first // tt
    tile = pl.BlockSpec((tt, SUBLANES, LANES), lambda i: (i + b0, 0, 0))
    return pl.pallas_call(
        functools.partial(_peer_v_kernel, tt=tt),
        grid=((rows - first) // tt,),
        in_specs=[pl.BlockSpec((tt, PEER_SEL), lambda i: (i + b0, 0), memory_space=pltpu.SMEM),
                  pl.BlockSpec((tt, 1, PEER_SEL), lambda i: (i + b0, 0, 0)),
                  tile, pl.BlockSpec(memory_space=pl.ANY)],
        out_specs=pl.BlockSpec((tt, D_MODEL), lambda i: (i, 0)),
        out_shape=jax.ShapeDtypeStruct((rows - first, D_MODEL), F32),
        scratch_shapes=[pltpu.VMEM(tab.shape, BF16), pltpu.VMEM((tt, SUBLANES, LANES), F32)]
                       + [pltpu.VMEM((PEER_SEL, LANES), F32)] * PEER_V_GROUP
                       + [pltpu.SemaphoreType.DMA],
        compiler_params=pltpu.CompilerParams(dimension_semantics=("arbitrary",),
                                             vmem_limit_bytes=VMEM_LIMIT_TABLE),
        name="peer_v",
    )(idx, act3, h13, tab)


SC_LANES = 16
SC_WORKERS = 32
SC_ROWS = 32
SC_COLS = 128
SC_COLS_U = 256
SC_ROW_GROUP = 4
SC_SHARE_U = (7, 32)
SC_SHARE_V = (11, 32)


def _peer_v_sc_kernel(tab_hbm, idx_hbm, act_hbm, h1_hbm, out_hbm, idx_v, act_v, rows0_v, rows1_v, acc_v, sem0, sem1,
                      *, n_tok):
    wid = lax.axis_index("s") * 2 + lax.axis_index("c")
    bufs = ((rows0_v, sem0), (rows1_v, sem1))
    n_chunks = PEER_SEL // SC_ROWS

    def gather(c):
        rows_v, sem = bufs[c % 2]
        return pltpu.make_async_copy(tab_hbm.at[idx_v.at[pl.ds(c * SC_ROWS, SC_ROWS)]], rows_v, sem)

    @pl.loop(0, n_tok)
    def _(i):
        t = wid * n_tok + i
        pltpu.sync_copy(idx_hbm.at[pl.ds(t * PEER_SEL, PEER_SEL)], idx_v)
        gather(0).start()
        pltpu.sync_copy(act_hbm.at[pl.ds(t * PEER_SEL, PEER_SEL)], act_v)
        pltpu.sync_copy(h1_hbm.at[t], acc_v)
        for c in range(n_chunks):
            if c + 1 < n_chunks:
                gather(c + 1).start()
            gather(c).wait()
            rows_v = bufs[c % 2][0]

            @pl.loop(0, D_MODEL // SC_COLS)
            def _(cb):
                cols = [pl.ds(cb * SC_COLS + j * SC_LANES, SC_LANES) for j in range(SC_COLS // SC_LANES)]

                @pl.loop(0, SC_ROWS // SC_ROW_GROUP)
                def _(rg):
                    accs = [acc_v[col] for col in cols]
                    for rr in range(SC_ROW_GROUP):
                        r = rg * SC_ROW_GROUP + rr
                        w = plsc.load_gather(act_v, [jnp.full((SC_LANES,), c * SC_ROWS, I32) + r])
                        accs = [a + w * rows_v[r, col] for a, col in zip(accs, cols)]
                    for a, col in zip(accs, cols):
                        acc_v[col] = a
        pltpu.sync_copy(acc_v, out_hbm.at[t])


def _peer_v_sc(tab, idx_flat, act_flat, h1_rows):
    n = h1_rows.shape[0]
    mesh = plsc.VectorSubcoreMesh(core_axis_name="c", subcore_axis_name="s")
    return pl.kernel(
        functools.partial(_peer_v_sc_kernel, n_tok=n // SC_WORKERS),
        out_type=jax.ShapeDtypeStruct((n, D_MODEL), F32),
        mesh=mesh,
        scratch_types=[pltpu.VMEM((PEER_SEL,), I32), pltpu.VMEM((PEER_SEL,), F32),
                       pltpu.VMEM((SC_ROWS, D_MODEL), F32), pltpu.VMEM((SC_ROWS, D_MODEL), F32),
                       pltpu.VMEM((D_MODEL,), F32), pltpu.SemaphoreType.DMA, pltpu.SemaphoreType.DMA],
        compiler_params=pltpu.CompilerParams(needs_layout_passes=False),
        name="peer_v_sc",
    )(tab, idx_flat, act_flat, h1_rows)


def _peer_u_sc_kernel(tab_hbm, idx_hbm, hn_hbm, part_hbm, idx_v, h_v, rows0_v, rows1_v, part_v, sem0, sem1, *, n_tok):
    wid = lax.axis_index("s") * 2 + lax.axis_index("c")
    bufs = ((rows0_v, sem0), (rows1_v, sem1))
    n_chunks = PEER_SEL // SC_ROWS
    part_w = PEER_SEL * SC_LANES

    def gather(c):
        rows_v, sem = bufs[c % 2]
        return pltpu.make_async_copy(tab_hbm.at[idx_v.at[pl.ds(c * SC_ROWS, SC_ROWS)]], rows_v, sem)

    @pl.loop(0, n_tok)
    def _(i):
        t = wid * n_tok + i
        pltpu.sync_copy(idx_hbm.at[pl.ds(t * PEER_SEL, PEER_SEL)], idx_v)
        gather(0).start()
        pltpu.sync_copy(hn_hbm.at[t], h_v)

        @pl.loop(0, PEER_SEL)
        def _(k):
            part_v[pl.ds(k * SC_LANES, SC_LANES)] = jnp.zeros((SC_LANES,), F32)

        for c in range(n_chunks):
            if c + 1 < n_chunks:
                gather(c + 1).start()
            gather(c).wait()
            rows_v = bufs[c % 2][0]

            @pl.loop(0, D_MODEL // SC_COLS_U)
            def _(cb):
                cols = [pl.ds(cb * SC_COLS_U + j * SC_LANES, SC_LANES) for j in range(SC_COLS_U // SC_LANES)]
                hs = [h_v[col] for col in cols]

                @pl.loop(0, SC_ROWS // SC_ROW_GROUP)
                def _(rg):
                    for rr in range(SC_ROW_GROUP):
                        r = rg * SC_ROW_GROUP + rr
                        slot = pl.ds((c * SC_ROWS + r) * SC_LANES, SC_LANES)
                        prods = [h * rows_v[r, col] for h, col in zip(hs, cols)]
                        while len(prods) > 1:
                            prods = [a + b for a, b in zip(prods[0::2], prods[1::2])]
                        part_v[slot] = part_v[slot] + prods[0]
        pltpu.sync_copy(part_v, part_hbm.at[pl.ds(t * part_w, part_w)])


def _peer_u_sc(tab, idx_flat, hn_rows):
    n = hn_rows.shape[0]
    mesh = plsc.VectorSubcoreMesh(core_axis_name="c", subcore_axis_name="s")
    return pl.kernel(
        functools.partial(_peer_u_sc_kernel, n_tok=n // SC_WORKERS),
        out_type=jax.ShapeDtypeStruct((n * PEER_SEL * SC_LANES,), F32),
        mesh=mesh,
        scratch_types=[pltpu.VMEM((PEER_SEL,), I32), pltpu.VMEM((D_MODEL,), F32),
                       pltpu.VMEM((SC_ROWS, D_MODEL), F32), pltpu.VMEM((SC_ROWS, D_MODEL), F32),
                       pltpu.VMEM((PEER_SEL * SC_LANES,), F32), pltpu.SemaphoreType.DMA, pltpu.SemaphoreType.DMA],
        compiler_params=pltpu.CompilerParams(needs_layout_passes=False),
        name="peer_u_sc",
    )(tab, idx_flat, hn_rows)


def _fold_act_kernel(part_ref, fold_ref, gate_ref, act_ref):
    p = part_ref[...]
    hi = p.astype(BF16)
    lo = (p - hi.astype(F32)).astype(BF16)
    dots = _dot(hi, fold_ref[...]) + _dot(lo, fold_ref[...])
    act_ref[...] = _gelu(dots) * gate_ref[...]


def _fold_act(part2, gate, tm):
    n = part2.shape[0]
    fold = (jnp.arange(PEER_SEL * SC_LANES)[:, None] // SC_LANES == jnp.arange(PEER_SEL)[None, :]).astype(BF16)
    return pl.pallas_call(
        _fold_act_kernel,
        grid=(n // tm,),
        in_specs=[pl.BlockSpec((tm, PEER_SEL * SC_LANES), lambda i: (i, 0)),
                  pl.BlockSpec(fold.shape, lambda i: (0, 0)),
                  pl.BlockSpec((tm, PEER_SEL), lambda i: (i, 0))],
        out_specs=pl.BlockSpec((tm, PEER_SEL), lambda i: (i, 0)),
        out_shape=jax.ShapeDtypeStruct((n, PEER_SEL), F32),
        compiler_params=pltpu.CompilerParams(dimension_semantics=("arbitrary",)),
        name="fold_act",
    )(part2, fold, gate)


def _block(x2, meta_tokens, norm1_g, w_in, q_norm_g, k_norm_g, attn_sinks, ssm_a_re, ssm_a_im, ssm_log_dt,
           ssm_b_re, ssm_b_im, ssm_c_re, ssm_c_im, ssm_d, ssm_glu_w, ssm_glu_b, attn_out_g, ssm_out_g, w_out,
           norm2_g, peer_w_query, peer_sub_keys, peer_u, peer_v, *, tm_proj, tm_ssm, tm_mix, tt_peer):
    rows = x2.shape[0]
    row2 = lambda a: a.reshape(1, -1).astype(F32)
    w_in_bf = w_in.astype(BF16)
    seg = jnp.arange(QK_WIDTH) // HEAD_DIM
    ones_bd = (seg[:, None] == seg[None, :]).astype(BF16)
    scale = HEAD_DIM ** -0.5
    qkg = jnp.concatenate([jnp.tile(q_norm_g.astype(F32), N_Q_HEADS) * scale,
                           jnp.tile(k_norm_g.astype(F32), N_KV_HEADS)]).reshape(1, QK_WIDTH)
    bmat, cmat, coef = _ssm_params(ssm_a_re.astype(F32), ssm_a_im.astype(F32), ssm_log_dt.astype(F32),
                                   ssm_b_re.astype(F32), ssm_b_im.astype(F32), ssm_c_re.astype(F32),
                                   ssm_c_im.astype(F32))
    glu_w_bf = ssm_glu_w.astype(BF16)
    ssm_args = (row2(ssm_d), glu_w_bf, row2(ssm_glu_b), row2(ssm_out_g))

    h0 = jnp.concatenate([jnp.zeros((BLOCK - N_META, D_MODEL), F32), meta_tokens.astype(F32)], axis=0)
    _, k0, vt0, u0 = _in_proj(h0, row2(norm1_g), w_in_bf, ones_bd, qkg, BLOCK)
    carry0 = jnp.zeros((2, 2, SUBLANES, SSM_HALF_STATES), F32)
    _, carry = _ssm(u0, bmat, cmat, coef, carry0, *ssm_args, BLOCK)

    qt, k, vt, u = _in_proj(x2, row2(norm1_g), w_in_bf, ones_bd, qkg, tm_proj)
    g_attn = jnp.broadcast_to(attn_out_g.astype(F32)[:, None], (ATTN_WIDTH, BLOCK))
    attn_n = _attention(attn_sinks.astype(F32), qt, k, vt, k0[BLOCK - N_META:], vt0[:, BLOCK - N_META:], g_attn)
    ssm_n, _ = _ssm(u, bmat, cmat, coef, carry, *ssm_args, tm_ssm)

    h1, hn, idx, gate = _mix(x2, attn_n, ssm_n, w_out.astype(BF16), row2(norm2_g), peer_w_query.astype(BF16),
                             peer_sub_keys.astype(BF16), tm_mix)
    unit = math.lcm(SC_WORKERS, tt_peer, tm_mix)
    share = lambda num, den: rows * num // den // unit * unit
    n_u, n_v = share(*SC_SHARE_U), share(*SC_SHARE_V)
    part = _peer_u_sc(peer_u.astype(F32), idx[:n_u].reshape(-1), hn[:n_u].reshape(n_u, D_MODEL))
    act_tc = _peer_u(idx, hn, gate.reshape(rows, 1, PEER_SEL), _expert_tiles(peer_u), tt_peer, n_u)
    act_sc = _fold_act(part.reshape(n_u, PEER_SEL * SC_LANES), gate[:n_u], tm_mix)
    act = jnp.concatenate([act_sc.reshape(n_u, 1, PEER_SEL), act_tc], axis=0)
    sc_out = _peer_v_sc(peer_v.astype(F32), idx[:n_v].reshape(-1), act[:n_v].reshape(-1),
                        h1[:n_v].reshape(n_v, D_MODEL))
    tc_out = _peer_v(idx, act, h1, _expert_tiles(peer_v), tt_peer, n_v)
    return jnp.concatenate([sc_out, tc_out], axis=0)


def kernel(x, meta_tokens, norm1_g, w_in, q_norm_g, k_norm_g, attn_sinks, ssm_a_re, ssm_a_im, ssm_log_dt, ssm_b_re, ssm_b_im, ssm_c_re, ssm_c_im, ssm_d, ssm_glu_w, ssm_glu_b, attn_out_g, ssm_out_g, w_out, norm2_g, peer_w_query, peer_sub_keys, peer_u, peer_v):
    b, seq, d = x.shape
    outs = []
    for bi in range(b):
        outs.append(_block(x[bi].astype(F32), meta_tokens, norm1_g[0], w_in[0], q_norm_g[0], k_norm_g[0],
                           attn_sinks[0], ssm_a_re[0], ssm_a_im[0], ssm_log_dt[0], ssm_b_re[0], ssm_b_im[0],
                           ssm_c_re[0], ssm_c_im[0], ssm_d[0], ssm_glu_w[0], ssm_glu_b[0], attn_out_g[0],
                           ssm_out_g[0], w_out[0], norm2_g[0], peer_w_query[0], peer_sub_keys[0], peer_u[0],
                           peer_v[0], tm_proj=512, tm_ssm=256, tm_mix=256, tt_peer=32))
    return jnp.stack(outs).astype(x.dtype)
```

```python
import functools
import math

import jax
import jax.numpy as jnp
from jax import lax
from jax.experimental import pallas as pl
from jax.experimental.pallas import tpu as pltpu
from jax.experimental.pallas import tpu_sc as plsc

F32 = jnp.float32
BF16 = jnp.bfloat16
I32 = jnp.int32

D_MODEL = 1024
N_META = 16
BLOCK = 128
ATTN_WIDTH = 512
HEAD_DIM = 64
N_Q_HEADS = 8
N_KV_HEADS = 2
GQA = 4
KV_WIDTH = 128
QK_WIDTH = ATTN_WIDTH + KV_WIDTH
SSM_WIDTH = 512
SSM_CH = 16
SSM_GROUPS = 32
SSM_STATE = 64
IN_WIDTH = ATTN_WIDTH + 2 * KV_WIDTH + SSM_WIDTH
PEER_HEADS = 8
PEER_N_KEYS = 128
PEER_TOPK = 16
PEER_DK = 256
PEER_HALF = 128
PEER_SEL = PEER_HEADS * PEER_TOPK
NORM_EPS = 1e-6
MASK_VALUE = -1e30

SUBLANES = 8
LANES = 128
SSM_HALF_W = SSM_WIDTH // 2
SSM_HALF_STATES = SSM_GROUPS // 2 * SSM_STATE
HALF_TILE = SUBLANES // 2
PEER_V_GROUP = 2
VMEM_LIMIT_TABLE = 52 * 1024 * 1024


def _dot(a, b):
    return jnp.dot(a, b, preferred_element_type=F32)


def _dot_nt(a, b):
    return lax.dot_general(a, b, (((1,), (1,)), ((), ())), preferred_element_type=F32)


def _gelu(x):
    return 0.5 * x * (1.0 + lax.erf(x * (2.0 ** -0.5)))


def _in_proj_kernel(x_ref, g1_ref, w_ref, ones_ref, qkg_ref, qt_ref, k_ref, vt_ref, u_ref):
    x = x_ref[...]
    ms = jnp.mean(x * x, axis=-1, keepdims=True)
    y = (x * lax.rsqrt(ms + NORM_EPS) * g1_ref[...]).astype(BF16)
    proj = _dot(y, w_ref[...])
    qk = proj[:, :QK_WIDTH]
    ss = _dot((qk * qk).astype(BF16), ones_ref[...]) * (1.0 / HEAD_DIM)
    qkn = qk * lax.rsqrt(ss + NORM_EPS) * qkg_ref[...]
    qt_ref[...] = qkn[:, :ATTN_WIDTH].T.astype(BF16)
    k_ref[...] = qkn[:, ATTN_WIDTH:].astype(BF16)
    vt_ref[...] = proj[:, QK_WIDTH:QK_WIDTH + KV_WIDTH].T.astype(BF16)
    u_ref[...] = proj[:, QK_WIDTH + KV_WIDTH:]


def _in_proj(x, g1, w_in_bf, ones_bd, qkg, tm):
    rows = x.shape[0]
    row_spec = lambda w: pl.BlockSpec((tm, w), lambda i: (i, 0))
    col_spec = lambda w: pl.BlockSpec((w, tm), lambda i: (0, i))
    full = lambda a: pl.BlockSpec(a.shape, lambda i: (0,) * a.ndim)
    return pl.pallas_call(
        _in_proj_kernel,
        grid=(rows // tm,),
        in_specs=[row_spec(D_MODEL), full(g1), full(w_in_bf), full(ones_bd), full(qkg)],
        out_specs=[col_spec(ATTN_WIDTH), row_spec(KV_WIDTH), col_spec(KV_WIDTH), row_spec(SSM_WIDTH)],
        out_shape=[jax.ShapeDtypeStruct((ATTN_WIDTH, rows), BF16),
                   jax.ShapeDtypeStruct((rows, KV_WIDTH), BF16),
                   jax.ShapeDtypeStruct((KV_WIDTH, rows), BF16),
                   jax.ShapeDtypeStruct((rows, SSM_WIDTH), F32)],
        compiler_params=pltpu.CompilerParams(dimension_semantics=("arbitrary",)),
        name="in_proj",
    )(x, g1, w_in_bf, ones_bd, qkg)


def _attn_kernel(sink_ref, qt_ref, kp_ref, kc_ref, vtp_ref, vtc_ref, km_ref, vtm_ref, g_ref, o_ref, acc_ref):
    i = pl.program_id(0)
    group_w = GQA * BLOCK
    key = lax.broadcasted_iota(I32, (BLOCK, group_w), 0)
    qry = lax.broadcasted_iota(I32, (BLOCK, group_w), 1) & (BLOCK - 1)
    prev_ok = jnp.logical_and(key > qry, i > 0)
    cur_ok = key <= qry
    for kv in range(N_KV_HEADS):
        hs = slice(kv * HEAD_DIM, (kv + 1) * HEAD_DIM)
        heads = range(kv * GQA, (kv + 1) * GQA)
        qt = jnp.concatenate([qt_ref[h * HEAD_DIM:(h + 1) * HEAD_DIM, :] for h in heads], axis=1)
        sink = jnp.concatenate([jnp.full((1, BLOCK), sink_ref[h], F32) for h in heads], axis=1)
        sp = jnp.where(prev_ok, _dot(kp_ref[:, hs], qt), MASK_VALUE)
        sc = jnp.where(cur_ok, _dot(kc_ref[:, hs], qt), MASK_VALUE)
        sm = _dot(km_ref[:, hs], qt)
        m = jnp.maximum(jnp.maximum(jnp.max(sp, axis=0, keepdims=True), jnp.max(sc, axis=0, keepdims=True)),
                        jnp.maximum(jnp.max(sm, axis=0, keepdims=True), sink))
        pp, pc, pm = jnp.exp(sp - m), jnp.exp(sc - m), jnp.exp(sm - m)
        denom = (jnp.sum(pp, axis=0, keepdims=True) + jnp.sum(pc, axis=0, keepdims=True)
                 + jnp.sum(pm, axis=0, keepdims=True) + jnp.exp(sink - m))
        ot = (_dot(vtp_ref[hs, :], pp.astype(BF16)) + _dot(vtc_ref[hs, :], pc.astype(BF16))
              + _dot(vtm_ref[hs, :], pm.astype(BF16))) / denom
        for g, h in enumerate(heads):
            acc_ref[h * HEAD_DIM:(h + 1) * HEAD_DIM, :] = ot[:, g * BLOCK:(g + 1) * BLOCK]
    a = acc_ref[...]
    ms = jnp.mean(a * a, axis=0, keepdims=True)
    o_ref[...] = (a * lax.rsqrt(ms + NORM_EPS) * g_ref[...]).T.astype(BF16)


def _attention(sinks, qt, k, vt, k_meta, vt_meta, g_out):
    nb = k.shape[0] // BLOCK
    prev_i = lambda i: jnp.maximum(i - 1, 0)
    full = lambda a: pl.BlockSpec(a.shape, lambda i: (0,) * a.ndim)
    return pl.pallas_call(
        _attn_kernel,
        grid=(nb,),
        in_specs=[pl.BlockSpec(memory_space=pltpu.SMEM),
                  pl.BlockSpec((ATTN_WIDTH, BLOCK), lambda i: (0, i)),
                  pl.BlockSpec((BLOCK, KV_WIDTH), lambda i: (prev_i(i), 0)),
                  pl.BlockSpec((BLOCK, KV_WIDTH), lambda i: (i, 0)),
                  pl.BlockSpec((KV_WIDTH, BLOCK), lambda i: (0, prev_i(i))),
                  pl.BlockSpec((KV_WIDTH, BLOCK), lambda i: (0, i)),
                  full(k_meta), full(vt_meta), full(g_out)],
        out_specs=pl.BlockSpec((BLOCK, ATTN_WIDTH), lambda i: (i, 0)),
        out_shape=jax.ShapeDtypeStruct((k.shape[0], ATTN_WIDTH), BF16),
        scratch_shapes=[pltpu.VMEM((ATTN_WIDTH, BLOCK), F32)],
        compiler_params=pltpu.CompilerParams(dimension_semantics=("arbitrary",)),
        name="attention",
    )(sinks, qt, k, k, vt, vt, k_meta, vt_meta, g_out)


def _ssm_kernel(u_ref, bmat_ref, cmat_ref, coef_ref, carry_in_ref, d_ref, gw_ref, gb_ref, g_ref,
                o_ref, carry_out_ref, st_ref, carry_ref, y_ref, *, tm):
    @pl.when(pl.program_id(0) == 0)
    def _():
        carry_ref[...] = carry_in_ref[...]

    ns = SSM_HALF_STATES
    for h in range(2):
        uh = u_ref[:, h * SSM_HALF_W:(h + 1) * SSM_HALF_W].astype(BF16)
        st_ref[...] = _dot(uh, bmat_ref[h])

        def block(b, carry):
            cr, ci = carry
            r0 = pl.multiple_of(b * SUBLANES, SUBLANES)
            xr = st_ref[pl.ds(r0, SUBLANES), 0:ns]
            xi = st_ref[pl.ds(r0, SUBLANES), ns:2 * ns]
            for j, shift in enumerate((1, 2, 4)):
                ar, ai = coef_ref[h, j, 0], coef_ref[h, j, 1]
                sr, si = pltpu.roll(xr, shift, 0), pltpu.roll(xi, shift, 0)
                xr, xi = xr + ar * sr - ai * si, xi + ar * si + ai * sr
            pr, pi = coef_ref[h, 3, 0], coef_ref[h, 3, 1]
            xr, xi = xr + pr * cr - pi * ci, xi + pr * ci + pi * cr
            st_ref[pl.ds(r0, SUBLANES), 0:ns] = xr
            st_ref[pl.ds(r0, SUBLANES), ns:2 * ns] = xi
            return (jnp.broadcast_to(xr[SUBLANES - 1:SUBLANES, :], (SUBLANES, ns)),
                    jnp.broadcast_to(xi[SUBLANES - 1:SUBLANES, :], (SUBLANES, ns)))

        cr, ci = lax.fori_loop(0, tm // SUBLANES, block, (carry_ref[h, 0], carry_ref[h, 1]))
        carry_ref[h, 0] = cr
        carry_ref[h, 1] = ci
        y_ref[:, h * SSM_HALF_W:(h + 1) * SSM_HALF_W] = _dot(st_ref[...].astype(BF16), cmat_ref[h])

    carry_out_ref[...] = carry_ref[...]
    z = _gelu(y_ref[...] + d_ref[...] * u_ref[...])
    z = z * jax.nn.sigmoid(_dot(z.astype(BF16), gw_ref[...]) + gb_ref[...])
    ms = jnp.mean(z * z, axis=-1, keepdims=True)
    o_ref[...] = (z * lax.rsqrt(ms + NORM_EPS) * g_ref[...]).astype(BF16)


def _ssm(u, bmat, cmat, coef, carry_in, d, gw_bf, gb, g_out, tm):
    rows = u.shape[0]
    row_spec = pl.BlockSpec((tm, SSM_WIDTH), lambda i: (i, 0))
    full = lambda a: pl.BlockSpec(a.shape, lambda i: (0,) * a.ndim)
    return pl.pallas_call(
        functools.partial(_ssm_kernel, tm=tm),
        grid=(rows // tm,),
        in_specs=[row_spec, full(bmat), full(cmat), full(coef), full(carry_in), full(d), full(gw_bf), full(gb),
                  full(g_out)],
        out_specs=[row_spec, full(carry_in)],
        out_shape=[jax.ShapeDtypeStruct((rows, SSM_WIDTH), BF16),
                   jax.ShapeDtypeStruct(carry_in.shape, F32)],
        scratch_shapes=[pltpu.VMEM((tm, 2 * SSM_HALF_STATES), F32),
                        pltpu.VMEM(carry_in.shape, F32),
                        pltpu.VMEM((tm, SSM_WIDTH), F32)],
        compiler_params=pltpu.CompilerParams(dimension_semantics=("arbitrary",)),
        name="ssm",
    )(u, bmat, cmat, coef, carry_in, d, gw_bf, gb, g_out)


def _ssm_params(a_re, a_im, log_dt, b_re, b_im, c_re, c_im):
    dt = jnp.exp(log_dt)[:, None]
    mag = jnp.exp(a_re * dt)
    abar_r, abar_i = mag * jnp.cos(a_im * dt), mag * jnp.sin(a_im * dt)
    den = a_re * a_re + a_im * a_im
    nr, ni = abar_r - 1.0, abar_i
    coef_r = (nr * a_re + ni * a_im) / den
    coef_i = (ni * a_re - nr * a_im) / den
    bbar_r = coef_r[..., None] * b_re - coef_i[..., None] * b_im
    bbar_i = coef_r[..., None] * b_im + coef_i[..., None] * b_re
    gh = SSM_GROUPS // 2
    eye = jnp.eye(gh, dtype=F32)

    def half_b(bb):
        return jnp.einsum('gnp,gk->gpkn', bb, eye).reshape(gh * SSM_CH, gh * SSM_STATE)

    def half_c(cc):
        return jnp.einsum('gpn,gk->gnkp', cc, eye).reshape(gh * SSM_STATE, gh * SSM_CH)

    bmat = jnp.stack([jnp.concatenate([half_b(bbar_r[s]), half_b(bbar_i[s])], axis=1)
                      for s in (slice(0, gh), slice(gh, None))]).astype(BF16)
    cmat = jnp.stack([jnp.concatenate([half_c(c_re[s]), -half_c(c_im[s])], axis=0)
                      for s in (slice(0, gh), slice(gh, None))]).astype(BF16)

    def cpow(k):
        m = mag ** k
        return m * jnp.cos(a_im * dt * k), m * jnp.sin(a_im * dt * k)

    rows = jnp.arange(SUBLANES)
    coefs = []
    for shift in (1, 2, 4):
        pr, pi = cpow(float(shift))
        keep = (rows >= shift).astype(F32)[:, None, None]
        coefs.append(jnp.stack([keep * pr[None], keep * pi[None]]))
    pw = (rows + 1).astype(F32)[:, None, None]
    m = mag[None] ** pw
    coefs.append(jnp.stack([m * jnp.cos(a_im[None] * dt[None] * pw), m * jnp.sin(a_im[None] * dt[None] * pw)]))
    coef = jnp.stack(coefs)
    coef = coef.reshape(4, 2, SUBLANES, 2, SSM_HALF_STATES).transpose(3, 0, 1, 2, 4)
    return bmat, cmat, coef


def _top16(s, payload, n_rows):
    rowid = lax.broadcasted_iota(I32, s.shape, 0).astype(F32)
    vals, pays = [], []
    for _ in range(PEER_TOPK):
        m = jnp.max(s, axis=0, keepdims=True)
        pos = jnp.min(jnp.where(s == m, rowid, float(n_rows)), axis=0, keepdims=True)
        sel = rowid == pos
        vals.append(m)
        pays.append(pos if payload is None else jnp.max(jnp.where(sel, payload, -1.0), axis=0, keepdims=True))
        s = jnp.where(sel, -jnp.inf, s)
    return jnp.concatenate(vals, axis=0), jnp.concatenate(pays, axis=0)


_CAND_PAIRS = [(a, b) for a in range(PEER_TOPK) for b in range(PEER_TOPK) if (a + 1) * (b + 1) <= PEER_TOPK]
_CAND_ROWS = -(-len(_CAND_PAIRS) // SUBLANES) * SUBLANES


def _mix_kernel(x_ref, a_ref, s_ref, wo_ref, g2_ref, wq_ref, keys_ref, h1_ref, hn_ref, idx_ref, gate_ref,
                qp_ref, idx_t_ref, gate_t_ref, *, tm):
    mix = _dot(a_ref[...], wo_ref[0:ATTN_WIDTH, :]) + _dot(s_ref[...], wo_ref[ATTN_WIDTH:, :])
    h1 = x_ref[...] + mix
    h1_ref[...] = h1.reshape(tm, SUBLANES, LANES)
    ms = jnp.mean(h1 * h1, axis=-1, keepdims=True)
    hn = h1 * lax.rsqrt(ms + NORM_EPS) * g2_ref[...]
    hn_ref[...] = hn.reshape(tm, SUBLANES, LANES)
    qp_ref[...] = _dot(hn.astype(BF16), wq_ref[...])

    def retrieve(h, tok):
        c0 = pl.multiple_of(h * PEER_DK, PEER_DK)
        q1 = qp_ref[tok, pl.ds(c0, PEER_HALF)].astype(BF16)
        q2 = qp_ref[tok, pl.ds(c0 + PEER_HALF, PEER_HALF)].astype(BF16)
        s1 = _dot_nt(keys_ref[h, 0], q1)
        s2 = _dot_nt(keys_ref[h, 1], q2)
        v1, i1 = _top16(s1, None, PEER_N_KEYS)
        v2, i2 = _top16(s2, None, PEER_N_KEYS)
        pad = _CAND_ROWS - len(_CAND_PAIRS)
        cand = jnp.concatenate([v1[a:a + 1, :] + v2[b:b + 1, :] for a, b in _CAND_PAIRS]
                               + [jnp.full((pad, LANES), -jnp.inf, F32)], axis=0)
        eidx = jnp.concatenate([i1[a:a + 1, :] * float(PEER_N_KEYS) + i2[b:b + 1, :] for a, b in _CAND_PAIRS]
                               + [jnp.zeros((pad, LANES), F32)], axis=0)
        sc, ei = _top16(cand, eidx, _CAND_ROWS)
        p = jnp.exp(sc - sc[0:1, :])
        r0 = pl.multiple_of(h * PEER_TOPK, PEER_TOPK)
        idx_t_ref[pl.ds(r0, PEER_TOPK), tok] = ei.astype(I32)
        gate_t_ref[pl.ds(r0, PEER_TOPK), tok] = p / jnp.sum(p, axis=0, keepdims=True)

    def head_pair(i, _):
        for h in (2 * i, 2 * i + 1):
            for sub in range(tm // LANES):
                retrieve(h, slice(sub * LANES, (sub + 1) * LANES))
        return 0

    lax.fori_loop(0, PEER_HEADS // 2, head_pair, 0)
    idx_ref[...] = idx_t_ref[...].T
    gate_ref[...] = gate_t_ref[...].T


def _mix(x, attn_n, ssm_n, w_out_bf, g2, wq_bf, keys_bf, tm):
    rows = x.shape[0]
    row_spec = lambda w: pl.BlockSpec((tm, w), lambda i: (i, 0))
    tile_spec = pl.BlockSpec((tm, SUBLANES, LANES), lambda i: (i, 0, 0))
    full = lambda a: pl.BlockSpec(a.shape, lambda i: (0,) * a.ndim)
    return pl.pallas_call(
        functools.partial(_mix_kernel, tm=tm),
        grid=(rows // tm,),
        in_specs=[row_spec(D_MODEL), row_spec(ATTN_WIDTH), row_spec(SSM_WIDTH), full(w_out_bf), full(g2),
                  full(wq_bf), full(keys_bf)],
        out_specs=[tile_spec, tile_spec, row_spec(PEER_SEL), row_spec(PEER_SEL)],
        out_shape=[jax.ShapeDtypeStruct((rows, SUBLANES, LANES), F32),
                   jax.ShapeDtypeStruct((rows, SUBLANES, LANES), F32),
                   jax.ShapeDtypeStruct((rows, PEER_SEL), I32),
                   jax.ShapeDtypeStruct((rows, PEER_SEL), F32)],
        scratch_shapes=[pltpu.VMEM((tm, PEER_HEADS * PEER_DK), F32),
                        pltpu.VMEM((PEER_SEL, tm), I32),
                        pltpu.VMEM((PEER_SEL, tm), F32)],
        compiler_params=pltpu.CompilerParams(dimension_semantics=("arbitrary",),
                                             vmem_limit_bytes=48 * 1024 * 1024),
        name="mix_topk",
    )(x, attn_n, ssm_n, w_out_bf, g2, wq_bf, keys_bf)


def _expert_tiles(t):
    return t.astype(BF16).reshape(t.shape[0], SUBLANES, LANES)


def _load_table(tab_hbm, tab_ref, sem):
    @pl.when(pl.program_id(0) == 0)
    def _():
        cp = pltpu.make_async_copy(tab_hbm, tab_ref, sem)
        cp.start()
        cp.wait()


def _peer_u_kernel(idx_ref, h_ref, gate_ref, tab_hbm, act_ref, tab_ref, part_a_ref, part_b_ref, sem, *, tt):
    _load_table(tab_hbm, tab_ref, sem)

    def products(t0, part_refs):
        toks = [(idx_ref.at[t0 + j], h_ref[t0 + j], part_refs[j]) for j in range(len(part_refs))]
        for k in range(PEER_SEL):
            for row, hv, part_ref in toks:
                p = tab_ref[row[k]].astype(F32) * hv
                part_ref[k * HALF_TILE:(k + 1) * HALF_TILE, :] = p[0:HALF_TILE] + p[HALF_TILE:]

    def row_sums(part_ref):
        part = part_ref[pl.ds(0, PEER_SEL, stride=HALF_TILE), :]
        for r in range(1, HALF_TILE):
            part = part + part_ref[pl.ds(r, PEER_SEL, stride=HALF_TILE), :]
        return part

    def finish(t, part):
        dots = jnp.sum(part.T, axis=0, keepdims=True)
        act_ref[t] = _gelu(dots) * gate_ref[t]

    products(0, (part_a_ref, part_b_ref))

    def pair(i, _):
        t = 2 * i
        sums_a, sums_b = row_sums(part_a_ref), row_sums(part_b_ref)
        finish(t - 2, sums_a)
        finish(t - 1, sums_b)
        products(t, (part_a_ref, part_b_ref))
        return 0

    lax.fori_loop(1, tt // 2, pair, 0)
    finish(tt - 2, row_sums(part_a_ref))
    finish(tt - 1, row_sums(part_b_ref))


def _peer_u(idx, hn3, gate3, tab, tt, first):
    rows = idx.shape[0]
    b0 = first // tt
    return pl.pallas_call(
        functools.partial(_peer_u_kernel, tt=tt),
        grid=((rows - first) // tt,),
        in_specs=[pl.BlockSpec((tt, PEER_SEL), lambda i: (i + b0, 0), memory_space=pltpu.SMEM),
                  pl.BlockSpec((tt, SUBLANES, LANES), lambda i: (i + b0, 0, 0)),
                  pl.BlockSpec((tt, 1, PEER_SEL), lambda i: (i + b0, 0, 0)),
                  pl.BlockSpec(memory_space=pl.ANY)],
        out_specs=pl.BlockSpec((tt, 1, PEER_SEL), lambda i: (i, 0, 0)),
        out_shape=jax.ShapeDtypeStruct((rows - first, 1, PEER_SEL), F32),
        scratch_shapes=[pltpu.VMEM(tab.shape, BF16),
                        pltpu.VMEM((PEER_SEL * HALF_TILE, LANES), F32),
                        pltpu.VMEM((PEER_SEL * HALF_TILE, LANES), F32),
                        pltpu.SemaphoreType.DMA],
        compiler_params=pltpu.CompilerParams(dimension_semantics=("arbitrary",),
                                             vmem_limit_bytes=VMEM_LIMIT_TABLE),
        name="peer_u",
    )(idx, hn3, gate3, tab)


def _peer_v_kernel(idx_ref, act_ref, h1_ref, tab_hbm, o_ref, tab_ref, tiles_ref, *rest, tt):
    rep_refs, sem = rest[:-1], rest[-1]
    _load_table(tab_hbm, tab_ref, sem)
    n_acc = 2
    g = PEER_V_GROUP

    def replicated(t):
        return jnp.broadcast_to(act_ref[t], (PEER_SEL, LANES)).T

    def accumulate(t0):
        rows = [idx_ref.at[t0 + j] for j in range(g)]
        accs = [[jnp.zeros((SUBLANES, LANES), F32)] * n_acc for _ in range(g)]
        for k in range(PEER_SEL):
            for j, rep_ref in enumerate(rep_refs):
                w = jnp.broadcast_to(rep_ref[k:k + 1, :], (SUBLANES, LANES))
                accs[j][k % n_acc] = accs[j][k % n_acc] + w * tab_ref[rows[j][k]].astype(F32)
        for j in range(g):
            tiles_ref[t0 + j] = h1_ref[t0 + j] + sum(accs[j][1:], accs[j][0])

    for j, rep_ref in enumerate(rep_refs):
        rep_ref[...] = replicated(j)

    def group(i, _):
        t = g * i
        nxt = [replicated(jnp.minimum(t + g + j, tt - 1)) for j in range(g)]
        accumulate(t)
        for rep_ref, tile in zip(rep_refs, nxt):
            rep_ref[...] = tile
        return 0

    lax.fori_loop(0, tt // g, group, 0)
    o_ref[...] = tiles_ref[...].reshape(tt, D_MODEL)


def _peer_v(idx, act3, h13, tab, tt, first):
    rows = idx.shape[0]
    b0 = first // tt
    tile = pl.BlockSpec((tt, SUBLANES, LANES), lambda i: (i + b0, 0, 0))
    return pl.pallas_call(
        functools.partial(_peer_v_kernel, tt=tt),
        grid=((rows - first) // tt,),
        in_specs=[pl.BlockSpec((tt, PEER_SEL), lambda i: (i + b0, 0), memory_space=pltpu.SMEM),
                  pl.BlockSpec((tt, 1, PEER_SEL), lambda i: (i + b0, 0, 0)),
                  tile, pl.BlockSpec(memory_space=pl.ANY)],
        out_specs=pl.BlockSpec((tt, D_MODEL), lambda i: (i, 0)),
        out_shape=jax.ShapeDtypeStruct((rows - first, D_MODEL), F32),
        scratch_shapes=[pltpu.VMEM(tab.shape, BF16), pltpu.VMEM((tt, SUBLANES, LANES), F32)]
                       + [pltpu.VMEM((PEER_SEL, LANES), F32)] * PEER_V_GROUP
                       + [pltpu.SemaphoreType.DMA],
        compiler_params=pltpu.CompilerParams(dimension_semantics=("arbitrary",),
                                             vmem_limit_bytes=VMEM_LIMIT_TABLE),
        name="peer_v",
    )(idx, act3, h13, tab)


SC_LANES = 16
SC_WORKERS = 32
SC_ROWS = 32
SC_COLS = 256
SC_COLS_U = 256
SC_ROW_GROUP = 4
SC_SHARE_U = (17, 64)
SC_SHARE_V = (11, 32)


def _peer_v_sc_kernel(tab_hbm, idx_hbm, act_hbm, h1_hbm, out_hbm, idx_v, act_v, rows0_v, rows1_v, acc_v, sem0, sem1,
                      sem_idx, sem_act, sem_h, *, n_tok):
    wid = lax.axis_index("s") * 2 + lax.axis_index("c")
    bufs = ((rows0_v, sem0), (rows1_v, sem1))
    n_chunks = PEER_SEL // SC_ROWS

    def gather(c):
        rows_v, sem = bufs[c % 2]
        return pltpu.make_async_copy(tab_hbm.at[idx_v.at[pl.ds(c * SC_ROWS, SC_ROWS)]], rows_v, sem)

    @pl.loop(0, n_tok)
    def _(i):
        t = wid * n_tok + i
        cp_idx = pltpu.make_async_copy(idx_hbm.at[pl.ds(t * PEER_SEL, PEER_SEL)], idx_v, sem_idx)
        cp_act = pltpu.make_async_copy(act_hbm.at[pl.ds(t * PEER_SEL, PEER_SEL)], act_v, sem_act)
        cp_h = pltpu.make_async_copy(h1_hbm.at[t], acc_v, sem_h)
        cp_idx.start()
        cp_act.start()
        cp_h.start()
        cp_idx.wait()
        gather(0).start()
        cp_act.wait()
        cp_h.wait()
        for c in range(n_chunks):
            if c + 1 < n_chunks:
                gather(c + 1).start()
            gather(c).wait()
            rows_v = bufs[c % 2][0]

            @pl.loop(0, D_MODEL // SC_COLS)
            def _(cb):
                cols = [pl.ds(cb * SC_COLS + j * SC_LANES, SC_LANES) for j in range(SC_COLS // SC_LANES)]

                @pl.loop(0, SC_ROWS // SC_ROW_GROUP)
                def _(rg):
                    accs = [acc_v[col] for col in cols]
                    for rr in range(SC_ROW_GROUP):
                        r = rg * SC_ROW_GROUP + rr
                        w = plsc.load_gather(act_v, [jnp.full((SC_LANES,), c * SC_ROWS, I32) + r])
                        accs = [a + w * rows_v[r, col] for a, col in zip(accs, cols)]
                    for a, col in zip(accs, cols):
                        acc_v[col] = a
        pltpu.sync_copy(acc_v, out_hbm.at[t])


def _peer_v_sc(tab, idx_flat, act_flat, h1_rows):
    n = h1_rows.shape[0]
    mesh = plsc.VectorSubcoreMesh(core_axis_name="c", subcore_axis_name="s")
    return pl.kernel(
        functools.partial(_peer_v_sc_kernel, n_tok=n // SC_WORKERS),
        out_type=jax.ShapeDtypeStruct((n, D_MODEL), F32),
        mesh=mesh,
        scratch_types=[pltpu.VMEM((PEER_SEL,), I32), pltpu.VMEM((PEER_SEL,), F32),
                       pltpu.VMEM((SC_ROWS, D_MODEL), F32), pltpu.VMEM((SC_ROWS, D_MODEL), F32),
                       pltpu.VMEM((D_MODEL,), F32)] + [pltpu.SemaphoreType.DMA] * 5,
        compiler_params=pltpu.CompilerParams(needs_layout_passes=False),
        name="peer_v_sc",
    )(tab, idx_flat, act_flat, h1_rows)


def _peer_u_sc_kernel(tab_hbm, idx_hbm, hn_hbm, part_hbm, idx_v, h_v, rows0_v, rows1_v, part_v, sem0, sem1, sem_idx,
                      sem_h, *, n_tok):
    wid = lax.axis_index("s") * 2 + lax.axis_index("c")
    bufs = ((rows0_v, sem0), (rows1_v, sem1))
    n_chunks = PEER_SEL // SC_ROWS
    part_w = PEER_SEL * SC_LANES

    def gather(c):
        rows_v, sem = bufs[c % 2]
        return pltpu.make_async_copy(tab_hbm.at[idx_v.at[pl.ds(c * SC_ROWS, SC_ROWS)]], rows_v, sem)

    @pl.loop(0, n_tok)
    def _(i):
        t = wid * n_tok + i
        cp_idx = pltpu.make_async_copy(idx_hbm.at[pl.ds(t * PEER_SEL, PEER_SEL)], idx_v, sem_idx)
        cp_h = pltpu.make_async_copy(hn_hbm.at[t], h_v, sem_h)
        cp_idx.start()
        cp_h.start()

        @pl.loop(0, PEER_SEL)
        def _(k):
            part_v[pl.ds(k * SC_LANES, SC_LANES)] = jnp.zeros((SC_LANES,), F32)

        cp_idx.wait()
        gather(0).start()
        cp_h.wait()

        for c in range(n_chunks):
            if c + 1 < n_chunks:
                gather(c + 1).start()
            gather(c).wait()
            rows_v = bufs[c % 2][0]

            @pl.loop(0, D_MODEL // SC_COLS_U)
            def _(cb):
                cols = [pl.ds(cb * SC_COLS_U + j * SC_LANES, SC_LANES) for j in range(SC_COLS_U // SC_LANES)]
                hs = [h_v[col] for col in cols]

                @pl.loop(0, SC_ROWS // SC_ROW_GROUP)
                def _(rg):
                    for rr in range(SC_ROW_GROUP):
                        r = rg * SC_ROW_GROUP + rr
                        slot = pl.ds((c * SC_ROWS + r) * SC_LANES, SC_LANES)
                        prods = [h * rows_v[r, col] for h, col in zip(hs, cols)]
                        while len(prods) > 1:
                            prods = [a + b for a, b in zip(prods[0::2], prods[1::2])]
                        part_v[slot] = part_v[slot] + prods[0]
        pltpu.sync_copy(part_v, part_hbm.at[pl.ds(t * part_w, part_w)])


def _peer_u_sc(tab, idx_flat, hn_rows):
    n = hn_rows.shape[0]
    mesh = plsc.VectorSubcoreMesh(core_axis_name="c", subcore_axis_name="s")
    return pl.kernel(
        functools.partial(_peer_u_sc_kernel, n_tok=n // SC_WORKERS),
        out_type=jax.ShapeDtypeStruct((n * PEER_SEL * SC_LANES,), F32),
        mesh=mesh,
        scratch_types=[pltpu.VMEM((PEER_SEL,), I32), pltpu.VMEM((D_MODEL,), F32),
                       pltpu.VMEM((SC_ROWS, D_MODEL), F32), pltpu.VMEM((SC_ROWS, D_MODEL), F32),
                       pltpu.VMEM((PEER_SEL * SC_LANES,), F32)] + [pltpu.SemaphoreType.DMA] * 4,
        compiler_params=pltpu.CompilerParams(needs_layout_passes=False),
        name="peer_u_sc",
    )(tab, idx_flat, hn_rows)


def _fold_act_kernel(part_ref, fold_ref, gate_ref, act_ref):
    p = part_ref[...]
    hi = p.astype(BF16)
    lo = (p - hi.astype(F32)).astype(BF16)
    dots = _dot(hi, fold_ref[...]) + _dot(lo, fold_ref[...])
    act_ref[...] = _gelu(dots) * gate_ref[...]


def _fold_act(part2, gate, tm):
    n = part2.shape[0]
    fold = (jnp.arange(PEER_SEL * SC_LANES)[:, None] // SC_LANES == jnp.arange(PEER_SEL)[None, :]).astype(BF16)
    return pl.pallas_call(
        _fold_act_kernel,
        grid=(n // tm,),
        in_specs=[pl.BlockSpec((tm, PEER_SEL * SC_LANES), lambda i: (i, 0)),
                  pl.BlockSpec(fold.shape, lambda i: (0, 0)),
                  pl.BlockSpec((tm, PEER_SEL), lambda i: (i, 0))],
        out_specs=pl.BlockSpec((tm, PEER_SEL), lambda i: (i, 0)),
        out_shape=jax.ShapeDtypeStruct((n, PEER_SEL), F32),
        compiler_params=pltpu.CompilerParams(dimension_semantics=("arbitrary",)),
        name="fold_act",
    )(part2, fold, gate)


def _block(x2, meta_tokens, norm1_g, w_in, q_norm_g, k_norm_g, attn_sinks, ssm_a_re, ssm_a_im, ssm_log_dt,
           ssm_b_re, ssm_b_im, ssm_c_re, ssm_c_im, ssm_d, ssm_glu_w, ssm_glu_b, attn_out_g, ssm_out_g, w_out,
           norm2_g, peer_w_query, peer_sub_keys, peer_u, peer_v, *, tm_proj, tm_ssm, tm_mix, tt_peer):
    rows = x2.shape[0]
    row2 = lambda a: a.reshape(1, -1).astype(F32)
    w_in_bf = w_in.astype(BF16)
    seg = jnp.arange(QK_WIDTH) // HEAD_DIM
    ones_bd = (seg[:, None] == seg[None, :]).astype(BF16)
    scale = HEAD_DIM ** -0.5
    qkg = jnp.concatenate([jnp.tile(q_norm_g.astype(F32), N_Q_HEADS) * scale,
                           jnp.tile(k_norm_g.astype(F32), N_KV_HEADS)]).reshape(1, QK_WIDTH)
    bmat, cmat, coef = _ssm_params(ssm_a_re.astype(F32), ssm_a_im.astype(F32), ssm_log_dt.astype(F32),
                                   ssm_b_re.astype(F32), ssm_b_im.astype(F32), ssm_c_re.astype(F32),
                                   ssm_c_im.astype(F32))
    glu_w_bf = ssm_glu_w.astype(BF16)
    ssm_args = (row2(ssm_d), glu_w_bf, row2(ssm_glu_b), row2(ssm_out_g))

    h0 = jnp.concatenate([jnp.zeros((BLOCK - N_META, D_MODEL), F32), meta_tokens.astype(F32)], axis=0)
    _, k0, vt0, u0 = _in_proj(h0, row2(norm1_g), w_in_bf, ones_bd, qkg, BLOCK)
    carry0 = jnp.zeros((2, 2, SUBLANES, SSM_HALF_STATES), F32)
    _, carry = _ssm(u0, bmat, cmat, coef, carry0, *ssm_args, BLOCK)

    qt, k, vt, u = _in_proj(x2, row2(norm1_g), w_in_bf, ones_bd, qkg, tm_proj)
    g_attn = jnp.broadcast_to(attn_out_g.astype(F32)[:, None], (ATTN_WIDTH, BLOCK))
    attn_n = _attention(attn_sinks.astype(F32), qt, k, vt, k0[BLOCK - N_META:], vt0[:, BLOCK - N_META:], g_attn)
    ssm_n, _ = _ssm(u, bmat, cmat, coef, carry, *ssm_args, tm_ssm)

    h1, hn, idx, gate = _mix(x2, attn_n, ssm_n, w_out.astype(BF16), row2(norm2_g), peer_w_query.astype(BF16),
                             peer_sub_keys.astype(BF16), tm_mix)
    unit = math.lcm(SC_WORKERS, tt_peer, tm_mix)
    share = lambda num, den: rows * num // den // unit * unit
    n_u, n_v = share(*SC_SHARE_U), share(*SC_SHARE_V)
    part = _peer_u_sc(peer_u.astype(F32), idx[:n_u].reshape(-1), hn[:n_u].reshape(n_u, D_MODEL))
    act_tc = _peer_u(idx, hn, gate.reshape(rows, 1, PEER_SEL), _expert_tiles(peer_u), tt_peer, n_u)
    act_sc = _fold_act(part.reshape(n_u, PEER_SEL * SC_LANES), gate[:n_u], tm_mix)
    act = jnp.concatenate([act_sc.reshape(n_u, 1, PEER_SEL), act_tc], axis=0)
    sc_out = _peer_v_sc(peer_v.astype(F32), idx[:n_v].reshape(-1), act[:n_v].reshape(-1),
                        h1[:n_v].reshape(n_v, D_MODEL))
    tc_out = _peer_v(idx, act, h1, _expert_tiles(peer_v), tt_peer, n_v)
    return jnp.concatenate([sc_out, tc_out], axis=0)


def kernel(x, meta_tokens, norm1_g, w_in, q_norm_g, k_norm_g, attn_sinks, ssm_a_re, ssm_a_im, ssm_log_dt, ssm_b_re, ssm_b_im, ssm_c_re, ssm_c_im, ssm_d, ssm_glu_w, ssm_glu_b, attn_out_g, ssm_out_g, w_out, norm2_g, peer_w_query, peer_sub_keys, peer_u, peer_v):
    b, seq, d = x.shape
    outs = []
    for bi in range(b):
        outs.append(_block(x[bi].astype(F32), meta_tokens, norm1_g[0], w_in[0], q_norm_g[0], k_norm_g[0],
                           attn_sinks[0], ssm_a_re[0], ssm_a_im[0], ssm_log_dt[0], ssm_b_re[0], ssm_b_im[0],
                           ssm_c_re[0], ssm_c_im[0], ssm_d[0], ssm_glu_w[0], ssm_glu_b[0], attn_out_g[0],
                           ssm_out_g[0], w_out[0], norm2_g[0], peer_w_query[0], peer_sub_keys[0], peer_u[0],
                           peer_v[0], tm_proj=512, tm_ssm=256, tm_mix=256, tt_peer=32))
    return jnp.stack(outs).astype(x.dtype)
```

```python
import functools
import math

import jax
import jax.numpy as jnp
from jax import lax
from jax.experimental import pallas as pl
from jax.experimental.pallas import tpu as pltpu
from jax.experimental.pallas import tpu_sc as plsc

F32 = jnp.float32
BF16 = jnp.bfloat16
I32 = jnp.int32

D_MODEL = 1024
N_META = 16
BLOCK = 128
ATTN_WIDTH = 512
HEAD_DIM = 64
N_Q_HEADS = 8
N_KV_HEADS = 2
GQA = 4
KV_WIDTH = 128
QK_WIDTH = ATTN_WIDTH + KV_WIDTH
SSM_WIDTH = 512
SSM_CH = 16
SSM_GROUPS = 32
SSM_STATE = 64
IN_WIDTH = ATTN_WIDTH + 2 * KV_WIDTH + SSM_WIDTH
PEER_HEADS = 8
PEER_N_KEYS = 128
PEER_TOPK = 16
PEER_DK = 256
PEER_HALF = 128
PEER_SEL = PEER_HEADS * PEER_TOPK
NORM_EPS = 1e-6
MASK_VALUE = -1e30

SUBLANES = 8
LANES = 128
SSM_HALF_W = SSM_WIDTH // 2
SSM_HALF_STATES = SSM_GROUPS // 2 * SSM_STATE
HALF_TILE = SUBLANES // 2
PEER_V_GROUP = 2
VMEM_LIMIT_TABLE = 52 * 1024 * 1024


def _dot(a, b):
    return jnp.dot(a, b, preferred_element_type=F32)


def _dot_nt(a, b):
    return lax.dot_general(a, b, (((1,), (1,)), ((), ())), preferred_element_type=F32)


def _gelu(x):
    return 0.5 * x * (1.0 + lax.erf(x * (2.0 ** -0.5)))


def _in_proj_kernel(x_ref, g1_ref, w_ref, ones_ref, qkg_ref, qt_ref, k_ref, vt_ref, u_ref):
    x = x_ref[...]
    ms = jnp.mean(x * x, axis=-1, keepdims=True)
    y = (x * lax.rsqrt(ms + NORM_EPS) * g1_ref[...]).astype(BF16)
    proj = _dot(y, w_ref[...])
    qk = proj[:, :QK_WIDTH]
    ss = _dot((qk * qk).astype(BF16), ones_ref[...]) * (1.0 / HEAD_DIM)
    qkn = qk * lax.rsqrt(ss + NORM_EPS) * qkg_ref[...]
    qt_ref[...] = qkn[:, :ATTN_WIDTH].T.astype(BF16)
    k_ref[...] = qkn[:, ATTN_WIDTH:].astype(BF16)
    vt_ref[...] = proj[:, QK_WIDTH:QK_WIDTH + KV_WIDTH].T.astype(BF16)
    u_ref[...] = proj[:, QK_WIDTH + KV_WIDTH:]


def _in_proj(x, g1, w_in_bf, ones_bd, qkg, tm):
    rows = x.shape[0]
    row_spec = lambda w: pl.BlockSpec((tm, w), lambda i: (i, 0))
    col_spec = lambda w: pl.BlockSpec((w, tm), lambda i: (0, i))
    full = lambda a: pl.BlockSpec(a.shape, lambda i: (0,) * a.ndim)
    return pl.pallas_call(
        _in_proj_kernel,
        grid=(rows // tm,),
        in_specs=[row_spec(D_MODEL), full(g1), full(w_in_bf), full(ones_bd), full(qkg)],
        out_specs=[col_spec(ATTN_WIDTH), row_spec(KV_WIDTH), col_spec(KV_WIDTH), row_spec(SSM_WIDTH)],
        out_shape=[jax.ShapeDtypeStruct((ATTN_WIDTH, rows), BF16),
                   jax.ShapeDtypeStruct((rows, KV_WIDTH), BF16),
                   jax.ShapeDtypeStruct((KV_WIDTH, rows), BF16),
                   jax.ShapeDtypeStruct((rows, SSM_WIDTH), F32)],
        compiler_params=pltpu.CompilerParams(dimension_semantics=("arbitrary",)),
        name="in_proj",
    )(x, g1, w_in_bf, ones_bd, qkg)


def _attn_kernel(sink_ref, qt_ref, kp_ref, kc_ref, vtp_ref, vtc_ref, km_ref, vtm_ref, g_ref, o_ref, acc_ref):
    i = pl.program_id(0)
    group_w = GQA * BLOCK
    key = lax.broadcasted_iota(I32, (BLOCK, group_w), 0)
    qry = lax.broadcasted_iota(I32, (BLOCK, group_w), 1) & (BLOCK - 1)
    prev_ok = jnp.logical_and(key > qry, i > 0)
    cur_ok = key <= qry
    for kv in range(N_KV_HEADS):
        hs = slice(kv * HEAD_DIM, (kv + 1) * HEAD_DIM)
        heads = range(kv * GQA, (kv + 1) * GQA)
        qt = jnp.concatenate([qt_ref[h * HEAD_DIM:(h + 1) * HEAD_DIM, :] for h in heads], axis=1)
        sink = jnp.concatenate([jnp.full((1, BLOCK), sink_ref[h], F32) for h in heads], axis=1)
        sp = jnp.where(prev_ok, _dot(kp_ref[:, hs], qt), MASK_VALUE)
        sc = jnp.where(cur_ok, _dot(kc_ref[:, hs], qt), MASK_VALUE)
        sm = _dot(km_ref[:, hs], qt)
        m = jnp.maximum(jnp.maximum(jnp.max(sp, axis=0, keepdims=True), jnp.max(sc, axis=0, keepdims=True)),
                        jnp.maximum(jnp.max(sm, axis=0, keepdims=True), sink))
        pp, pc, pm = jnp.exp(sp - m), jnp.exp(sc - m), jnp.exp(sm - m)
        denom = (jnp.sum(pp, axis=0, keepdims=True) + jnp.sum(pc, axis=0, keepdims=True)
                 + jnp.sum(pm, axis=0, keepdims=True) + jnp.exp(sink - m))
        ot = (_dot(vtp_ref[hs, :], pp.astype(BF16)) + _dot(vtc_ref[hs, :], pc.astype(BF16))
              + _dot(vtm_ref[hs, :], pm.astype(BF16))) / denom
        for g, h in enumerate(heads):
            acc_ref[h * HEAD_DIM:(h + 1) * HEAD_DIM, :] = ot[:, g * BLOCK:(g + 1) * BLOCK]
    a = acc_ref[...]
    ms = jnp.mean(a * a, axis=0, keepdims=True)
    o_ref[...] = (a * lax.rsqrt(ms + NORM_EPS) * g_ref[...]).T.astype(BF16)


def _attention(sinks, qt, k, vt, k_meta, vt_meta, g_out):
    nb = k.shape[0] // BLOCK
    prev_i = lambda i: jnp.maximum(i - 1, 0)
    full = lambda a: pl.BlockSpec(a.shape, lambda i: (0,) * a.ndim)
    return pl.pallas_call(
        _attn_kernel,
        grid=(nb,),
        in_specs=[pl.BlockSpec(memory_space=pltpu.SMEM),
                  pl.BlockSpec((ATTN_WIDTH, BLOCK), lambda i: (0, i)),
                  pl.BlockSpec((BLOCK, KV_WIDTH), lambda i: (prev_i(i), 0)),
                  pl.BlockSpec((BLOCK, KV_WIDTH), lambda i: (i, 0)),
                  pl.BlockSpec((KV_WIDTH, BLOCK), lambda i: (0, prev_i(i))),
                  pl.BlockSpec((KV_WIDTH, BLOCK), lambda i: (0, i)),
                  full(k_meta), full(vt_meta), full(g_out)],
        out_specs=pl.BlockSpec((BLOCK, ATTN_WIDTH), lambda i: (i, 0)),
        out_shape=jax.ShapeDtypeStruct((k.shape[0], ATTN_WIDTH), BF16),
        scratch_shapes=[pltpu.VMEM((ATTN_WIDTH, BLOCK), F32)],
        compiler_params=pltpu.CompilerParams(dimension_semantics=("arbitrary",)),
        name="attention",
    )(sinks, qt, k, k, vt, vt, k_meta, vt_meta, g_out)


def _ssm_kernel(u_ref, bmat_ref, cmat_ref, coef_ref, carry_in_ref, d_ref, gw_ref, gb_ref, g_ref,
                o_ref, carry_out_ref, st_ref, carry_ref, y_ref, *, tm):
    @pl.when(pl.program_id(0) == 0)
    def _():
        carry_ref[...] = carry_in_ref[...]

    ns = SSM_HALF_STATES
    for h in range(2):
        uh = u_ref[:, h * SSM_HALF_W:(h + 1) * SSM_HALF_W].astype(BF16)
        st_ref[...] = _dot(uh, bmat_ref[h])

        def block(b, carry):
            cr, ci = carry
            r0 = pl.multiple_of(b * SUBLANES, SUBLANES)
            xr = st_ref[pl.ds(r0, SUBLANES), 0:ns]
            xi = st_ref[pl.ds(r0, SUBLANES), ns:2 * ns]
            for j, shift in enumerate((1, 2, 4)):
                ar, ai = coef_ref[h, j, 0], coef_ref[h, j, 1]
                sr, si = pltpu.roll(xr, shift, 0), pltpu.roll(xi, shift, 0)
                xr, xi = xr + ar * sr - ai * si, xi + ar * si + ai * sr
            pr, pi = coef_ref[h, 3, 0], coef_ref[h, 3, 1]
            xr, xi = xr + pr * cr - pi * ci, xi + pr * ci + pi * cr
            st_ref[pl.ds(r0, SUBLANES), 0:ns] = xr
            st_ref[pl.ds(r0, SUBLANES), ns:2 * ns] = xi
            return (jnp.broadcast_to(xr[SUBLANES - 1:SUBLANES, :], (SUBLANES, ns)),
                    jnp.broadcast_to(xi[SUBLANES - 1:SUBLANES, :], (SUBLANES, ns)))

        cr, ci = lax.fori_loop(0, tm // SUBLANES, block, (carry_ref[h, 0], carry_ref[h, 1]))
        carry_ref[h, 0] = cr
        carry_ref[h, 1] = ci
        y_ref[:, h * SSM_HALF_W:(h + 1) * SSM_HALF_W] = _dot(st_ref[...].astype(BF16), cmat_ref[h])

    carry_out_ref[...] = carry_ref[...]
    z = _gelu(y_ref[...] + d_ref[...] * u_ref[...])
    z = z * jax.nn.sigmoid(_dot(z.astype(BF16), gw_ref[...]) + gb_ref[...])
    ms = jnp.mean(z * z, axis=-1, keepdims=True)
    o_ref[...] = (z * lax.rsqrt(ms + NORM_EPS) * g_ref[...]).astype(BF16)


def _ssm(u, bmat, cmat, coef, carry_in, d, gw_bf, gb, g_out, tm):
    rows = u.shape[0]
    row_spec = pl.BlockSpec((tm, SSM_WIDTH), lambda i: (i, 0))
    full = lambda a: pl.BlockSpec(a.shape, lambda i: (0,) * a.ndim)
    return pl.pallas_call(
        functools.partial(_ssm_kernel, tm=tm),
        grid=(rows // tm,),
        in_specs=[row_spec, full(bmat), full(cmat), full(coef), full(carry_in), full(d), full(gw_bf), full(gb),
                  full(g_out)],
        out_specs=[row_spec, full(carry_in)],
        out_shape=[jax.ShapeDtypeStruct((rows, SSM_WIDTH), BF16),
                   jax.ShapeDtypeStruct(carry_in.shape, F32)],
        scratch_shapes=[pltpu.VMEM((tm, 2 * SSM_HALF_STATES), F32),
                        pltpu.VMEM(carry_in.shape, F32),
                        pltpu.VMEM((tm, SSM_WIDTH), F32)],
        compiler_params=pltpu.CompilerParams(dimension_semantics=("arbitrary",)),
        name="ssm",
    )(u, bmat, cmat, coef, carry_in, d, gw_bf, gb, g_out)


def _ssm_params(a_re, a_im, log_dt, b_re, b_im, c_re, c_im):
    dt = jnp.exp(log_dt)[:, None]
    mag = jnp.exp(a_re * dt)
    abar_r, abar_i = mag * jnp.cos(a_im * dt), mag * jnp.sin(a_im * dt)
    den = a_re * a_re + a_im * a_im
    nr, ni = abar_r - 1.0, abar_i
    coef_r = (nr * a_re + ni * a_im) / den
    coef_i = (ni * a_re - nr * a_im) / den
    bbar_r = coef_r[..., None] * b_re - coef_i[..., None] * b_im
    bbar_i = coef_r[..., None] * b_im + coef_i[..., None] * b_re
    gh = SSM_GROUPS // 2
    eye = jnp.eye(gh, dtype=F32)

    def half_b(bb):
        return jnp.einsum('gnp,gk->gpkn', bb, eye).reshape(gh * SSM_CH, gh * SSM_STATE)

    def half_c(cc):
        return jnp.einsum('gpn,gk->gnkp', cc, eye).reshape(gh * SSM_STATE, gh * SSM_CH)

    bmat = jnp.stack([jnp.concatenate([half_b(bbar_r[s]), half_b(bbar_i[s])], axis=1)
                      for s in (slice(0, gh), slice(gh, None))]).astype(BF16)
    cmat = jnp.stack([jnp.concatenate([half_c(c_re[s]), -half_c(c_im[s])], axis=0)
                      for s in (slice(0, gh), slice(gh, None))]).astype(BF16)

    def cpow(k):
        m = mag ** k
        return m * jnp.cos(a_im * dt * k), m * jnp.sin(a_im * dt * k)

    rows = jnp.arange(SUBLANES)
    coefs = []
    for shift in (1, 2, 4):
        pr, pi = cpow(float(shift))
        keep = (rows >= shift).astype(F32)[:, None, None]
        coefs.append(jnp.stack([keep * pr[None], keep * pi[None]]))
    pw = (rows + 1).astype(F32)[:, None, None]
    m = mag[None] ** pw
    coefs.append(jnp.stack([m * jnp.cos(a_im[None] * dt[None] * pw), m * jnp.sin(a_im[None] * dt[None] * pw)]))
    coef = jnp.stack(coefs)
    coef = coef.reshape(4, 2, SUBLANES, 2, SSM_HALF_STATES).transpose(3, 0, 1, 2, 4)
    return bmat, cmat, coef


def _top16(s, payload, n_rows):
    rowid = lax.broadcasted_iota(I32, s.shape, 0).astype(F32)
    vals, pays = [], []
    for _ in range(PEER_TOPK):
        m = jnp.max(s, axis=0, keepdims=True)
        pos = jnp.min(jnp.where(s == m, rowid, float(n_rows)), axis=0, keepdims=True)
        sel = rowid == pos
        vals.append(m)
        pays.append(pos if payload is None else jnp.max(jnp.where(sel, payload, -1.0), axis=0, keepdims=True))
        s = jnp.where(sel, -jnp.inf, s)
    return jnp.concatenate(vals, axis=0), jnp.concatenate(pays, axis=0)


_CAND_PAIRS = [(a, b) for a in range(PEER_TOPK) for b in range(PEER_TOPK) if (a + 1) * (b + 1) <= PEER_TOPK]
_CAND_ROWS = -(-len(_CAND_PAIRS) // SUBLANES) * SUBLANES


def _mix_kernel(x_ref, a_ref, s_ref, wo_ref, g2_ref, wq_ref, keys_ref, h1_ref, hn_ref, idx_ref, gate_ref,
                qp_ref, idx_t_ref, gate_t_ref, *, tm):
    mix = _dot(a_ref[...], wo_ref[0:ATTN_WIDTH, :]) + _dot(s_ref[...], wo_ref[ATTN_WIDTH:, :])
    h1 = x_ref[...] + mix
    h1_ref[...] = h1.reshape(tm, SUBLANES, LANES)
    ms = jnp.mean(h1 * h1, axis=-1, keepdims=True)
    hn = h1 * lax.rsqrt(ms + NORM_EPS) * g2_ref[...]
    hn_ref[...] = hn.reshape(tm, SUBLANES, LANES)
    qp_ref[...] = _dot(hn.astype(BF16), wq_ref[...])

    def retrieve(h, tok):
        c0 = pl.multiple_of(h * PEER_DK, PEER_DK)
        q1 = qp_ref[tok, pl.ds(c0, PEER_HALF)].astype(BF16)
        q2 = qp_ref[tok, pl.ds(c0 + PEER_HALF, PEER_HALF)].astype(BF16)
        s1 = _dot_nt(keys_ref[h, 0], q1)
        s2 = _dot_nt(keys_ref[h, 1], q2)
        v1, i1 = _top16(s1, None, PEER_N_KEYS)
        v2, i2 = _top16(s2, None, PEER_N_KEYS)
        pad = _CAND_ROWS - len(_CAND_PAIRS)
        cand = jnp.concatenate([v1[a:a + 1, :] + v2[b:b + 1, :] for a, b in _CAND_PAIRS]
                               + [jnp.full((pad, LANES), -jnp.inf, F32)], axis=0)
        eidx = jnp.concatenate([i1[a:a + 1, :] * float(PEER_N_KEYS) + i2[b:b + 1, :] for a, b in _CAND_PAIRS]
                               + [jnp.zeros((pad, LANES), F32)], axis=0)
        sc, ei = _top16(cand, eidx, _CAND_ROWS)
        p = jnp.exp(sc - sc[0:1, :])
        r0 = pl.multiple_of(h * PEER_TOPK, PEER_TOPK)
        idx_t_ref[pl.ds(r0, PEER_TOPK), tok] = ei.astype(I32)
        gate_t_ref[pl.ds(r0, PEER_TOPK), tok] = p / jnp.sum(p, axis=0, keepdims=True)

    def head_pair(i, _):
        for h in (2 * i, 2 * i + 1):
            for sub in range(tm // LANES):
                retrieve(h, slice(sub * LANES, (sub + 1) * LANES))
        return 0

    lax.fori_loop(0, PEER_HEADS // 2, head_pair, 0)
    idx_ref[...] = idx_t_ref[...].T
    gate_ref[...] = gate_t_ref[...].T


def _mix(x, attn_n, ssm_n, w_out_bf, g2, wq_bf, keys_bf, tm, first, rows):
    b0 = first // tm
    in_spec = lambda w: pl.BlockSpec((tm, w), lambda i: (i + b0, 0))
    row_spec = lambda w: pl.BlockSpec((tm, w), lambda i: (i, 0))
    tile_spec = pl.BlockSpec((tm, SUBLANES, LANES), lambda i: (i, 0, 0))
    full = lambda a: pl.BlockSpec(a.shape, lambda i: (0,) * a.ndim)
    return pl.pallas_call(
        functools.partial(_mix_kernel, tm=tm),
        grid=(rows // tm,),
        in_specs=[in_spec(D_MODEL), in_spec(ATTN_WIDTH), in_spec(SSM_WIDTH), full(w_out_bf), full(g2),
                  full(wq_bf), full(keys_bf)],
        out_specs=[tile_spec, tile_spec, row_spec(PEER_SEL), row_spec(PEER_SEL)],
        out_shape=[jax.ShapeDtypeStruct((rows, SUBLANES, LANES), F32),
                   jax.ShapeDtypeStruct((rows, SUBLANES, LANES), F32),
                   jax.ShapeDtypeStruct((rows, PEER_SEL), I32),
                   jax.ShapeDtypeStruct((rows, PEER_SEL), F32)],
        scratch_shapes=[pltpu.VMEM((tm, PEER_HEADS * PEER_DK), F32),
                        pltpu.VMEM((PEER_SEL, tm), I32),
                        pltpu.VMEM((PEER_SEL, tm), F32)],
        compiler_params=pltpu.CompilerParams(dimension_semantics=("arbitrary",),
                                             vmem_limit_bytes=48 * 1024 * 1024),
        name="mix_topk",
    )(x, attn_n, ssm_n, w_out_bf, g2, wq_bf, keys_bf)


def _expert_tiles(t):
    return t.astype(BF16).reshape(t.shape[0], SUBLANES, LANES)


def _load_table(tab_hbm, tab_ref, sem):
    @pl.when(pl.program_id(0) == 0)
    def _():
        cp = pltpu.make_async_copy(tab_hbm, tab_ref, sem)
        cp.start()
        cp.wait()


def _peer_u_kernel(idx_ref, h_ref, gate_ref, tab_hbm, act_ref, tab_ref, part_a_ref, part_b_ref, sem, *, tt):
    _load_table(tab_hbm, tab_ref, sem)

    def products(t0, part_refs):
        toks = [(idx_ref.at[t0 + j], h_ref[t0 + j], part_refs[j]) for j in range(len(part_refs))]
        for k in range(PEER_SEL):
            for row, hv, part_ref in toks:
                p = tab_ref[row[k]].astype(F32) * hv
                part_ref[k * HALF_TILE:(k + 1) * HALF_TILE, :] = p[0:HALF_TILE] + p[HALF_TILE:]

    def row_sums(part_ref):
        part = part_ref[pl.ds(0, PEER_SEL, stride=HALF_TILE), :]
        for r in range(1, HALF_TILE):
            part = part + part_ref[pl.ds(r, PEER_SEL, stride=HALF_TILE), :]
        return part

    def finish(t, part):
        dots = jnp.sum(part.T, axis=0, keepdims=True)
        act_ref[t] = _gelu(dots) * gate_ref[t]

    products(0, (part_a_ref, part_b_ref))

    def pair(i, _):
        t = 2 * i
        sums_a, sums_b = row_sums(part_a_ref), row_sums(part_b_ref)
        finish(t - 2, sums_a)
        finish(t - 1, sums_b)
        products(t, (part_a_ref, part_b_ref))
        return 0

    lax.fori_loop(1, tt // 2, pair, 0)
    finish(tt - 2, row_sums(part_a_ref))
    finish(tt - 1, row_sums(part_b_ref))


def _peer_u(idx, hn3, gate3, tab, tt, first):
    rows = idx.shape[0]
    b0 = first // tt
    return pl.pallas_call(
        functools.partial(_peer_u_kernel, tt=tt),
        grid=((rows - first) // tt,),
        in_specs=[pl.BlockSpec((tt, PEER_SEL), lambda i: (i + b0, 0), memory_space=pltpu.SMEM),
                  pl.BlockSpec((tt, SUBLANES, LANES), lambda i: (i + b0, 0, 0)),
                  pl.BlockSpec((tt, 1, PEER_SEL), lambda i: (i + b0, 0, 0)),
                  pl.BlockSpec(memory_space=pl.ANY)],
        out_specs=pl.BlockSpec((tt, 1, PEER_SEL), lambda i: (i, 0, 0)),
        out_shape=jax.ShapeDtypeStruct((rows - first, 1, PEER_SEL), F32),
        scratch_shapes=[pltpu.VMEM(tab.shape, BF16),
                        pltpu.VMEM((PEER_SEL * HALF_TILE, LANES), F32),
                        pltpu.VMEM((PEER_SEL * HALF_TILE, LANES), F32),
                        pltpu.SemaphoreType.DMA],
        compiler_params=pltpu.CompilerParams(dimension_semantics=("arbitrary",),
                                             vmem_limit_bytes=VMEM_LIMIT_TABLE),
        name="peer_u",
    )(idx, hn3, gate3, tab)


def _peer_v_kernel(idx_ref, act_ref, h1_ref, tab_hbm, o_ref, tab_ref, tiles_ref, *rest, tt):
    rep_refs, sem = rest[:-1], rest[-1]
    _load_table(tab_hbm, tab_ref, sem)
    n_acc = 2
    g = PEER_V_GROUP

    def replicated(t):
        return jnp.broadcast_to(act_ref[t], (PEER_SEL, LANES)).T

    def accumulate(t0):
        rows = [idx_ref.at[t0 + j] for j in range(g)]
        accs = [[jnp.zeros((SUBLANES, LANES), F32)] * n_acc for _ in range(g)]
        for k in range(PEER_SEL):
            for j, rep_ref in enumerate(rep_refs):
                w = jnp.broadcast_to(rep_ref[k:k + 1, :], (SUBLANES, LANES))
                accs[j][k % n_acc] = accs[j][k % n_acc] + w * tab_ref[rows[j][k]].astype(F32)
        for j in range(g):
            tiles_ref[t0 + j] = h1_ref[t0 + j] + sum(accs[j][1:], accs[j][0])

    for j, rep_ref in enumerate(rep_refs):
        rep_ref[...] = replicated(j)

    def group(i, _):
        t = g * i
        nxt = [replicated(jnp.minimum(t + g + j, tt - 1)) for j in range(g)]
        accumulate(t)
        for rep_ref, tile in zip(rep_refs, nxt):
            rep_ref[...] = tile
        return 0

    lax.fori_loop(0, tt // g, group, 0)
    o_ref[...] = tiles_ref[...].reshape(tt, D_MODEL)


def _peer_v(idx, act3, h13, tab, tt, first):
    rows = idx.shape[0]
    b0 = first // tt
    tile = pl.BlockSpec((tt, SUBLANES, LANES), lambda i: (i + b0, 0, 0))
    return pl.pallas_call(
        functools.partial(_peer_v_kernel, tt=tt),
        grid=((rows - first) // tt,),
        in_specs=[pl.BlockSpec((tt, PEER_SEL), lambda i: (i + b0, 0), memory_space=pltpu.SMEM),
                  pl.BlockSpec((tt, 1, PEER_SEL), lambda i: (i + b0, 0, 0)),
                  tile, pl.BlockSpec(memory_space=pl.ANY)],
        out_specs=pl.BlockSpec((tt, D_MODEL), lambda i: (i, 0)),
        out_shape=jax.ShapeDtypeStruct((rows - first, D_MODEL), F32),
        scratch_shapes=[pltpu.VMEM(tab.shape, BF16), pltpu.VMEM((tt, SUBLANES, LANES), F32)]
                       + [pltpu.VMEM((PEER_SEL, LANES), F32)] * PEER_V_GROUP
                       + [pltpu.SemaphoreType.DMA],
        compiler_params=pltpu.CompilerParams(dimension_semantics=("arbitrary",),
                                             vmem_limit_bytes=VMEM_LIMIT_TABLE),
        name="peer_v",
    )(idx, act3, h13, tab)


SC_LANES = 16
SC_WORKERS = 32
SC_ROWS = 32
SC_COLS = 256
SC_COLS_U = 256
SC_ROW_GROUP = 4
SC_SHARE = (11, 32)


def _peer_v_sc_kernel(tab_hbm, idx_hbm, act_hbm, h1_hbm, out_hbm, idx_v, act_v, rows0_v, rows1_v, acc_v, sem0, sem1,
                      *, n_tok):
    wid = lax.axis_index("s") * 2 + lax.axis_index("c")
    bufs = ((rows0_v, sem0), (rows1_v, sem1))
    n_chunks = PEER_SEL // SC_ROWS

    def gather(c):
        rows_v, sem = bufs[c % 2]
        return pltpu.make_async_copy(tab_hbm.at[idx_v.at[pl.ds(c * SC_ROWS, SC_ROWS)]], rows_v, sem)

    @pl.loop(0, n_tok)
    def _(i):
        t = wid * n_tok + i
        pltpu.sync_copy(idx_hbm.at[pl.ds(t * PEER_SEL, PEER_SEL)], idx_v)
        gather(0).start()
        pltpu.sync_copy(act_hbm.at[pl.ds(t * PEER_SEL, PEER_SEL)], act_v)
        pltpu.sync_copy(h1_hbm.at[t], acc_v)
        for c in range(n_chunks):
            if c + 1 < n_chunks:
                gather(c + 1).start()
            gather(c).wait()
            rows_v = bufs[c % 2][0]

            @pl.loop(0, D_MODEL // SC_COLS)
            def _(cb):
                cols = [pl.ds(cb * SC_COLS + j * SC_LANES, SC_LANES) for j in range(SC_COLS // SC_LANES)]

                @pl.loop(0, SC_ROWS // SC_ROW_GROUP)
                def _(rg):
                    accs = [acc_v[col] for col in cols]
                    for rr in range(SC_ROW_GROUP):
                        r = rg * SC_ROW_GROUP + rr
                        w = plsc.load_gather(act_v, [jnp.full((SC_LANES,), c * SC_ROWS, I32) + r])
                        accs = [a + w * rows_v[r, col] for a, col in zip(accs, cols)]
                    for a, col in zip(accs, cols):
                        acc_v[col] = a
        pltpu.sync_copy(acc_v, out_hbm.at[t])


def _peer_v_sc(tab, idx_flat, act_flat, h1_rows):
    n = h1_rows.shape[0]
    mesh = plsc.VectorSubcoreMesh(core_axis_name="c", subcore_axis_name="s")
    return pl.kernel(
        functools.partial(_peer_v_sc_kernel, n_tok=n // SC_WORKERS),
        out_type=jax.ShapeDtypeStruct((n, D_MODEL), F32),
        mesh=mesh,
        scratch_types=[pltpu.VMEM((PEER_SEL,), I32), pltpu.VMEM((PEER_SEL,), F32),
                       pltpu.VMEM((SC_ROWS, D_MODEL), F32), pltpu.VMEM((SC_ROWS, D_MODEL), F32),
                       pltpu.VMEM((D_MODEL,), F32), pltpu.SemaphoreType.DMA, pltpu.SemaphoreType.DMA],
        compiler_params=pltpu.CompilerParams(needs_layout_passes=False),
        name="peer_v_sc",
    )(tab, idx_flat, act_flat, h1_rows)


def _peer_u_sc_kernel(tab_hbm, idx_hbm, hn_hbm, part_hbm, idx_v, h_v, rows0_v, rows1_v, part_v, sem0, sem1, *, n_tok):
    wid = lax.axis_index("s") * 2 + lax.axis_index("c")
    bufs = ((rows0_v, sem0), (rows1_v, sem1))
    n_chunks = PEER_SEL // SC_ROWS
    part_w = PEER_SEL * SC_LANES

    def gather(c):
        rows_v, sem = bufs[c % 2]
        return pltpu.make_async_copy(tab_hbm.at[idx_v.at[pl.ds(c * SC_ROWS, SC_ROWS)]], rows_v, sem)

    @pl.loop(0, n_tok)
    def _(i):
        t = wid * n_tok + i
        pltpu.sync_copy(idx_hbm.at[pl.ds(t * PEER_SEL, PEER_SEL)], idx_v)
        gather(0).start()
        pltpu.sync_copy(hn_hbm.at[t], h_v)

        @pl.loop(0, PEER_SEL)
        def _(k):
            part_v[pl.ds(k * SC_LANES, SC_LANES)] = jnp.zeros((SC_LANES,), F32)

        for c in range(n_chunks):
            if c + 1 < n_chunks:
                gather(c + 1).start()
            gather(c).wait()
            rows_v = bufs[c % 2][0]

            @pl.loop(0, D_MODEL // SC_COLS_U)
            def _(cb):
                cols = [pl.ds(cb * SC_COLS_U + j * SC_LANES, SC_LANES) for j in range(SC_COLS_U // SC_LANES)]
                hs = [h_v[col] for col in cols]

                @pl.loop(0, SC_ROWS // SC_ROW_GROUP)
                def _(rg):
                    for rr in range(SC_ROW_GROUP):
                        r = rg * SC_ROW_GROUP + rr
                        slot = pl.ds((c * SC_ROWS + r) * SC_LANES, SC_LANES)
                        prods = [h * rows_v[r, col] for h, col in zip(hs, cols)]
                        while len(prods) > 1:
                            prods = [a + b for a, b in zip(prods[0::2], prods[1::2])]
                        part_v[slot] = part_v[slot] + prods[0]
        pltpu.sync_copy(part_v, part_hbm.at[pl.ds(t * part_w, part_w)])


def _peer_u_sc(tab, idx_flat, hn_rows):
    n = hn_rows.shape[0]
    mesh = plsc.VectorSubcoreMesh(core_axis_name="c", subcore_axis_name="s")
    return pl.kernel(
        functools.partial(_peer_u_sc_kernel, n_tok=n // SC_WORKERS),
        out_type=jax.ShapeDtypeStruct((n * PEER_SEL * SC_LANES,), F32),
        mesh=mesh,
        scratch_types=[pltpu.VMEM((PEER_SEL,), I32), pltpu.VMEM((D_MODEL,), F32),
                       pltpu.VMEM((SC_ROWS, D_MODEL), F32), pltpu.VMEM((SC_ROWS, D_MODEL), F32),
                       pltpu.VMEM((PEER_SEL * SC_LANES,), F32), pltpu.SemaphoreType.DMA, pltpu.SemaphoreType.DMA],
        compiler_params=pltpu.CompilerParams(needs_layout_passes=False),
        name="peer_u_sc",
    )(tab, idx_flat, hn_rows)


def _fold_act_kernel(part_ref, fold_ref, gate_ref, act_ref):
    p = part_ref[...]
    hi = p.astype(BF16)
    lo = (p - hi.astype(F32)).astype(BF16)
    dots = _dot(hi, fold_ref[...]) + _dot(lo, fold_ref[...])
    act_ref[...] = _gelu(dots) * gate_ref[...]


def _fold_act(part2, gate, tm):
    n = part2.shape[0]
    fold = (jnp.arange(PEER_SEL * SC_LANES)[:, None] // SC_LANES == jnp.arange(PEER_SEL)[None, :]).astype(BF16)
    return pl.pallas_call(
        _fold_act_kernel,
        grid=(n // tm,),
        in_specs=[pl.BlockSpec((tm, PEER_SEL * SC_LANES), lambda i: (i, 0)),
                  pl.BlockSpec(fold.shape, lambda i: (0, 0)),
                  pl.BlockSpec((tm, PEER_SEL), lambda i: (i, 0))],
        out_specs=pl.BlockSpec((tm, PEER_SEL), lambda i: (i, 0)),
        out_shape=jax.ShapeDtypeStruct((n, PEER_SEL), F32),
        compiler_params=pltpu.CompilerParams(dimension_semantics=("arbitrary",)),
        name="fold_act",
    )(part2, fold, gate)


def _block(x2, meta_tokens, norm1_g, w_in, q_norm_g, k_norm_g, attn_sinks, ssm_a_re, ssm_a_im, ssm_log_dt,
           ssm_b_re, ssm_b_im, ssm_c_re, ssm_c_im, ssm_d, ssm_glu_w, ssm_glu_b, attn_out_g, ssm_out_g, w_out,
           norm2_g, peer_w_query, peer_sub_keys, peer_u, peer_v, *, tm_proj, tm_ssm, tm_mix, tt_peer):
    rows = x2.shape[0]
    row2 = lambda a: a.reshape(1, -1).astype(F32)
    w_in_bf = w_in.astype(BF16)
    seg = jnp.arange(QK_WIDTH) // HEAD_DIM
    ones_bd = (seg[:, None] == seg[None, :]).astype(BF16)
    scale = HEAD_DIM ** -0.5
    qkg = jnp.concatenate([jnp.tile(q_norm_g.astype(F32), N_Q_HEADS) * scale,
                           jnp.tile(k_norm_g.astype(F32), N_KV_HEADS)]).reshape(1, QK_WIDTH)
    bmat, cmat, coef = _ssm_params(ssm_a_re.astype(F32), ssm_a_im.astype(F32), ssm_log_dt.astype(F32),
                                   ssm_b_re.astype(F32), ssm_b_im.astype(F32), ssm_c_re.astype(F32),
                                   ssm_c_im.astype(F32))
    glu_w_bf = ssm_glu_w.astype(BF16)
    ssm_args = (row2(ssm_d), glu_w_bf, row2(ssm_glu_b), row2(ssm_out_g))

    h0 = jnp.concatenate([jnp.zeros((BLOCK - N_META, D_MODEL), F32), meta_tokens.astype(F32)], axis=0)
    _, k0, vt0, u0 = _in_proj(h0, row2(norm1_g), w_in_bf, ones_bd, qkg, BLOCK)
    carry0 = jnp.zeros((2, 2, SUBLANES, SSM_HALF_STATES), F32)
    _, carry = _ssm(u0, bmat, cmat, coef, carry0, *ssm_args, BLOCK)

    qt, k, vt, u = _in_proj(x2, row2(norm1_g), w_in_bf, ones_bd, qkg, tm_proj)
    g_attn = jnp.broadcast_to(attn_out_g.astype(F32)[:, None], (ATTN_WIDTH, BLOCK))
    attn_n = _attention(attn_sinks.astype(F32), qt, k, vt, k0[BLOCK - N_META:], vt0[:, BLOCK - N_META:], g_attn)
    ssm_n, _ = _ssm(u, bmat, cmat, coef, carry, *ssm_args, tm_ssm)

    unit = math.lcm(SC_WORKERS, tt_peer, tm_mix)
    n_sc = rows * SC_SHARE[0] // SC_SHARE[1] // unit * unit
    mix_args = (x2, attn_n, ssm_n, w_out.astype(BF16), row2(norm2_g), peer_w_query.astype(BF16),
                peer_sub_keys.astype(BF16), tm_mix)
    h1_a, hn_a, idx_a, gate_a = _mix(*mix_args, 0, n_sc)
    h1_b, hn_b, idx_b, gate_b = _mix(*mix_args, n_sc, rows - n_sc)
    part = _peer_u_sc(peer_u.astype(F32), idx_a.reshape(-1), hn_a.reshape(n_sc, D_MODEL))
    act_b = _peer_u(idx_b, hn_b, gate_b.reshape(rows - n_sc, 1, PEER_SEL), _expert_tiles(peer_u), tt_peer, 0)
    act_a = _fold_act(part.reshape(n_sc, PEER_SEL * SC_LANES), gate_a, tm_mix)
    out_a = _peer_v_sc(peer_v.astype(F32), idx_a.reshape(-1), act_a.reshape(-1), h1_a.reshape(n_sc, D_MODEL))
    out_b = _peer_v(idx_b, act_b, h1_b, _expert_tiles(peer_v), tt_peer, 0)
    return jnp.concatenate([out_a, out_b], axis=0)


def kernel(x, meta_tokens, norm1_g, w_in, q_norm_g, k_norm_g, attn_sinks, ssm_a_re, ssm_a_im, ssm_log_dt, ssm_b_re, ssm_b_im, ssm_c_re, ssm_c_im, ssm_d, ssm_glu_w, ssm_glu_b, attn_out_g, ssm_out_g, w_out, norm2_g, peer_w_query, peer_sub_keys, peer_u, peer_v):
    b, seq, d = x.shape
    outs = []
    for bi in range(b):
        outs.append(_block(x[bi].astype(F32), meta_tokens, norm1_g[0], w_in[0], q_norm_g[0], k_norm_g[0],
                           attn_sinks[0], ssm_a_re[0], ssm_a_im[0], ssm_log_dt[0], ssm_b_re[0], ssm_b_im[0],
                           ssm_c_re[0], ssm_c_im[0], ssm_d[0], ssm_glu_w[0], ssm_glu_b[0], attn_out_g[0],
                           ssm_out_g[0], w_out[0], norm2_g[0], peer_w_query[0], peer_sub_keys[0], peer_u[0],
                           peer_v[0], tm_proj=512, tm_ssm=256, tm_mix=256, tt_peer=32))
    return jnp.stack(outs).astype(x.dtype)
```

```python
import functools
import math

import jax
import jax.numpy as jnp
from jax import lax
from jax.experimental import pallas as pl
from jax.experimental.pallas import tpu as pltpu
from jax.experimental.pallas import tpu_sc as plsc

F32 = jnp.float32
BF16 = jnp.bfloat16
I32 = jnp.int32

D_MODEL = 1024
N_META = 16
BLOCK = 128
ATTN_WIDTH = 512
HEAD_DIM = 64
N_Q_HEADS = 8
N_KV_HEADS = 2
GQA = 4
KV_WIDTH = 128
QK_WIDTH = ATTN_WIDTH + KV_WIDTH
SSM_WIDTH = 512
SSM_CH = 16
SSM_GROUPS = 32
SSM_STATE = 64
IN_WIDTH = ATTN_WIDTH + 2 * KV_WIDTH + SSM_WIDTH
PEER_HEADS = 8
PEER_N_KEYS = 128
PEER_TOPK = 16
PEER_DK = 256
PEER_HALF = 128
PEER_SEL = PEER_HEADS * PEER_TOPK
NORM_EPS = 1e-6
MASK_VALUE = -1e30

SUBLANES = 8
LANES = 128
SSM_HALF_W = SSM_WIDTH // 2
SSM_HALF_STATES = SSM_GROUPS // 2 * SSM_STATE
HALF_TILE = SUBLANES // 2
PEER_V_GROUP = 2
VMEM_LIMIT_TABLE = 52 * 1024 * 1024


def _dot(a, b):
    return jnp.dot(a, b, preferred_element_type=F32)


def _dot_nt(a, b):
    return lax.dot_general(a, b, (((1,), (1,)), ((), ())), preferred_element_type=F32)


def _gelu(x):
    return 0.5 * x * (1.0 + lax.erf(x * (2.0 ** -0.5)))


def _in_proj_kernel(x_ref, g1_ref, w_ref, ones_ref, qkg_ref, qt_ref, k_ref, vt_ref, u_ref):
    x = x_ref[...]
    ms = jnp.mean(x * x, axis=-1, keepdims=True)
    y = (x * lax.rsqrt(ms + NORM_EPS) * g1_ref[...]).astype(BF16)
    proj = _dot(y, w_ref[...])
    qk = proj[:, :QK_WIDTH]
    ss = _dot((qk * qk).astype(BF16), ones_ref[...]) * (1.0 / HEAD_DIM)
    qkn = qk * lax.rsqrt(ss + NORM_EPS) * qkg_ref[...]
    qt_ref[...] = qkn[:, :ATTN_WIDTH].T.astype(BF16)
    k_ref[...] = qkn[:, ATTN_WIDTH:].astype(BF16)
    vt_ref[...] = proj[:, QK_WIDTH:QK_WIDTH + KV_WIDTH].T.astype(BF16)
    u_ref[...] = proj[:, QK_WIDTH + KV_WIDTH:]


def _in_proj(x, g1, w_in_bf, ones_bd, qkg, tm):
    rows = x.shape[0]
    row_spec = lambda w: pl.BlockSpec((tm, w), lambda i: (i, 0))
    col_spec = lambda w: pl.BlockSpec((w, tm), lambda i: (0, i))
    full = lambda a: pl.BlockSpec(a.shape, lambda i: (0,) * a.ndim)
    return pl.pallas_call(
        _in_proj_kernel,
        grid=(rows // tm,),
        in_specs=[row_spec(D_MODEL), full(g1), full(w_in_bf), full(ones_bd), full(qkg)],
        out_specs=[col_spec(ATTN_WIDTH), row_spec(KV_WIDTH), col_spec(KV_WIDTH), row_spec(SSM_WIDTH)],
        out_shape=[jax.ShapeDtypeStruct((ATTN_WIDTH, rows), BF16),
                   jax.ShapeDtypeStruct((rows, KV_WIDTH), BF16),
                   jax.ShapeDtypeStruct((KV_WIDTH, rows), BF16),
                   jax.ShapeDtypeStruct((rows, SSM_WIDTH), F32)],
        compiler_params=pltpu.CompilerParams(dimension_semantics=("arbitrary",)),
        name="in_proj",
    )(x, g1, w_in_bf, ones_bd, qkg)


def _attn_kernel(sink_ref, qt_ref, kp_ref, kc_ref, vtp_ref, vtc_ref, km_ref, vtm_ref, g_ref, o_ref, acc_ref):
    i = pl.program_id(0)
    group_w = GQA * BLOCK
    key = lax.broadcasted_iota(I32, (BLOCK, group_w), 0)
    qry = lax.broadcasted_iota(I32, (BLOCK, group_w), 1) & (BLOCK - 1)
    prev_ok = jnp.logical_and(key > qry, i > 0)
    cur_ok = key <= qry
    for kv in range(N_KV_HEADS):
        hs = slice(kv * HEAD_DIM, (kv + 1) * HEAD_DIM)
        heads = range(kv * GQA, (kv + 1) * GQA)
        qt = jnp.concatenate([qt_ref[h * HEAD_DIM:(h + 1) * HEAD_DIM, :] for h in heads], axis=1)
        sink = jnp.concatenate([jnp.full((1, BLOCK), sink_ref[h], F32) for h in heads], axis=1)
        sp = jnp.where(prev_ok, _dot(kp_ref[:, hs], qt), MASK_VALUE)
        sc = jnp.where(cur_ok, _dot(kc_ref[:, hs], qt), MASK_VALUE)
        sm = _dot(km_ref[:, hs], qt)
        m = jnp.maximum(jnp.maximum(jnp.max(sp, axis=0, keepdims=True), jnp.max(sc, axis=0, keepdims=True)),
                        jnp.maximum(jnp.max(sm, axis=0, keepdims=True), sink))
        pp, pc, pm = jnp.exp(sp - m), jnp.exp(sc - m), jnp.exp(sm - m)
        denom = (jnp.sum(pp, axis=0, keepdims=True) + jnp.sum(pc, axis=0, keepdims=True)
                 + jnp.sum(pm, axis=0, keepdims=True) + jnp.exp(sink - m))
        ot = (_dot(vtp_ref[hs, :], pp.astype(BF16)) + _dot(vtc_ref[hs, :], pc.astype(BF16))
              + _dot(vtm_ref[hs, :], pm.astype(BF16))) / denom
        for g, h in enumerate(heads):
            acc_ref[h * HEAD_DIM:(h + 1) * HEAD_DIM, :] = ot[:, g * BLOCK:(g + 1) * BLOCK]
    a = acc_ref[...]
    ms = jnp.mean(a * a, axis=0, keepdims=True)
    o_ref[...] = (a * lax.rsqrt(ms + NORM_EPS) * g_ref[...]).T.astype(BF16)


def _attention(sinks, qt, k, vt, k_meta, vt_meta, g_out):
    nb = k.shape[0] // BLOCK
    prev_i = lambda i: jnp.maximum(i - 1, 0)
    full = lambda a: pl.BlockSpec(a.shape, lambda i: (0,) * a.ndim)
    return pl.pallas_call(
        _attn_kernel,
        grid=(nb,),
        in_specs=[pl.BlockSpec(memory_space=pltpu.SMEM),
                  pl.BlockSpec((ATTN_WIDTH, BLOCK), lambda i: (0, i)),
                  pl.BlockSpec((BLOCK, KV_WIDTH), lambda i: (prev_i(i), 0)),
                  pl.BlockSpec((BLOCK, KV_WIDTH), lambda i: (i, 0)),
                  pl.BlockSpec((KV_WIDTH, BLOCK), lambda i: (0, prev_i(i))),
                  pl.BlockSpec((KV_WIDTH, BLOCK), lambda i: (0, i)),
                  full(k_meta), full(vt_meta), full(g_out)],
        out_specs=pl.BlockSpec((BLOCK, ATTN_WIDTH), lambda i: (i, 0)),
        out_shape=jax.ShapeDtypeStruct((k.shape[0], ATTN_WIDTH), BF16),
        scratch_shapes=[pltpu.VMEM((ATTN_WIDTH, BLOCK), F32)],
        compiler_params=pltpu.CompilerParams(dimension_semantics=("arbitrary",)),
        name="attention",
    )(sinks, qt, k, k, vt, vt, k_meta, vt_meta, g_out)


def _ssm_kernel(u_ref, bmat_ref, cmat_ref, coef_ref, carry_in_ref, d_ref, gw_ref, gb_ref, g_ref,
                o_ref, carry_out_ref, st_ref, carry_ref, y_ref, *, tm):
    @pl.when(pl.program_id(0) == 0)
    def _():
        carry_ref[...] = carry_in_ref[...]

    ns = SSM_HALF_STATES
    for h in range(2):
        uh = u_ref[:, h * SSM_HALF_W:(h + 1) * SSM_HALF_W].astype(BF16)
        st_ref[...] = _dot(uh, bmat_ref[h])

        def block(b, carry):
            cr, ci = carry
            r0 = pl.multiple_of(b * SUBLANES, SUBLANES)
            xr = st_ref[pl.ds(r0, SUBLANES), 0:ns]
            xi = st_ref[pl.ds(r0, SUBLANES), ns:2 * ns]
            for j, shift in enumerate((1, 2, 4)):
                ar, ai = coef_ref[h, j, 0], coef_ref[h, j, 1]
                sr, si = pltpu.roll(xr, shift, 0), pltpu.roll(xi, shift, 0)
                xr, xi = xr + ar * sr - ai * si, xi + ar * si + ai * sr
            pr, pi = coef_ref[h, 3, 0], coef_ref[h, 3, 1]
            xr, xi = xr + pr * cr - pi * ci, xi + pr * ci + pi * cr
            st_ref[pl.ds(r0, SUBLANES), 0:ns] = xr
            st_ref[pl.ds(r0, SUBLANES), ns:2 * ns] = xi
            return (jnp.broadcast_to(xr[SUBLANES - 1:SUBLANES, :], (SUBLANES, ns)),
                    jnp.broadcast_to(xi[SUBLANES - 1:SUBLANES, :], (SUBLANES, ns)))

        cr, ci = lax.fori_loop(0, tm // SUBLANES, block, (carry_ref[h, 0], carry_ref[h, 1]))
        carry_ref[h, 0] = cr
        carry_ref[h, 1] = ci
        y_ref[:, h * SSM_HALF_W:(h + 1) * SSM_HALF_W] = _dot(st_ref[...].astype(BF16), cmat_ref[h])

    carry_out_ref[...] = carry_ref[...]
    z = _gelu(y_ref[...] + d_ref[...] * u_ref[...])
    z = z * jax.nn.sigmoid(_dot(z.astype(BF16), gw_ref[...]) + gb_ref[...])
    ms = jnp.mean(z * z, axis=-1, keepdims=True)
    o_ref[...] = (z * lax.rsqrt(ms + NORM_EPS) * g_ref[...]).astype(BF16)


def _ssm(u, bmat, cmat, coef, carry_in, d, gw_bf, gb, g_out, tm):
    rows = u.shape[0]
    row_spec = pl.BlockSpec((tm, SSM_WIDTH), lambda i: (i, 0))
    full = lambda a: pl.BlockSpec(a.shape, lambda i: (0,) * a.ndim)
    return pl.pallas_call(
        functools.partial(_ssm_kernel, tm=tm),
        grid=(rows // tm,),
        in_specs=[row_spec, full(bmat), full(cmat), full(coef), full(carry_in), full(d), full(gw_bf), full(gb),
                  full(g_out)],
        out_specs=[row_spec, full(carry_in)],
        out_shape=[jax.ShapeDtypeStruct((rows, SSM_WIDTH), BF16),
                   jax.ShapeDtypeStruct(carry_in.shape, F32)],
        scratch_shapes=[pltpu.VMEM((tm, 2 * SSM_HALF_STATES), F32),
                        pltpu.VMEM(carry_in.shape, F32),
                        pltpu.VMEM((tm, SSM_WIDTH), F32)],
        compiler_params=pltpu.CompilerParams(dimension_semantics=("arbitrary",)),
        name="ssm",
    )(u, bmat, cmat, coef, carry_in, d, gw_bf, gb, g_out)


def _ssm_params(a_re, a_im, log_dt, b_re, b_im, c_re, c_im):
    dt = jnp.exp(log_dt)[:, None]
    mag = jnp.exp(a_re * dt)
    abar_r, abar_i = mag * jnp.cos(a_im * dt), mag * jnp.sin(a_im * dt)
    den = a_re * a_re + a_im * a_im
    nr, ni = abar_r - 1.0, abar_i
    coef_r = (nr * a_re + ni * a_im) / den
    coef_i = (ni * a_re - nr * a_im) / den
    bbar_r = coef_r[..., None] * b_re - coef_i[..., None] * b_im
    bbar_i = coef_r[..., None] * b_im + coef_i[..., None] * b_re
    gh = SSM_GROUPS // 2
    eye = jnp.eye(gh, dtype=F32)

    def half_b(bb):
        return jnp.einsum('gnp,gk->gpkn', bb, eye).reshape(gh * SSM_CH, gh * SSM_STATE)

    def half_c(cc):
        return jnp.einsum('gpn,gk->gnkp', cc, eye).reshape(gh * SSM_STATE, gh * SSM_CH)

    bmat = jnp.stack([jnp.concatenate([half_b(bbar_r[s]), half_b(bbar_i[s])], axis=1)
                      for s in (slice(0, gh), slice(gh, None))]).astype(BF16)
    cmat = jnp.stack([jnp.concatenate([half_c(c_re[s]), -half_c(c_im[s])], axis=0)
                      for s in (slice(0, gh), slice(gh, None))]).astype(BF16)

    def cpow(k):
        m = mag ** k
        return m * jnp.cos(a_im * dt * k), m * jnp.sin(a_im * dt * k)

    rows = jnp.arange(SUBLANES)
    coefs = []
    for shift in (1, 2, 4):
        pr, pi = cpow(float(shift))
        keep = (rows >= shift).astype(F32)[:, None, None]
        coefs.append(jnp.stack([keep * pr[None], keep * pi[None]]))
    pw = (rows + 1).astype(F32)[:, None, None]
    m = mag[None] ** pw
    coefs.append(jnp.stack([m * jnp.cos(a_im[None] * dt[None] * pw), m * jnp.sin(a_im[None] * dt[None] * pw)]))
    coef = jnp.stack(coefs)
    coef = coef.reshape(4, 2, SUBLANES, 2, SSM_HALF_STATES).transpose(3, 0, 1, 2, 4)
    return bmat, cmat, coef


def _top16(s, payload, n_rows):
    rowid = lax.broadcasted_iota(I32, s.shape, 0).astype(F32)
    vals, pays = [], []
    for _ in range(PEER_TOPK):
        m = jnp.max(s, axis=0, keepdims=True)
        pos = jnp.min(jnp.where(s == m, rowid, float(n_rows)), axis=0, keepdims=True)
        sel = rowid == pos
        vals.append(m)
        pays.append(pos if payload is None else jnp.max(jnp.where(sel, payload, -1.0), axis=0, keepdims=True))
        s = jnp.where(sel, -jnp.inf, s)
    return jnp.concatenate(vals, axis=0), jnp.concatenate(pays, axis=0)


_CAND_PAIRS = [(a, b) for a in range(PEER_TOPK) for b in range(PEER_TOPK) if (a + 1) * (b + 1) <= PEER_TOPK]
_CAND_ROWS = -(-len(_CAND_PAIRS) // SUBLANES) * SUBLANES


def _mix_kernel(x_ref, a_ref, s_ref, wo_ref, g2_ref, wq_ref, keys_ref, h1_ref, hn_ref, idx_ref, gate_ref,
                qp_ref, idx_t_ref, gate_t_ref, *, tm):
    mix = _dot(a_ref[...], wo_ref[0:ATTN_WIDTH, :]) + _dot(s_ref[...], wo_ref[ATTN_WIDTH:, :])
    h1 = x_ref[...] + mix
    h1_ref[...] = h1.reshape(tm, SUBLANES, LANES)
    ms = jnp.mean(h1 * h1, axis=-1, keepdims=True)
    hn = h1 * lax.rsqrt(ms + NORM_EPS) * g2_ref[...]
    hn_ref[...] = hn.reshape(tm, SUBLANES, LANES)
    qp_ref[...] = _dot(hn.astype(BF16), wq_ref[...])

    def retrieve(h, tok):
        c0 = pl.multiple_of(h * PEER_DK, PEER_DK)
        q1 = qp_ref[tok, pl.ds(c0, PEER_HALF)].astype(BF16)
        q2 = qp_ref[tok, pl.ds(c0 + PEER_HALF, PEER_HALF)].astype(BF16)
        s1 = _dot_nt(keys_ref[h, 0], q1)
        s2 = _dot_nt(keys_ref[h, 1], q2)
        v1, i1 = _top16(s1, None, PEER_N_KEYS)
        v2, i2 = _top16(s2, None, PEER_N_KEYS)
        pad = _CAND_ROWS - len(_CAND_PAIRS)
        cand = jnp.concatenate([v1[a:a + 1, :] + v2[b:b + 1, :] for a, b in _CAND_PAIRS]
                               + [jnp.full((pad, LANES), -jnp.inf, F32)], axis=0)
        eidx = jnp.concatenate([i1[a:a + 1, :] * float(PEER_N_KEYS) + i2[b:b + 1, :] for a, b in _CAND_PAIRS]
                               + [jnp.zeros((pad, LANES), F32)], axis=0)
        sc, ei = _top16(cand, eidx, _CAND_ROWS)
        p = jnp.exp(sc - sc[0:1, :])
        r0 = pl.multiple_of(h * PEER_TOPK, PEER_TOPK)
        idx_t_ref[pl.ds(r0, PEER_TOPK), tok] = ei.astype(I32)
        gate_t_ref[pl.ds(r0, PEER_TOPK), tok] = p / jnp.sum(p, axis=0, keepdims=True)

    def head_pair(i, _):
        for h in (2 * i, 2 * i + 1):
            for sub in range(tm // LANES):
                retrieve(h, slice(sub * LANES, (sub + 1) * LANES))
        return 0

    lax.fori_loop(0, PEER_HEADS // 2, head_pair, 0)
    idx_ref[...] = idx_t_ref[...].T
    gate_ref[...] = gate_t_ref[...].T


def _mix(x, attn_n, ssm_n, w_out_bf, g2, wq_bf, keys_bf, tm):
    rows = x.shape[0]
    row_spec = lambda w: pl.BlockSpec((tm, w), lambda i: (i, 0))
    tile_spec = pl.BlockSpec((tm, SUBLANES, LANES), lambda i: (i, 0, 0))
    full = lambda a: pl.BlockSpec(a.shape, lambda i: (0,) * a.ndim)
    return pl.pallas_call(
        functools.partial(_mix_kernel, tm=tm),
        grid=(rows // tm,),
        in_specs=[row_spec(D_MODEL), row_spec(ATTN_WIDTH), row_spec(SSM_WIDTH), full(w_out_bf), full(g2),
                  full(wq_bf), full(keys_bf)],
        out_specs=[tile_spec, tile_spec, row_spec(PEER_SEL), row_spec(PEER_SEL)],
        out_shape=[jax.ShapeDtypeStruct((rows, SUBLANES, LANES), F32),
                   jax.ShapeDtypeStruct((rows, SUBLANES, LANES), F32),
                   jax.ShapeDtypeStruct((rows, PEER_SEL), I32),
                   jax.ShapeDtypeStruct((rows, PEER_SEL), F32)],
        scratch_shapes=[pltpu.VMEM((tm, PEER_HEADS * PEER_DK), F32),
                        pltpu.VMEM((PEER_SEL, tm), I32),
                        pltpu.VMEM((PEER_SEL, tm), F32)],
        compiler_params=pltpu.CompilerParams(dimension_semantics=("arbitrary",),
                                             vmem_limit_bytes=48 * 1024 * 1024),
        name="mix_topk",
    )(x, attn_n, ssm_n, w_out_bf, g2, wq_bf, keys_bf)


def _expert_tiles(t):
    return t.astype(BF16).reshape(t.shape[0], SUBLANES, LANES)


def _load_table(tab_hbm, tab_ref, sem):
    @pl.when(pl.program_id(0) == 0)
    def _():
        cp = pltpu.make_async_copy(tab_hbm, tab_ref, sem)
        cp.start()
        cp.wait()


def _peer_u_kernel(idx_ref, h_ref, gate_ref, tab_hbm, act_ref, tab_ref, part_a_ref, part_b_ref, sem, *, tt):
    _load_table(tab_hbm, tab_ref, sem)

    def products(t0, part_refs):
        toks = [(idx_ref.at[t0 + j], h_ref[t0 + j], part_refs[j]) for j in range(len(part_refs))]
        for k in range(PEER_SEL):
            for row, hv, part_ref in toks:
                p = tab_ref[row[k]].astype(F32) * hv
                part_ref[k * HALF_TILE:(k + 1) * HALF_TILE, :] = p[0:HALF_TILE] + p[HALF_TILE:]

    def row_sums(part_ref):
        part = part_ref[pl.ds(0, PEER_SEL, stride=HALF_TILE), :]
        for r in range(1, HALF_TILE):
            part = part + part_ref[pl.ds(r, PEER_SEL, stride=HALF_TILE), :]
        return part

    def finish(t, part):
        dots = jnp.sum(part.T, axis=0, keepdims=True)
        act_ref[t] = _gelu(dots) * gate_ref[t]

    products(0, (part_a_ref, part_b_ref))

    def pair(i, _):
        t = 2 * i
        sums_a, sums_b = row_sums(part_a_ref), row_sums(part_b_ref)
        finish(t - 2, sums_a)
        finish(t - 1, sums_b)
        products(t, (part_a_ref, part_b_ref))
        return 0

    lax.fori_loop(1, tt // 2, pair, 0)
    finish(tt - 2, row_sums(part_a_ref))
    finish(tt - 1, row_sums(part_b_ref))


def _peer_u(idx, hn3, gate3, tab, tt, first):
    rows = idx.shape[0]
    b0 = first // tt
    return pl.pallas_call(
        functools.partial(_peer_u_kernel, tt=tt),
        grid=((rows - first) // tt,),
        in_specs=[pl.BlockSpec((tt, PEER_SEL), lambda i: (i + b0, 0), memory_space=pltpu.SMEM),
                  pl.BlockSpec((tt, SUBLANES, LANES), lambda i: (i + b0, 0, 0)),
                  pl.BlockSpec((tt, 1, PEER_SEL), lambda i: (i + b0, 0, 0)),
                  pl.BlockSpec(memory_space=pl.ANY)],
        out_specs=pl.BlockSpec((tt, 1, PEER_SEL), lambda i: (i, 0, 0)),
        out_shape=jax.ShapeDtypeStruct((rows - first, 1, PEER_SEL), F32),
        scratch_shapes=[pltpu.VMEM(tab.shape, BF16),
                        pltpu.VMEM((PEER_SEL * HALF_TILE, LANES), F32),
                        pltpu.VMEM((PEER_SEL * HALF_TILE, LANES), F32),
                        pltpu.SemaphoreType.DMA],
        compiler_params=pltpu.CompilerParams(dimension_semantics=("arbitrary",),
                                             vmem_limit_bytes=VMEM_LIMIT_TABLE),
        name="peer_u",
    )(idx, hn3, gate3, tab)


def _peer_v_kernel(idx_ref, act_ref, h1_ref, tab_hbm, o_ref, tab_ref, tiles_ref, *rest, tt):
    rep_refs, sem = rest[:-1], rest[-1]
    _load_table(tab_hbm, tab_ref, sem)
    n_acc = 2
    g = PEER_V_GROUP

    def replicated(t):
        return jnp.broadcast_to(act_ref[t], (PEER_SEL, LANES)).T

    def accumulate(t0):
        rows = [idx_ref.at[t0 + j] for j in range(g)]
        accs = [[jnp.zeros((SUBLANES, LANES), F32)] * n_acc for _ in range(g)]
        for k in range(PEER_SEL):
            for j, rep_ref in enumerate(rep_refs):
                w = jnp.broadcast_to(rep_ref[k:k + 1, :], (SUBLANES, LANES))
                accs[j][k % n_acc] = accs[j][k % n_acc] + w * tab_ref[rows[j][k]].astype(F32)
        for j in range(g):
            tiles_ref[t0 + j] = h1_ref[t0 + j] + sum(accs[j][1:], accs[j][0])

    for j, rep_ref in enumerate(rep_refs):
        rep_ref[...] = replicated(j)

    def group(i, _):
        t = g * i
        nxt = [replicated(jnp.minimum(t + g + j, tt - 1)) for j in range(g)]
        accumulate(t)
        for rep_ref, tile in zip(rep_refs, nxt):
            rep_ref[...] = tile
        return 0

    lax.fori_loop(0, tt // g, group, 0)
    o_ref[...] = tiles_ref[...].reshape(tt, D_MODEL)


def _peer_v(idx, act3, h13, tab, tt, first):
    rows = idx.shape[0]
    b0 = first // tt
    tile = pl.BlockSpec((tt, SUBLANES, LANES), lambda i: (i + b0, 0, 0))
    return pl.pallas_call(
        functools.partial(_peer_v_kernel, tt=tt),
        grid=((rows - first) // tt,),
        in_specs=[pl.BlockSpec((tt, PEER_SEL), lambda i: (i + b0, 0), memory_space=pltpu.SMEM),
                  pl.BlockSpec((tt, 1, PEER_SEL), lambda i: (i + b0, 0, 0)),
                  tile, pl.BlockSpec(memory_space=pl.ANY)],
        out_specs=pl.BlockSpec((tt, D_MODEL), lambda i: (i, 0)),
        out_shape=jax.ShapeDtypeStruct((rows - first, D_MODEL), F32),
        scratch_shapes=[pltpu.VMEM(tab.shape, BF16), pltpu.VMEM((tt, SUBLANES, LANES), F32)]
                       + [pltpu.VMEM((PEER_SEL, LANES), F32)] * PEER_V_GROUP
                       + [pltpu.SemaphoreType.DMA],
        compiler_params=pltpu.CompilerParams(dimension_semantics=("arbitrary",),
                                             vmem_limit_bytes=VMEM_LIMIT_TABLE),
        name="peer_v",
    )(idx, act3, h13, tab)


SC_LANES = 16
SC_WORKERS = 32
SC_ROWS = 32
SC_COLS = 256
SC_COLS_U = 256
SC_ROW_GROUP = 4
SC_SHARE_U = (17, 64)
SC_SHARE_V = (23, 64)


def _peer_v_sc_kernel(tab_hbm, idx_hbm, act_hbm, h1_hbm, out_hbm, idx_v, act_v, rows0_v, rows1_v, acc_v, sem0, sem1,
                      *, n_tok):
    wid = lax.axis_index("s") * 2 + lax.axis_index("c")
    bufs = ((rows0_v, sem0), (rows1_v, sem1))
    n_chunks = PEER_SEL // SC_ROWS

    def gather(c):
        rows_v, sem = bufs[c % 2]
        return pltpu.make_async_copy(tab_hbm.at[idx_v.at[pl.ds(c * SC_ROWS, SC_ROWS)]], rows_v, sem)

    @pl.loop(0, n_tok)
    def _(i):
        t = wid * n_tok + i
        pltpu.sync_copy(idx_hbm.at[pl.ds(t * PEER_SEL, PEER_SEL)], idx_v)
        gather(0).start()
        pltpu.sync_copy(act_hbm.at[pl.ds(t * PEER_SEL, PEER_SEL)], act_v)
        pltpu.sync_copy(h1_hbm.at[t], acc_v)
        for c in range(n_chunks):
            if c + 1 < n_chunks:
                gather(c + 1).start()
            gather(c).wait()
            rows_v = bufs[c % 2][0]

            @pl.loop(0, D_MODEL // SC_COLS)
            def _(cb):
                cols = [pl.ds(cb * SC_COLS + j * SC_LANES, SC_LANES) for j in range(SC_COLS // SC_LANES)]

                @pl.loop(0, SC_ROWS // SC_ROW_GROUP)
                def _(rg):
                    accs = [acc_v[col] for col in cols]
                    for rr in range(SC_ROW_GROUP):
                        r = rg * SC_ROW_GROUP + rr
                        w = plsc.load_gather(act_v, [jnp.full((SC_LANES,), c * SC_ROWS, I32) + r])
                        accs = [a + w * rows_v[r, col] for a, col in zip(accs, cols)]
                    for a, col in zip(accs, cols):
                        acc_v[col] = a
        pltpu.sync_copy(acc_v, out_hbm.at[t])


def _peer_v_sc(tab, idx_flat, act_flat, h1_rows):
    n = h1_rows.shape[0]
    mesh = plsc.VectorSubcoreMesh(core_axis_name="c", subcore_axis_name="s")
    return pl.kernel(
        functools.partial(_peer_v_sc_kernel, n_tok=n // SC_WORKERS),
        out_type=jax.ShapeDtypeStruct((n, D_MODEL), F32),
        mesh=mesh,
        scratch_types=[pltpu.VMEM((PEER_SEL,), I32), pltpu.VMEM((PEER_SEL,), F32),
                       pltpu.VMEM((SC_ROWS, D_MODEL), F32), pltpu.VMEM((SC_ROWS, D_MODEL), F32),
                       pltpu.VMEM((D_MODEL,), F32), pltpu.SemaphoreType.DMA, pltpu.SemaphoreType.DMA],
        compiler_params=pltpu.CompilerParams(needs_layout_passes=False),
        name="peer_v_sc",
    )(tab, idx_flat, act_flat, h1_rows)


def _peer_u_sc_kernel(tab_hbm, idx_hbm, hn_hbm, part_hbm, idx_v, h_v, rows0_v, rows1_v, part_v, sem0, sem1, *, n_tok):
    wid = lax.axis_index("s") * 2 + lax.axis_index("c")
    bufs = ((rows0_v, sem0), (rows1_v, sem1))
    n_chunks = PEER_SEL // SC_ROWS
    part_w = PEER_SEL * SC_LANES

    def gather(c):
        rows_v, sem = bufs[c % 2]
        return pltpu.make_async_copy(tab_hbm.at[idx_v.at[pl.ds(c * SC_ROWS, SC_ROWS)]], rows_v, sem)

    @pl.loop(0, n_tok)
    def _(i):
        t = wid * n_tok + i
        pltpu.sync_copy(idx_hbm.at[pl.ds(t * PEER_SEL, PEER_SEL)], idx_v)
        gather(0).start()
        pltpu.sync_copy(hn_hbm.at[t], h_v)

        @pl.loop(0, PEER_SEL)
        def _(k):
            part_v[pl.ds(k * SC_LANES, SC_LANES)] = jnp.zeros((SC_LANES,), F32)

        for c in range(n_chunks):
            if c + 1 < n_chunks:
                gather(c + 1).start()
            gather(c).wait()
            rows_v = bufs[c % 2][0]

            @pl.loop(0, D_MODEL // SC_COLS_U)
            def _(cb):
                cols = [pl.ds(cb * SC_COLS_U + j * SC_LANES, SC_LANES) for j in range(SC_COLS_U // SC_LANES)]
                hs = [h_v[col] for col in cols]

                @pl.loop(0, SC_ROWS // SC_ROW_GROUP)
                def _(rg):
                    for rr in range(SC_ROW_GROUP):
                        r = rg * SC_ROW_GROUP + rr
                        slot = pl.ds((c * SC_ROWS + r) * SC_LANES, SC_LANES)
                        prods = [h * rows_v[r, col] for h, col in zip(hs, cols)]
                        while len(prods) > 1:
                            prods = [a + b for a, b in zip(prods[0::2], prods[1::2])]
                        part_v[slot] = part_v[slot] + prods[0]
        pltpu.sync_copy(part_v, part_hbm.at[pl.ds(t * part_w, part_w)])


def _peer_u_sc(tab, idx_flat, hn_rows):
    n = hn_rows.shape[0]
    mesh = plsc.VectorSubcoreMesh(core_axis_name="c", subcore_axis_name="s")
    return pl.kernel(
        functools.partial(_peer_u_sc_kernel, n_tok=n // SC_WORKERS),
        out_type=jax.ShapeDtypeStruct((n * PEER_SEL * SC_LANES,), F32),
        mesh=mesh,
        scratch_types=[pltpu.VMEM((PEER_SEL,), I32), pltpu.VMEM((D_MODEL,), F32),
                       pltpu.VMEM((SC_ROWS, D_MODEL), F32), pltpu.VMEM((SC_ROWS, D_MODEL), F32),
                       pltpu.VMEM((PEER_SEL * SC_LANES,), F32), pltpu.SemaphoreType.DMA, pltpu.SemaphoreType.DMA],
        compiler_params=pltpu.CompilerParams(needs_layout_passes=False),
        name="peer_u_sc",
    )(tab, idx_flat, hn_rows)


def _fold_act_kernel(part_ref, fold_ref, gate_ref, act_ref):
    p = part_ref[...]
    hi = p.astype(BF16)
    lo = (p - hi.astype(F32)).astype(BF16)
    dots = _dot(hi, fold_ref[...]) + _dot(lo, fold_ref[...])
    act_ref[...] = _gelu(dots) * gate_ref[...]


def _fold_act(part2, gate, tm):
    n = part2.shape[0]
    fold = (jnp.arange(PEER_SEL * SC_LANES)[:, None] // SC_LANES == jnp.arange(PEER_SEL)[None, :]).astype(BF16)
    return pl.pallas_call(
        _fold_act_kernel,
        grid=(n // tm,),
        in_specs=[pl.BlockSpec((tm, PEER_SEL * SC_LANES), lambda i: (i, 0)),
                  pl.BlockSpec(fold.shape, lambda i: (0, 0)),
                  pl.BlockSpec((tm, PEER_SEL), lambda i: (i, 0))],
        out_specs=pl.BlockSpec((tm, PEER_SEL), lambda i: (i, 0)),
        out_shape=jax.ShapeDtypeStruct((n, PEER_SEL), F32),
        compiler_params=pltpu.CompilerParams(dimension_semantics=("arbitrary",)),
        name="fold_act",
    )(part2, fold, gate)


def _block(x2, meta_tokens, norm1_g, w_in, q_norm_g, k_norm_g, attn_sinks, ssm_a_re, ssm_a_im, ssm_log_dt,
           ssm_b_re, ssm_b_im, ssm_c_re, ssm_c_im, ssm_d, ssm_glu_w, ssm_glu_b, attn_out_g, ssm_out_g, w_out,
           norm2_g, peer_w_query, peer_sub_keys, peer_u, peer_v, *, tm_proj, tm_ssm, tm_mix, tt_peer):
    rows = x2.shape[0]
    row2 = lambda a: a.reshape(1, -1).astype(F32)
    w_in_bf = w_in.astype(BF16)
    seg = jnp.arange(QK_WIDTH) // HEAD_DIM
    ones_bd = (seg[:, None] == seg[None, :]).astype(BF16)
    scale = HEAD_DIM ** -0.5
    qkg = jnp.concatenate([jnp.tile(q_norm_g.astype(F32), N_Q_HEADS) * scale,
                           jnp.tile(k_norm_g.astype(F32), N_KV_HEADS)]).reshape(1, QK_WIDTH)
    bmat, cmat, coef = _ssm_params(ssm_a_re.astype(F32), ssm_a_im.astype(F32), ssm_log_dt.astype(F32),
                                   ssm_b_re.astype(F32), ssm_b_im.astype(F32), ssm_c_re.astype(F32),
                                   ssm_c_im.astype(F32))
    glu_w_bf = ssm_glu_w.astype(BF16)
    ssm_args = (row2(ssm_d), glu_w_bf, row2(ssm_glu_b), row2(ssm_out_g))

    h0 = jnp.concatenate([jnp.zeros((BLOCK - N_META, D_MODEL), F32), meta_tokens.astype(F32)], axis=0)
    _, k0, vt0, u0 = _in_proj(h0, row2(norm1_g), w_in_bf, ones_bd, qkg, BLOCK)
    carry0 = jnp.zeros((2, 2, SUBLANES, SSM_HALF_STATES), F32)
    _, carry = _ssm(u0, bmat, cmat, coef, carry0, *ssm_args, BLOCK)

    qt, k, vt, u = _in_proj(x2, row2(norm1_g), w_in_bf, ones_bd, qkg, tm_proj)
    g_attn = jnp.broadcast_to(attn_out_g.astype(F32)[:, None], (ATTN_WIDTH, BLOCK))
    attn_n = _attention(attn_sinks.astype(F32), qt, k, vt, k0[BLOCK - N_META:], vt0[:, BLOCK - N_META:], g_attn)
    ssm_n, _ = _ssm(u, bmat, cmat, coef, carry, *ssm_args, tm_ssm)

    h1, hn, idx, gate = _mix(x2, attn_n, ssm_n, w_out.astype(BF16), row2(norm2_g), peer_w_query.astype(BF16),
                             peer_sub_keys.astype(BF16), tm_mix)
    unit = math.lcm(SC_WORKERS, tt_peer, tm_mix)
    share = lambda num, den: rows * num // den // unit * unit
    n_u, n_v = share(*SC_SHARE_U), share(*SC_SHARE_V)
    part = _peer_u_sc(peer_u.astype(F32), idx[:n_u].reshape(-1), hn[:n_u].reshape(n_u, D_MODEL))
    act_tc = _peer_u(idx, hn, gate.reshape(rows, 1, PEER_SEL), _expert_tiles(peer_u), tt_peer, n_u)
    act_sc = _fold_act(part.reshape(n_u, PEER_SEL * SC_LANES), gate[:n_u], tm_mix)
    act = jnp.concatenate([act_sc.reshape(n_u, 1, PEER_SEL), act_tc], axis=0)
    sc_out = _peer_v_sc(peer_v.astype(F32), idx[:n_v].reshape(-1), act[:n_v].reshape(-1),
                        h1[:n_v].reshape(n_v, D_MODEL))
    tc_out = _peer_v(idx, act, h1, _expert_tiles(peer_v), tt_peer, n_v)
    return jnp.concatenate([sc_out, tc_out], axis=0)


def kernel(x, meta_tokens, norm1_g, w_in, q_norm_g, k_norm_g, attn_sinks, ssm_a_re, ssm_a_im, ssm_log_dt, ssm_b_re, ssm_b_im, ssm_c_re, ssm_c_im, ssm_d, ssm_glu_w, ssm_glu_b, attn_out_g, ssm_out_g, w_out, norm2_g, peer_w_query, peer_sub_keys, peer_u, peer_v):
    b, seq, d = x.shape
    outs = []
    for bi in range(b):
        outs.append(_block(x[bi].astype(F32), meta_tokens, norm1_g[0], w_in[0], q_norm_g[0], k_norm_g[0],
                           attn_sinks[0], ssm_a_re[0], ssm_a_im[0], ssm_log_dt[0], ssm_b_re[0], ssm_b_im[0],
                           ssm_c_re[0], ssm_c_im[0], ssm_d[0], ssm_glu_w[0], ssm_glu_b[0], attn_out_g[0],
                           ssm_out_g[0], w_out[0], norm2_g[0], peer_w_query[0], peer_sub_keys[0], peer_u[0],
                           peer_v[0], tm_proj=512, tm_ssm=256, tm_mix=256, tt_peer=64))
    return jnp.stack(outs).astype(x.dtype)
```

```python
import functools
import math

import jax
import jax.numpy as jnp
from jax import lax
from jax.experimental import pallas as pl
from jax.experimental.pallas import tpu as pltpu
from jax.experimental.pallas import tpu_sc as plsc

F32 = jnp.float32
BF16 = jnp.bfloat16
I32 = jnp.int32

D_MODEL = 1024
N_META = 16
BLOCK = 128
ATTN_WIDTH = 512
HEAD_DIM = 64
N_Q_HEADS = 8
N_KV_HEADS = 2
GQA = 4
KV_WIDTH = 128
QK_WIDTH = ATTN_WIDTH + KV_WIDTH
SSM_WIDTH = 512
SSM_CH = 16
SSM_GROUPS = 32
SSM_STATE = 64
IN_WIDTH = ATTN_WIDTH + 2 * KV_WIDTH + SSM_WIDTH
PEER_HEADS = 8
PEER_N_KEYS = 128
PEER_TOPK = 16
PEER_DK = 256
PEER_HALF = 128
PEER_SEL = PEER_HEADS * PEER_TOPK
NORM_EPS = 1e-6
MASK_VALUE = -1e30

SUBLANES = 8
LANES = 128
SSM_HALF_W = SSM_WIDTH // 2
SSM_HALF_STATES = SSM_GROUPS // 2 * SSM_STATE
HALF_TILE = SUBLANES // 2
PEER_V_GROUP = 2
VMEM_LIMIT_TABLE = 52 * 1024 * 1024


def _dot(a, b):
    return jnp.dot(a, b, preferred_element_type=F32)


def _dot_nt(a, b):
    return lax.dot_general(a, b, (((1,), (1,)), ((), ())), preferred_element_type=F32)


def _gelu(x):
    return 0.5 * x * (1.0 + lax.erf(x * (2.0 ** -0.5)))


def _in_proj_kernel(x_ref, g1_ref, w_ref, ones_ref, qkg_ref, qt_ref, k_ref, vt_ref, u_ref):
    x = x_ref[...]
    ms = jnp.mean(x * x, axis=-1, keepdims=True)
    y = (x * lax.rsqrt(ms + NORM_EPS) * g1_ref[...]).astype(BF16)
    proj = _dot(y, w_ref[...])
    qk = proj[:, :QK_WIDTH]
    ss = _dot((qk * qk).astype(BF16), ones_ref[...]) * (1.0 / HEAD_DIM)
    qkn = qk * lax.rsqrt(ss + NORM_EPS) * qkg_ref[...]
    qt_ref[...] = qkn[:, :ATTN_WIDTH].T.astype(BF16)
    k_ref[...] = qkn[:, ATTN_WIDTH:].astype(BF16)
    vt_ref[...] = proj[:, QK_WIDTH:QK_WIDTH + KV_WIDTH].T.astype(BF16)
    u_ref[...] = proj[:, QK_WIDTH + KV_WIDTH:]


def _in_proj(x, g1, w_in_bf, ones_bd, qkg, tm):
    rows = x.shape[0]
    row_spec = lambda w: pl.BlockSpec((tm, w), lambda i: (i, 0))
    col_spec = lambda w: pl.BlockSpec((w, tm), lambda i: (0, i))
    full = lambda a: pl.BlockSpec(a.shape, lambda i: (0,) * a.ndim)
    return pl.pallas_call(
        _in_proj_kernel,
        grid=(rows // tm,),
        in_specs=[row_spec(D_MODEL), full(g1), full(w_in_bf), full(ones_bd), full(qkg)],
        out_specs=[col_spec(ATTN_WIDTH), row_spec(KV_WIDTH), col_spec(KV_WIDTH), row_spec(SSM_WIDTH)],
        out_shape=[jax.ShapeDtypeStruct((ATTN_WIDTH, rows), BF16),
                   jax.ShapeDtypeStruct((rows, KV_WIDTH), BF16),
                   jax.ShapeDtypeStruct((KV_WIDTH, rows), BF16),
                   jax.ShapeDtypeStruct((rows, SSM_WIDTH), F32)],
        compiler_params=pltpu.CompilerParams(dimension_semantics=("arbitrary",)),
        name="in_proj",
    )(x, g1, w_in_bf, ones_bd, qkg)


def _attn_kernel(sink_ref, qt_ref, kp_ref, kc_ref, vtp_ref, vtc_ref, km_ref, vtm_ref, g_ref, o_ref, acc_ref):
    i = pl.program_id(0)
    group_w = GQA * BLOCK
    key = lax.broadcasted_iota(I32, (BLOCK, group_w), 0)
    qry = lax.broadcasted_iota(I32, (BLOCK, group_w), 1) & (BLOCK - 1)
    prev_ok = jnp.logical_and(key > qry, i > 0)
    cur_ok = key <= qry
    for kv in range(N_KV_HEADS):
        hs = slice(kv * HEAD_DIM, (kv + 1) * HEAD_DIM)
        heads = range(kv * GQA, (kv + 1) * GQA)
        qt = jnp.concatenate([qt_ref[h * HEAD_DIM:(h + 1) * HEAD_DIM, :] for h in heads], axis=1)
        sink = jnp.concatenate([jnp.full((1, BLOCK), sink_ref[h], F32) for h in heads], axis=1)
        sp = jnp.where(prev_ok, _dot(kp_ref[:, hs], qt), MASK_VALUE)
        sc = jnp.where(cur_ok, _dot(kc_ref[:, hs], qt), MASK_VALUE)
        sm = _dot(km_ref[:, hs], qt)
        m = jnp.maximum(jnp.maximum(jnp.max(sp, axis=0, keepdims=True), jnp.max(sc, axis=0, keepdims=True)),
                        jnp.maximum(jnp.max(sm, axis=0, keepdims=True), sink))
        pp, pc, pm = jnp.exp(sp - m), jnp.exp(sc - m), jnp.exp(sm - m)
        denom = (jnp.sum(pp, axis=0, keepdims=True) + jnp.sum(pc, axis=0, keepdims=True)
                 + jnp.sum(pm, axis=0, keepdims=True) + jnp.exp(sink - m))
        ot = (_dot(vtp_ref[hs, :], pp.astype(BF16)) + _dot(vtc_ref[hs, :], pc.astype(BF16))
              + _dot(vtm_ref[hs, :], pm.astype(BF16))) / denom
        for g, h in enumerate(heads):
            acc_ref[h * HEAD_DIM:(h + 1) * HEAD_DIM, :] = ot[:, g * BLOCK:(g + 1) * BLOCK]
    a = acc_ref[...]
    ms = jnp.mean(a * a, axis=0, keepdims=True)
    o_ref[...] = (a * lax.rsqrt(ms + NORM_EPS) * g_ref[...]).T.astype(BF16)


def _attention(sinks, qt, k, vt, k_meta, vt_meta, g_out):
    nb = k.shape[0] // BLOCK
    prev_i = lambda i: jnp.maximum(i - 1, 0)
    full = lambda a: pl.BlockSpec(a.shape, lambda i: (0,) * a.ndim)
    return pl.pallas_call(
        _attn_kernel,
        grid=(nb,),
        in_specs=[pl.BlockSpec(memory_space=pltpu.SMEM),
                  pl.BlockSpec((ATTN_WIDTH, BLOCK), lambda i: (0, i)),
                  pl.BlockSpec((BLOCK, KV_WIDTH), lambda i: (prev_i(i), 0)),
                  pl.BlockSpec((BLOCK, KV_WIDTH), lambda i: (i, 0)),
                  pl.BlockSpec((KV_WIDTH, BLOCK), lambda i: (0, prev_i(i))),
                  pl.BlockSpec((KV_WIDTH, BLOCK), lambda i: (0, i)),
                  full(k_meta), full(vt_meta), full(g_out)],
        out_specs=pl.BlockSpec((BLOCK, ATTN_WIDTH), lambda i: (i, 0)),
        out_shape=jax.ShapeDtypeStruct((k.shape[0], ATTN_WIDTH), BF16),
        scratch_shapes=[pltpu.VMEM((ATTN_WIDTH, BLOCK), F32)],
        compiler_params=pltpu.CompilerParams(dimension_semantics=("arbitrary",)),
        name="attention",
    )(sinks, qt, k, k, vt, vt, k_meta, vt_meta, g_out)


def _ssm_kernel(u_ref, bmat_ref, cmat_ref, coef_ref, carry_in_ref, d_ref, gw_ref, gb_ref, g_ref,
                o_ref, carry_out_ref, st_ref, carry_ref, y_ref, *, tm):
    @pl.when(pl.program_id(0) == 0)
    def _():
        carry_ref[...] = carry_in_ref[...]

    ns = SSM_HALF_STATES
    for h in range(2):
        uh = u_ref[:, h * SSM_HALF_W:(h + 1) * SSM_HALF_W].astype(BF16)
        st_ref[h] = _dot(uh, bmat_ref[h])

    def block(b, carry):
        r0 = pl.multiple_of(b * SUBLANES, SUBLANES)
        out = []
        for h in range(2):
            cr, ci = carry[2 * h], carry[2 * h + 1]
            xr = st_ref[h, pl.ds(r0, SUBLANES), 0:ns]
            xi = st_ref[h, pl.ds(r0, SUBLANES), ns:2 * ns]
            for j, shift in enumerate((1, 2, 4)):
                ar, ai = coef_ref[h, j, 0], coef_ref[h, j, 1]
                sr, si = pltpu.roll(xr, shift, 0), pltpu.roll(xi, shift, 0)
                xr, xi = xr + ar * sr - ai * si, xi + ar * si + ai * sr
            pr, pi = coef_ref[h, 3, 0], coef_ref[h, 3, 1]
            xr, xi = xr + pr * cr - pi * ci, xi + pr * ci + pi * cr
            st_ref[h, pl.ds(r0, SUBLANES), 0:ns] = xr
            st_ref[h, pl.ds(r0, SUBLANES), ns:2 * ns] = xi
            out += [jnp.broadcast_to(xr[SUBLANES - 1:SUBLANES, :], (SUBLANES, ns)),
                    jnp.broadcast_to(xi[SUBLANES - 1:SUBLANES, :], (SUBLANES, ns))]
        return tuple(out)

    carry = lax.fori_loop(0, tm // SUBLANES, block,
                          (carry_ref[0, 0], carry_ref[0, 1], carry_ref[1, 0], carry_ref[1, 1]))
    for h in range(2):
        carry_ref[h, 0] = carry[2 * h]
        carry_ref[h, 1] = carry[2 * h + 1]
        y_ref[:, h * SSM_HALF_W:(h + 1) * SSM_HALF_W] = _dot(st_ref[h].astype(BF16), cmat_ref[h])

    carry_out_ref[...] = carry_ref[...]
    z = _gelu(y_ref[...] + d_ref[...] * u_ref[...])
    z = z * jax.nn.sigmoid(_dot(z.astype(BF16), gw_ref[...]) + gb_ref[...])
    ms = jnp.mean(z * z, axis=-1, keepdims=True)
    o_ref[...] = (z * lax.rsqrt(ms + NORM_EPS) * g_ref[...]).astype(BF16)


def _ssm(u, bmat, cmat, coef, carry_in, d, gw_bf, gb, g_out, tm):
    rows = u.shape[0]
    row_spec = pl.BlockSpec((tm, SSM_WIDTH), lambda i: (i, 0))
    full = lambda a: pl.BlockSpec(a.shape, lambda i: (0,) * a.ndim)
    return pl.pallas_call(
        functools.partial(_ssm_kernel, tm=tm),
        grid=(rows // tm,),
        in_specs=[row_spec, full(bmat), full(cmat), full(coef), full(carry_in), full(d), full(gw_bf), full(gb),
                  full(g_out)],
        out_specs=[row_spec, full(carry_in)],
        out_shape=[jax.ShapeDtypeStruct((rows, SSM_WIDTH), BF16),
                   jax.ShapeDtypeStruct(carry_in.shape, F32)],
        scratch_shapes=[pltpu.VMEM((2, tm, 2 * SSM_HALF_STATES), F32),
                        pltpu.VMEM(carry_in.shape, F32),
                        pltpu.VMEM((tm, SSM_WIDTH), F32)],
        compiler_params=pltpu.CompilerParams(dimension_semantics=("arbitrary",)),
        name="ssm",
    )(u, bmat, cmat, coef, carry_in, d, gw_bf, gb, g_out)


def _ssm_params(a_re, a_im, log_dt, b_re, b_im, c_re, c_im):
    dt = jnp.exp(log_dt)[:, None]
    mag = jnp.exp(a_re * dt)
    abar_r, abar_i = mag * jnp.cos(a_im * dt), mag * jnp.sin(a_im * dt)
    den = a_re * a_re + a_im * a_im
    nr, ni = abar_r - 1.0, abar_i
    coef_r = (nr * a_re + ni * a_im) / den
    coef_i = (ni * a_re - nr * a_im) / den
    bbar_r = coef_r[..., None] * b_re - coef_i[..., None] * b_im
    bbar_i = coef_r[..., None] * b_im + coef_i[..., None] * b_re
    gh = SSM_GROUPS // 2
    eye = jnp.eye(gh, dtype=F32)

    def half_b(bb):
        return jnp.einsum('gnp,gk->gpkn', bb, eye).reshape(gh * SSM_CH, gh * SSM_STATE)

    def half_c(cc):
        return jnp.einsum('gpn,gk->gnkp', cc, eye).reshape(gh * SSM_STATE, gh * SSM_CH)

    bmat = jnp.stack([jnp.concatenate([half_b(bbar_r[s]), half_b(bbar_i[s])], axis=1)
                      for s in (slice(0, gh), slice(gh, None))]).astype(BF16)
    cmat = jnp.stack([jnp.concatenate([half_c(c_re[s]), -half_c(c_im[s])], axis=0)
                      for s in (slice(0, gh), slice(gh, None))]).astype(BF16)

    def cpow(k):
        m = mag ** k
        return m * jnp.cos(a_im * dt * k), m * jnp.sin(a_im * dt * k)

    rows = jnp.arange(SUBLANES)
    coefs = []
    for shift in (1, 2, 4):
        pr, pi = cpow(float(shift))
        keep = (rows >= shift).astype(F32)[:, None, None]
        coefs.append(jnp.stack([keep * pr[None], keep * pi[None]]))
    pw = (rows + 1).astype(F32)[:, None, None]
    m = mag[None] ** pw
    coefs.append(jnp.stack([m * jnp.cos(a_im[None] * dt[None] * pw), m * jnp.sin(a_im[None] * dt[None] * pw)]))
    coef = jnp.stack(coefs)
    coef = coef.reshape(4, 2, SUBLANES, 2, SSM_HALF_STATES).transpose(3, 0, 1, 2, 4)
    return bmat, cmat, coef


def _top16(s, payload, n_rows):
    rowid = lax.broadcasted_iota(I32, s.shape, 0).astype(F32)
    vals, pays = [], []
    for _ in range(PEER_TOPK):
        m = jnp.max(s, axis=0, keepdims=True)
        pos = jnp.min(jnp.where(s == m, rowid, float(n_rows)), axis=0, keepdims=True)
        sel = rowid == pos
        vals.append(m)
        pays.append(pos if payload is None else jnp.max(jnp.where(sel, payload, -1.0), axis=0, keepdims=True))
        s = jnp.where(sel, -jnp.inf, s)
    return jnp.concatenate(vals, axis=0), jnp.concatenate(pays, axis=0)


_CAND_PAIRS = [(a, b) for a in range(PEER_TOPK) for b in range(PEER_TOPK) if (a + 1) * (b + 1) <= PEER_TOPK]
_CAND_ROWS = -(-len(_CAND_PAIRS) // SUBLANES) * SUBLANES


def _mix_kernel(x_ref, a_ref, s_ref, wo_ref, g2_ref, wq_ref, keys_ref, h1_ref, hn_ref, idx_ref, gate_ref,
                qp_ref, idx_t_ref, gate_t_ref, *, tm):
    mix = _dot(a_ref[...], wo_ref[0:ATTN_WIDTH, :]) + _dot(s_ref[...], wo_ref[ATTN_WIDTH:, :])
    h1 = x_ref[...] + mix
    h1_ref[...] = h1.reshape(tm, SUBLANES, LANES)
    ms = jnp.mean(h1 * h1, axis=-1, keepdims=True)
    hn = h1 * lax.rsqrt(ms + NORM_EPS) * g2_ref[...]
    hn_ref[...] = hn.reshape(tm, SUBLANES, LANES)
    qp_ref[...] = _dot(hn.astype(BF16), wq_ref[...])

    def retrieve(h, tok):
        c0 = pl.multiple_of(h * PEER_DK, PEER_DK)
        q1 = qp_ref[tok, pl.ds(c0, PEER_HALF)].astype(BF16)
        q2 = qp_ref[tok, pl.ds(c0 + PEER_HALF, PEER_HALF)].astype(BF16)
        s1 = _dot_nt(keys_ref[h, 0], q1)
        s2 = _dot_nt(keys_ref[h, 1], q2)
        v1, i1 = _top16(s1, None, PEER_N_KEYS)
        v2, i2 = _top16(s2, None, PEER_N_KEYS)
        pad = _CAND_ROWS - len(_CAND_PAIRS)
        cand = jnp.concatenate([v1[a:a + 1, :] + v2[b:b + 1, :] for a, b in _CAND_PAIRS]
                               + [jnp.full((pad, LANES), -jnp.inf, F32)], axis=0)
        eidx = jnp.concatenate([i1[a:a + 1, :] * float(PEER_N_KEYS) + i2[b:b + 1, :] for a, b in _CAND_PAIRS]
                               + [jnp.zeros((pad, LANES), F32)], axis=0)
        sc, ei = _top16(cand, eidx, _CAND_ROWS)
        p = jnp.exp(sc - sc[0:1, :])
        r0 = pl.multiple_of(h * PEER_TOPK, PEER_TOPK)
        idx_t_ref[pl.ds(r0, PEER_TOPK), tok] = ei.astype(I32)
        gate_t_ref[pl.ds(r0, PEER_TOPK), tok] = p / jnp.sum(p, axis=0, keepdims=True)

    def head_pair(i, _):
        for h in (2 * i, 2 * i + 1):
            for sub in range(tm // LANES):
                retrieve(h, slice(sub * LANES, (sub + 1) * LANES))
        return 0

    lax.fori_loop(0, PEER_HEADS // 2, head_pair, 0)
    idx_ref[...] = idx_t_ref[...].T
    gate_ref[...] = gate_t_ref[...].T


def _mix(x, attn_n, ssm_n, w_out_bf, g2, wq_bf, keys_bf, tm):
    rows = x.shape[0]
    row_spec = lambda w: pl.BlockSpec((tm, w), lambda i: (i, 0))
    tile_spec = pl.BlockSpec((tm, SUBLANES, LANES), lambda i: (i, 0, 0))
    full = lambda a: pl.BlockSpec(a.shape, lambda i: (0,) * a.ndim)
    return pl.pallas_call(
        functools.partial(_mix_kernel, tm=tm),
        grid=(rows // tm,),
        in_specs=[row_spec(D_MODEL), row_spec(ATTN_WIDTH), row_spec(SSM_WIDTH), full(w_out_bf), full(g2),
                  full(wq_bf), full(keys_bf)],
        out_specs=[tile_spec, tile_spec, row_spec(PEER_SEL), row_spec(PEER_SEL)],
        out_shape=[jax.ShapeDtypeStruct((rows, SUBLANES, LANES), F32),
                   jax.ShapeDtypeStruct((rows, SUBLANES, LANES), F32),
                   jax.ShapeDtypeStruct((rows, PEER_SEL), I32),
                   jax.ShapeDtypeStruct((rows, PEER_SEL), F32)],
        scratch_shapes=[pltpu.VMEM((tm, PEER_HEADS * PEER_DK), F32),
                        pltpu.VMEM((PEER_SEL, tm), I32),
                        pltpu.VMEM((PEER_SEL, tm), F32)],
        compiler_params=pltpu.CompilerParams(dimension_semantics=("arbitrary",),
                                             vmem_limit_bytes=48 * 1024 * 1024),
        name="mix_topk",
    )(x, attn_n, ssm_n, w_out_bf, g2, wq_bf, keys_bf)


def _expert_tiles(t):
    return t.astype(BF16).reshape(t.shape[0], SUBLANES, LANES)


def _load_table(tab_hbm, tab_ref, sem):
    @pl.when(pl.program_id(0) == 0)
    def _():
        cp = pltpu.make_async_copy(tab_hbm, tab_ref, sem)
        cp.start()
        cp.wait()


def _peer_u_kernel(idx_ref, h_ref, gate_ref, tab_hbm, act_ref, tab_ref, part_a_ref, part_b_ref, sem, *, tt):
    _load_table(tab_hbm, tab_ref, sem)

    def products(t0, part_refs):
        toks = [(idx_ref.at[t0 + j], h_ref[t0 + j], part_refs[j]) for j in range(len(part_refs))]
        for k in range(PEER_SEL):
            for row, hv, part_ref in toks:
                p = tab_ref[row[k]].astype(F32) * hv
                part_ref[k * HALF_TILE:(k + 1) * HALF_TILE, :] = p[0:HALF_TILE] + p[HALF_TILE:]

    def row_sums(part_ref):
        part = part_ref[pl.ds(0, PEER_SEL, stride=HALF_TILE), :]
        for r in range(1, HALF_TILE):
            part = part + part_ref[pl.ds(r, PEER_SEL, stride=HALF_TILE), :]
        return part

    def finish(t, part):
        dots = jnp.sum(part.T, axis=0, keepdims=True)
        act_ref[t] = _gelu(dots) * gate_ref[t]

    products(0, (part_a_ref, part_b_ref))

    def pair(i, _):
        t = 2 * i
        sums_a, sums_b = row_sums(part_a_ref), row_sums(part_b_ref)
        finish(t - 2, sums_a)
        finish(t - 1, sums_b)
        products(t, (part_a_ref, part_b_ref))
        return 0

    lax.fori_loop(1, tt // 2, pair, 0)
    finish(tt - 2, row_sums(part_a_ref))
    finish(tt - 1, row_sums(part_b_ref))


def _peer_u(idx, hn3, gate3, tab, tt, first):
    rows = idx.shape[0]
    b0 = first // tt
    return pl.pallas_call(
        functools.partial(_peer_u_kernel, tt=tt),
        grid=((rows - first) // tt,),
        in_specs=[pl.BlockSpec((tt, PEER_SEL), lambda i: (i + b0, 0), memory_space=pltpu.SMEM),
                  pl.BlockSpec((tt, SUBLANES, LANES), lambda i: (i + b0, 0, 0)),
                  pl.BlockSpec((tt, 1, PEER_SEL), lambda i: (i + b0, 0, 0)),
                  pl.BlockSpec(memory_space=pl.ANY)],
        out_specs=pl.BlockSpec((tt, 1, PEER_SEL), lambda i: (i, 0, 0)),
        out_shape=jax.ShapeDtypeStruct((rows - first, 1, PEER_SEL), F32),
        scratch_shapes=[pltpu.VMEM(tab.shape, BF16),
                        pltpu.VMEM((PEER_SEL * HALF_TILE, LANES), F32),
                        pltpu.VMEM((PEER_SEL * HALF_TILE, LANES), F32),
                        pltpu.SemaphoreType.DMA],
        compiler_params=pltpu.CompilerParams(dimension_semantics=("arbitrary",),
                                             vmem_limit_bytes=VMEM_LIMIT_TABLE),
        name="peer_u",
    )(idx, hn3, gate3, tab)


def _peer_v_kernel(idx_ref, act_ref, h1_ref, tab_hbm, o_ref, tab_ref, tiles_ref, *rest, tt):
    rep_refs, sem = rest[:-1], rest[-1]
    _load_table(tab_hbm, tab_ref, sem)
    n_acc = 2
    g = PEER_V_GROUP

    def replicated(t):
        return jnp.broadcast_to(act_ref[t], (PEER_SEL, LANES)).T

    def accumulate(t0):
        rows = [idx_ref.at[t0 + j] for j in range(g)]
        accs = [[jnp.zeros((SUBLANES, LANES), F32)] * n_acc for _ in range(g)]
        for k in range(PEER_SEL):
            for j, rep_ref in enumerate(rep_refs):
                w = jnp.broadcast_to(rep_ref[k:k + 1, :], (SUBLANES, LANES))
                accs[j][k % n_acc] = accs[j][k % n_acc] + w * tab_ref[rows[j][k]].astype(F32)
        for j in range(g):
            tiles_ref[t0 + j] = h1_ref[t0 + j] + sum(accs[j][1:], accs[j][0])

    for j, rep_ref in enumerate(rep_refs):
        rep_ref[...] = replicated(j)

    def group(i, _):
        t = g * i
        nxt = [replicated(jnp.minimum(t + g + j, tt - 1)) for j in range(g)]
        accumulate(t)
        for rep_ref, tile in zip(rep_refs, nxt):
            rep_ref[...] = tile
        return 0

    lax.fori_loop(0, tt // g, group, 0)
    o_ref[...] = tiles_ref[...].reshape(tt, D_MODEL)


def _peer_v(idx, act3, h13, tab, tt, first):
    rows = idx.shape[0]
    b0 = first // tt
    tile = pl.BlockSpec((tt, SUBLANES, LANES), lambda i: (i + b0, 0, 0))
    return pl.pallas_call(
        functools.partial(_peer_v_kernel, tt=tt),
        grid=((rows - first) // tt,),
        in_specs=[pl.BlockSpec((tt, PEER_SEL), lambda i: (i + b0, 0), memory_space=pltpu.SMEM),
                  pl.BlockSpec((tt, 1, PEER_SEL), lambda i: (i + b0, 0, 0)),
                  tile, pl.BlockSpec(memory_space=pl.ANY)],
        out_specs=pl.BlockSpec((tt, D_MODEL), lambda i: (i, 0)),
        out_shape=jax.ShapeDtypeStruct((rows - first, D_MODEL), F32),
        scratch_shapes=[pltpu.VMEM(tab.shape, BF16), pltpu.VMEM((tt, SUBLANES, LANES), F32)]
                       + [pltpu.VMEM((PEER_SEL, LANES), F32)] * PEER_V_GROUP
                       + [pltpu.SemaphoreType.DMA],
        compiler_params=pltpu.CompilerParams(dimension_semantics=("arbitrary",),
                                             vmem_limit_bytes=VMEM_LIMIT_TABLE),
        name="peer_v",
    )(idx, act3, h13, tab)


SC_LANES = 16
SC_WORKERS = 32
SC_ROWS = 32
SC_COLS = 256
SC_COLS_U = 256
SC_ROW_GROUP = 4
SC_SHARE_U = (17, 64)
SC_SHARE_V = (23, 64)


def _peer_v_sc_kernel(tab_hbm, idx_hbm, act_hbm, h1_hbm, out_hbm, idx_v, act_v, rows0_v, rows1_v, acc_v, sem0, sem1,
                      *, n_tok):
    wid = lax.axis_index("s") * 2 + lax.axis_index("c")
    bufs = ((rows0_v, sem0), (rows1_v, sem1))
    n_chunks = PEER_SEL // SC_ROWS

    def gather(c):
        rows_v, sem = bufs[c % 2]
        return pltpu.make_async_copy(tab_hbm.at[idx_v.at[pl.ds(c * SC_ROWS, SC_ROWS)]], rows_v, sem)

    @pl.loop(0, n_tok)
    def _(i):
        t = wid * n_tok + i
        pltpu.sync_copy(idx_hbm.at[pl.ds(t * PEER_SEL, PEER_SEL)], idx_v)
        gather(0).start()
        pltpu.sync_copy(act_hbm.at[pl.ds(t * PEER_SEL, PEER_SEL)], act_v)
        pltpu.sync_copy(h1_hbm.at[t], acc_v)
        for c in range(n_chunks):
            if c + 1 < n_chunks:
                gather(c + 1).start()
            gather(c).wait()
            rows_v = bufs[c % 2][0]

            @pl.loop(0, D_MODEL // SC_COLS)
            def _(cb):
                cols = [pl.ds(cb * SC_COLS + j * SC_LANES, SC_LANES) for j in range(SC_COLS // SC_LANES)]

                @pl.loop(0, SC_ROWS // SC_ROW_GROUP)
                def _(rg):
                    accs = [acc_v[col] for col in cols]
                    for rr in range(SC_ROW_GROUP):
                        r = rg * SC_ROW_GROUP + rr
                        w = plsc.load_gather(act_v, [jnp.full((SC_LANES,), c * SC_ROWS, I32) + r])
                        accs = [a + w * rows_v[r, col] for a, col in zip(accs, cols)]
                    for a, col in zip(accs, cols):
                        acc_v[col] = a
        pltpu.sync_copy(acc_v, out_hbm.at[t])


def _peer_v_sc(tab, idx_flat, act_flat, h1_rows):
    n = h1_rows.shape[0]
    mesh = plsc.VectorSubcoreMesh(core_axis_name="c", subcore_axis_name="s")
    return pl.kernel(
        functools.partial(_peer_v_sc_kernel, n_tok=n // SC_WORKERS),
        out_type=jax.ShapeDtypeStruct((n, D_MODEL), F32),
        mesh=mesh,
        scratch_types=[pltpu.VMEM((PEER_SEL,), I32), pltpu.VMEM((PEER_SEL,), F32),
                       pltpu.VMEM((SC_ROWS, D_MODEL), F32), pltpu.VMEM((SC_ROWS, D_MODEL), F32),
                       pltpu.VMEM((D_MODEL,), F32), pltpu.SemaphoreType.DMA, pltpu.SemaphoreType.DMA],
        compiler_params=pltpu.CompilerParams(needs_layout_passes=False),
        name="peer_v_sc",
    )(tab, idx_flat, act_flat, h1_rows)


def _peer_u_sc_kernel(tab_hbm, idx_hbm, hn_hbm, part_hbm, idx_v, h_v, rows0_v, rows1_v, part_v, sem0, sem1, *, n_tok):
    wid = lax.axis_index("s") * 2 + lax.axis_index("c")
    bufs = ((rows0_v, sem0), (rows1_v, sem1))
    n_chunks = PEER_SEL // SC_ROWS
    part_w = PEER_SEL * SC_LANES

    def gather(c):
        rows_v, sem = bufs[c % 2]
        return pltpu.make_async_copy(tab_hbm.at[idx_v.at[pl.ds(c * SC_ROWS, SC_ROWS)]], rows_v, sem)

    @pl.loop(0, n_tok)
    def _(i):
        t = wid * n_tok + i
        pltpu.sync_copy(idx_hbm.at[pl.ds(t * PEER_SEL, PEER_SEL)], idx_v)
        gather(0).start()
        pltpu.sync_copy(hn_hbm.at[t], h_v)

        @pl.loop(0, PEER_SEL)
        def _(k):
            part_v[pl.ds(k * SC_LANES, SC_LANES)] = jnp.zeros((SC_LANES,), F32)

        for c in range(n_chunks):
            if c + 1 < n_chunks:
                gather(c + 1).start()
            gather(c).wait()
            rows_v = bufs[c % 2][0]

            @pl.loop(0, D_MODEL // SC_COLS_U)
            def _(cb):
                cols = [pl.ds(cb * SC_COLS_U + j * SC_LANES, SC_LANES) for j in range(SC_COLS_U // SC_LANES)]
                hs = [h_v[col] for col in cols]

                @pl.loop(0, SC_ROWS // SC_ROW_GROUP)
                def _(rg):
                    for rr in range(SC_ROW_GROUP):
                        r = rg * SC_ROW_GROUP + rr
                        slot = pl.ds((c * SC_ROWS + r) * SC_LANES, SC_LANES)
                        prods = [h * rows_v[r, col] for h, col in zip(hs, cols)]
                        while len(prods) > 1:
                            prods = [a + b for a, b in zip(prods[0::2], prods[1::2])]
                        part_v[slot] = part_v[slot] + prods[0]
        pltpu.sync_copy(part_v, part_hbm.at[pl.ds(t * part_w, part_w)])


def _peer_u_sc(tab, idx_flat, hn_rows):
    n = hn_rows.shape[0]
    mesh = plsc.VectorSubcoreMesh(core_axis_name="c", subcore_axis_name="s")
    return pl.kernel(
        functools.partial(_peer_u_sc_kernel, n_tok=n // SC_WORKERS),
        out_type=jax.ShapeDtypeStruct((n * PEER_SEL * SC_LANES,), F32),
        mesh=mesh,
        scratch_types=[pltpu.VMEM((PEER_SEL,), I32), pltpu.VMEM((D_MODEL,), F32),
                       pltpu.VMEM((SC_ROWS, D_MODEL), F32), pltpu.VMEM((SC_ROWS, D_MODEL), F32),
                       pltpu.VMEM((PEER_SEL * SC_LANES,), F32), pltpu.SemaphoreType.DMA, pltpu.SemaphoreType.DMA],
        compiler_params=pltpu.CompilerParams(needs_layout_passes=False),
        name="peer_u_sc",
    )(tab, idx_flat, hn_rows)


def _fold_act_kernel(part_ref, fold_ref, gate_ref, act_ref):
    p = part_ref[...]
    hi = p.astype(BF16)
    lo = (p - hi.astype(F32)).astype(BF16)
    dots = _dot(hi, fold_ref[...]) + _dot(lo, fold_ref[...])
    act_ref[...] = _gelu(dots) * gate_ref[...]


def _fold_act(part2, gate, tm):
    n = part2.shape[0]
    fold = (jnp.arange(PEER_SEL * SC_LANES)[:, None] // SC_LANES == jnp.arange(PEER_SEL)[None, :]).astype(BF16)
    return pl.pallas_call(
        _fold_act_kernel,
        grid=(n // tm,),
        in_specs=[pl.BlockSpec((tm, PEER_SEL * SC_LANES), lambda i: (i, 0)),
                  pl.BlockSpec(fold.shape, lambda i: (0, 0)),
                  pl.BlockSpec((tm, PEER_SEL), lambda i: (i, 0))],
        out_specs=pl.BlockSpec((tm, PEER_SEL), lambda i: (i, 0)),
        out_shape=jax.ShapeDtypeStruct((n, PEER_SEL), F32),
        compiler_params=pltpu.CompilerParams(dimension_semantics=("arbitrary",)),
        name="fold_act",
    )(part2, fold, gate)


def _block(x2, meta_tokens, norm1_g, w_in, q_norm_g, k_norm_g, attn_sinks, ssm_a_re, ssm_a_im, ssm_log_dt,
           ssm_b_re, ssm_b_im, ssm_c_re, ssm_c_im, ssm_d, ssm_glu_w, ssm_glu_b, attn_out_g, ssm_out_g, w_out,
           norm2_g, peer_w_query, peer_sub_keys, peer_u, peer_v, *, tm_proj, tm_ssm, tm_mix, tt_peer):
    rows = x2.shape[0]
    row2 = lambda a: a.reshape(1, -1).astype(F32)
    w_in_bf = w_in.astype(BF16)
    seg = jnp.arange(QK_WIDTH) // HEAD_DIM
    ones_bd = (seg[:, None] == seg[None, :]).astype(BF16)
    scale = HEAD_DIM ** -0.5
    qkg = jnp.concatenate([jnp.tile(q_norm_g.astype(F32), N_Q_HEADS) * scale,
                           jnp.tile(k_norm_g.astype(F32), N_KV_HEADS)]).reshape(1, QK_WIDTH)
    bmat, cmat, coef = _ssm_params(ssm_a_re.astype(F32), ssm_a_im.astype(F32), ssm_log_dt.astype(F32),
                                   ssm_b_re.astype(F32), ssm_b_im.astype(F32), ssm_c_re.astype(F32),
                                   ssm_c_im.astype(F32))
    glu_w_bf = ssm_glu_w.astype(BF16)
    ssm_args = (row2(ssm_d), glu_w_bf, row2(ssm_glu_b), row2(ssm_out_g))

    h0 = jnp.concatenate([jnp.zeros((BLOCK - N_META, D_MODEL), F32), meta_tokens.astype(F32)], axis=0)
    _, k0, vt0, u0 = _in_proj(h0, row2(norm1_g), w_in_bf, ones_bd, qkg, BLOCK)
    carry0 = jnp.zeros((2, 2, SUBLANES, SSM_HALF_STATES), F32)
    _, carry = _ssm(u0, bmat, cmat, coef, carry0, *ssm_args, BLOCK)

    qt, k, vt, u = _in_proj(x2, row2(norm1_g), w_in_bf, ones_bd, qkg, tm_proj)
    g_attn = jnp.broadcast_to(attn_out_g.astype(F32)[:, None], (ATTN_WIDTH, BLOCK))
    attn_n = _attention(attn_sinks.astype(F32), qt, k, vt, k0[BLOCK - N_META:], vt0[:, BLOCK - N_META:], g_attn)
    ssm_n, _ = _ssm(u, bmat, cmat, coef, carry, *ssm_args, tm_ssm)

    h1, hn, idx, gate = _mix(x2, attn_n, ssm_n, w_out.astype(BF16), row2(norm2_g), peer_w_query.astype(BF16),
                             peer_sub_keys.astype(BF16), tm_mix)
    unit = math.lcm(SC_WORKERS, tt_peer, tm_mix)
    share = lambda num, den: rows * num // den // unit * unit
    n_u, n_v = share(*SC_SHARE_U), share(*SC_SHARE_V)
    part = _peer_u_sc(peer_u.astype(F32), idx[:n_u].reshape(-1), hn[:n_u].reshape(n_u, D_MODEL))
    act_tc = _peer_u(idx, hn, gate.reshape(rows, 1, PEER_SEL), _expert_tiles(peer_u), tt_peer, n_u)
    act_sc = _fold_act(part.reshape(n_u, PEER_SEL * SC_LANES), gate[:n_u], tm_mix)
    act = jnp.concatenate([act_sc.reshape(n_u, 1, PEER_SEL), act_tc], axis=0)
    sc_out = _peer_v_sc(peer_v.astype(F32), idx[:n_v].reshape(-1), act[:n_v].reshape(-1),
                        h1[:n_v].reshape(n_v, D_MODEL))
    tc_out = _peer_v(idx, act, h1, _expert_tiles(peer_v), tt_peer, n_v)
    return jnp.concatenate([sc_out, tc_out], axis=0)


def kernel(x, meta_tokens, norm1_g, w_in, q_norm_g, k_norm_g, attn_sinks, ssm_a_re, ssm_a_im, ssm_log_dt, ssm_b_re, ssm_b_im, ssm_c_re, ssm_c_im, ssm_d, ssm_glu_w, ssm_glu_b, attn_out_g, ssm_out_g, w_out, norm2_g, peer_w_query, peer_sub_keys, peer_u, peer_v):
    b, seq, d = x.shape
    outs = []
    for bi in range(b):
        outs.append(_block(x[bi].astype(F32), meta_tokens, norm1_g[0], w_in[0], q_norm_g[0], k_norm_g[0],
                           attn_sinks[0], ssm_a_re[0], ssm_a_im[0], ssm_log_dt[0], ssm_b_re[0], ssm_b_im[0],
                           ssm_c_re[0], ssm_c_im[0], ssm_d[0], ssm_glu_w[0], ssm_glu_b[0], attn_out_g[0],
                           ssm_out_g[0], w_out[0], norm2_g[0], peer_w_query[0], peer_sub_keys[0], peer_u[0],
                           peer_v[0], tm_proj=512, tm_ssm=256, tm_mix=256, tt_peer=64))
    return jnp.stack(outs).astype(x.dtype)
```
